```python
import jax
import jax.numpy as jnp
from jax import lax
import numpy as np

D_MODEL = 1024
BATCH = 4
SEQ = 4096
DEPTH = 4
DEC_BATCH = 128
DEC_SEQ = 4
PAST_LEN = 2048
PAGE_SIZE = 128

N_MIXERS = 3
N_POOL_LAYERS = (DEPTH + 2) // 3
N_MOBA_LAYERS = (DEPTH + 1) // 3
N_GMLP_LAYERS = DEPTH // 3

POOL_WINDOWS = (2, 4, 8, 16)
D_POOL = D_MODEL
POOL_GC = D_POOL // len(POOL_WINDOWS)
POOL_STATE = max(POOL_WINDOWS) - 1

N_HEADS = 8
HEAD_DIM = D_MODEL // N_HEADS
MOBA_BLOCK = 256
MOBA_TOPK = 3
Q_BLOCK = 128

GMLP_CHUNK = 128
D_GMLP = D_MODEL
GMLP_GROUPS = 8
GMLP_GC = D_GMLP // GMLP_GROUPS

MOE_GROUPS = 4
MOE_PER_GROUP = 4
N_EXPERTS = MOE_GROUPS * MOE_PER_GROUP
MOE_TOPK = 2
D_EXPERT = D_MODEL // 2

RMS_EPS = 1e-6
LN_EPS = 1e-5

kernel_name = 'hybrid_pool_moba_gmlp_hmoe_step'


def _rmsnorm(x, g):
    xf = x.astype(jnp.float32)
    y = xf * lax.rsqrt(jnp.mean(xf * xf, axis=-1, keepdims=True) + RMS_EPS)
    return (y * g.astype(jnp.float32)).astype(x.dtype)


def _pool_mixer(h, prefix, pos0, w_in, w_grp, scale, w_out):
    u = h @ w_in
    B, T, _ = u.shape
    P = prefix.shape[1]
    ext = jnp.concatenate([prefix.astype(u.dtype), u], axis=1)
    cs = jnp.cumsum(ext.astype(jnp.float32), axis=1)
    cs = jnp.concatenate([jnp.zeros((B, 1, D_POOL), jnp.float32), cs], axis=1)
    pos = pos0 + jnp.arange(T, dtype=jnp.int32)
    means = []
    for g, w in enumerate(POOL_WINDOWS):
        ch = slice(g * POOL_GC, (g + 1) * POOL_GC)
        win_sum = cs[:, P + 1:P + 1 + T, ch] - cs[:, P + 1 - w:P + 1 - w + T, ch]
        count = jnp.minimum(pos + 1, w).astype(jnp.float32)[None, :, None]
        means.append(win_sum / count)
    r = jnp.concatenate(means, axis=-1) - u.astype(jnp.float32)
    r = r.astype(h.dtype).reshape(B, T, len(POOL_WINDOWS), POOL_GC)
    r = jnp.einsum('btgc,gce->btge', r, w_grp).reshape(B, T, D_POOL) * scale
    return r @ w_out, ext[:, -P:]


def _moba_attend(q, kb, vb, k_mean, q_pos):
    B, H, Q, hd = q.shape
    NB = kb.shape[1]
    n_sel = min(MOBA_TOPK, NB)
    own = q_pos // MOBA_BLOCK
    gate = jnp.einsum('bhqd,bnhd->bhqn', q.astype(jnp.float32), k_mean)
    gate = jnp.where(jnp.arange(NB)[None, :] < own[:, None], gate, -jnp.inf)
    _, top_idx = lax.top_k(gate, n_sel)
    sel = jnp.concatenate([top_idx.astype(jnp.int32),
                           jnp.broadcast_to(own[:, None], (B, H, Q, 1)).astype(jnp.int32)], axis=-1)
    key_off = jnp.arange(MOBA_BLOCK, dtype=jnp.int32)
    bi = jnp.arange(B)[:, None, None]
    hi = jnp.arange(H)[None, :, None]
    scale = hd ** -0.5
    scores = []
    for s in range(n_sel + 1):
        k_s = kb[bi, sel[..., s], :, hi]
        sc = jnp.einsum('bhqd,bhqkd->bhqk', q, k_s).astype(jnp.float32) * scale
        if s < n_sel:
            ok = (s < own)[:, None]
        else:
            ok = (own[:, None] * MOBA_BLOCK + key_off[None, :]) <= q_pos[:, None]
        scores.append(jnp.where(ok, sc, -jnp.inf))
    prob = jax.nn.softmax(jnp.concatenate(scores, axis=-1), axis=-1)
    out = jnp.zeros((B, H, Q, hd), jnp.float32)
    for s in range(n_sel + 1):
        v_s = vb[bi, sel[..., s], :, hi]
        p_s = prob[..., s * MOBA_BLOCK:(s + 1) * MOBA_BLOCK].astype(v_s.dtype)
        out = out + jnp.einsum('bhqk,bhqkd->bhqd', p_s, v_s).astype(jnp.float32)
    return out.astype(q.dtype)


def _moba_prompt(q, k, v):
    B, S, H, hd = q.shape
    nb = -(-S // MOBA_BLOCK)
    pad = ((0, 0), (0, nb * MOBA_BLOCK - S), (0, 0), (0, 0))
    kb = jnp.pad(k, pad).reshape(B, nb, MOBA_BLOCK, H, hd)
    vb = jnp.pad(v, pad).reshape(B, nb, MOBA_BLOCK, H, hd)
    k_mean = jnp.mean(kb, axis=2, dtype=jnp.float32)
    nq = S // Q_BLOCK
    qc = q.reshape(B, nq, Q_BLOCK, H, hd).transpose(1, 0, 3, 2, 4)
    starts = jnp.arange(nq, dtype=jnp.int32) * Q_BLOCK
    out = lax.map(lambda a: _moba_attend(a[0], kb, vb, k_mean,
                                         a[1] + jnp.arange(Q_BLOCK, dtype=jnp.int32)), (qc, starts))
    return out.transpose(1, 0, 3, 2, 4).reshape(B, S, H * hd)


def _moba_sample(q, k, v, cache_k, cache_v, page_table):
    B, T, H, hd = q.shape
    past = page_table.shape[1] * PAGE_SIZE
    L = past + T
    nb = -(-L // MOBA_BLOCK)
    pad = jnp.zeros((B, nb * MOBA_BLOCK - L, H, hd), k.dtype)

    def blocks(cache, new):
        past_rows = cache[page_table].reshape(B, past, H, hd).astype(new.dtype)
        return jnp.concatenate([past_rows, new, pad], axis=1).reshape(B, nb, MOBA_BLOCK, H, hd)

    kb = blocks(cache_k, k)
    vb = blocks(cache_v, v)
    k_mean = jnp.mean(kb, axis=2, dtype=jnp.float32)
    pos = past + jnp.arange(T, dtype=jnp.int32)
    out = _moba_attend(q.transpose(0, 2, 1, 3), kb, vb, k_mean, pos)
    return out.transpose(0, 2, 1, 3).reshape(B, T, H * hd)


def _gmlp_mixer(h, w_in, ln_g, ln_b, w_s, b_s, w_out):
    B, T, _ = h.shape
    z = jax.nn.gelu(h @ w_in)
    u, v = jnp.split(z, 2, axis=-1)
    vf = v.astype(jnp.float32)
    mu = jnp.mean(vf, axis=-1, keepdims=True)
    var = jnp.mean(jnp.square(vf - mu), axis=-1, keepdims=True)
    vn = ((vf - mu) * lax.rsqrt(var + LN_EPS) * ln_g.astype(jnp.float32)
          + ln_b.astype(jnp.float32)).astype(h.dtype)
    n_chunk = -(-T // GMLP_CHUNK)
    vc = jnp.pad(vn, ((0, 0), (0, n_chunk * GMLP_CHUNK - T), (0, 0)))
    vc = vc.reshape(B, n_chunk, GMLP_CHUNK, GMLP_GROUPS, GMLP_GC)
    w_causal = w_s * jnp.tril(jnp.ones((GMLP_CHUNK, GMLP_CHUNK), w_s.dtype))
    mix = jnp.einsum('gts,bnsgc->bntgc', w_causal, vc) + b_s.T[:, :, None]
    mix = mix.reshape(B, n_chunk * GMLP_CHUNK, D_GMLP)[:, :T]
    return (u * mix) @ w_out, vn


def _hier_moe(h, rg_w, rg_b, re_w, re_b, w_gate, w_up, w_down):
    B, T, D = h.shape
    hf = h.reshape(B * T, D)
    N = B * T
    h32 = hf.astype(jnp.float32)
    g_prob = jax.nn.softmax(h32 @ rg_w.astype(jnp.float32) + rg_b.astype(jnp.float32), axis=-1)
    g_p, g_idx = lax.top_k(g_prob, 1)
    e_logit = (h32 @ re_w.astype(jnp.float32) + re_b.astype(jnp.float32)).reshape(N, MOE_GROUPS, MOE_PER_GROUP)
    e_logit = jnp.take_along_axis(e_logit, jnp.broadcast_to(g_idx[:, :, None], (N, 1, MOE_PER_GROUP)), axis=1)[:, 0]
    e_p, e_idx = lax.top_k(jax.nn.softmax(e_logit, axis=-1), MOE_TOPK)
    w = g_p * e_p / jnp.sum(e_p, axis=-1, keepdims=True)
    expert = g_idx * MOE_PER_GROUP + e_idx
    gates = jnp.einsum('nk,nke->ne', w, jax.nn.one_hot(expert, N_EXPERTS, dtype=jnp.float32))
    a = jnp.einsum('nd,edf->nef', hf, w_gate)
    b = jnp.einsum('nd,edf->nef', hf, w_up)
    hid = jax.nn.silu(a) * b * gates[:, :, None].astype(a.dtype)
    return jnp.einsum('nef,efd->nd', hid, w_down).reshape(B, T, D)


def _trunk(x, pos0, pool_prefix, moba_fn, p):
    pool_states, k_rows, v_rows, gmlp_rows = [], [], [], []
    for i in range(DEPTH):
        kind, j = i % N_MIXERS, i // N_MIXERS
        h = _rmsnorm(x, p['norm_mix'][i])
        if kind == 0:
            y, st = _pool_mixer(h, pool_prefix[j], pos0, p['pool_w_in'][j], p['pool_w_grp'][j],
                                p['pool_scale'][j], p['pool_w_out'][j])
            pool_states.append(st)
        elif kind == 1:
            B, T, _ = h.shape
            q, k, v = jnp.split(h @ p['attn_w_qkv'][j], 3, axis=-1)
            q = q.reshape(B, T, N_HEADS, HEAD_DIM)
            k = k.reshape(B, T, N_HEADS, HEAD_DIM)
            v = v.reshape(B, T, N_HEADS, HEAD_DIM)
            y = moba_fn(j, q, k, v) @ p['attn_w_out'][j]
            k_rows.append(k)
            v_rows.append(v)
        else:
            y, vn = _gmlp_mixer(h, p['gmlp_w_in'][j], p['gmlp_ln_g'][j], p['gmlp_ln_b'][j],
                                p['gmlp_w_s'][j], p['gmlp_b_s'][j], p['gmlp_w_out'][j])
            gmlp_rows.append(vn)
        x = x + y
        x = x + _hier_moe(_rmsnorm(x, p['norm_ffn'][i]), p['moe_rg_w'][i], p['moe_rg_b'][i],
                          p['moe_re_w'][i], p['moe_re_b'][i], p['moe_w_gate'][i],
                          p['moe_w_up'][i], p['moe_w_down'][i])
    return (_rmsnorm(x, p['norm_final']), jnp.stack(pool_states), jnp.stack(k_rows),
            jnp.stack(v_rows), jnp.stack(gmlp_rows))


def setup_inputs(seed: int = 0) -> dict:
    key = jax.random.key(seed)
    ks = iter(jax.random.split(key, 40))

    def nrm(shape, s):
        return jax.random.normal(next(ks), shape, jnp.float32) * s

    n_pages = PAST_LEN // PAGE_SIZE
    n_phys = (DEC_BATCH * n_pages * 5) // 4
    page_table = jax.random.permutation(next(ks), n_phys)[:DEC_BATCH * n_pages]
    page_table = page_table.reshape(DEC_BATCH, n_pages).astype(jnp.int32)
    return {
        'x_prompt': nrm((BATCH, SEQ, D_MODEL), 1.0),
        'x_sample': nrm((DEC_BATCH, DEC_SEQ, D_MODEL), 1.0),
        'state_pool': nrm((N_POOL_LAYERS, DEC_BATCH, POOL_STATE, D_POOL), 1.0),
        'cache_k': nrm((N_MOBA_LAYERS, n_phys, PAGE_SIZE, N_HEADS, HEAD_DIM), 1.0),
        'cache_v': nrm((N_MOBA_LAYERS, n_phys, PAGE_SIZE, N_HEADS, HEAD_DIM), 1.0),
        'page_table': page_table,
        'norm_mix': 1.0 + nrm((DEPTH, D_MODEL), 0.01),
        'norm_ffn': 1.0 + nrm((DEPTH, D_MODEL), 0.01),
        'norm_final': 1.0 + nrm((D_MODEL,), 0.01),
        'pool_w_in': nrm((N_POOL_LAYERS, D_MODEL, D_POOL), D_MODEL ** -0.5),
        'pool_w_grp': nrm((N_POOL_LAYERS, len(POOL_WINDOWS), POOL_GC, POOL_GC), POOL_GC ** -0.5),
        'pool_scale': 1.0 + nrm((N_POOL_LAYERS, D_POOL), 0.1),
        'pool_w_out': nrm((N_POOL_LAYERS, D_POOL, D_MODEL), D_POOL ** -0.5),
        'attn_w_qkv': nrm((N_MOBA_LAYERS, D_MODEL, 3 * N_HEADS * HEAD_DIM), D_MODEL ** -0.5),
        'attn_w_out': nrm((N_MOBA_LAYERS, N_HEADS * HEAD_DIM, D_MODEL), (N_HEADS * HEAD_DIM) ** -0.5),
        'gmlp_w_in': nrm((N_GMLP_LAYERS, D_MODEL, 2 * D_GMLP), D_MODEL ** -0.5),
        'gmlp_ln_g': 1.0 + nrm((N_GMLP_LAYERS, D_GMLP), 0.01),
        'gmlp_ln_b': nrm((N_GMLP_LAYERS, D_GMLP), 0.01),
        'gmlp_w_s': nrm((N_GMLP_LAYERS, GMLP_GROUPS, GMLP_CHUNK, GMLP_CHUNK), GMLP_CHUNK ** -0.5),
        'gmlp_b_s': 1.0 + nrm((N_GMLP_LAYERS, GMLP_GROUPS, GMLP_CHUNK), 0.1),
        'gmlp_w_out': nrm((N_GMLP_LAYERS, D_GMLP, D_MODEL), D_GMLP ** -0.5),
        'moe_rg_w': nrm((DEPTH, D_MODEL, MOE_GROUPS), D_MODEL ** -0.5),
        'moe_rg_b': nrm((DEPTH, MOE_GROUPS), 0.01),
        'moe_re_w': nrm((DEPTH, D_MODEL, N_EXPERTS), D_MODEL ** -0.5),
        'moe_re_b': nrm((DEPTH, N_EXPERTS), 0.01),
        'moe_w_gate': nrm((DEPTH, N_EXPERTS, D_MODEL, D_EXPERT), D_MODEL ** -0.5),
        'moe_w_up': nrm((DEPTH, N_EXPERTS, D_MODEL, D_EXPERT), D_MODEL ** -0.5),
        'moe_w_down': nrm((DEPTH, N_EXPERTS, D_EXPERT, D_MODEL), D_EXPERT ** -0.5),
    }


def reference(x_prompt, x_sample, state_pool, cache_k, cache_v, page_table,
              norm_mix, norm_ffn, norm_final,
              pool_w_in, pool_w_grp, pool_scale, pool_w_out,
              attn_w_qkv, attn_w_out,
              gmlp_w_in, gmlp_ln_g, gmlp_ln_b, gmlp_w_s, gmlp_b_s, gmlp_w_out,
              moe_rg_w, moe_rg_b, moe_re_w, moe_re_b, moe_w_gate, moe_w_up, moe_w_down):
    p = dict(norm_mix=norm_mix, norm_ffn=norm_ffn, norm_final=norm_final,
             pool_w_in=pool_w_in, pool_w_grp=pool_w_grp, pool_scale=pool_scale, pool_w_out=pool_w_out,
             attn_w_qkv=attn_w_qkv, attn_w_out=attn_w_out,
             gmlp_w_in=gmlp_w_in, gmlp_ln_g=gmlp_ln_g, gmlp_ln_b=gmlp_ln_b, gmlp_w_s=gmlp_w_s,
             gmlp_b_s=gmlp_b_s, gmlp_w_out=gmlp_w_out,
             moe_rg_w=moe_rg_w, moe_rg_b=moe_rg_b, moe_re_w=moe_re_w, moe_re_b=moe_re_b,
             moe_w_gate=moe_w_gate, moe_w_up=moe_w_up, moe_w_down=moe_w_down)
    past_len = page_table.shape[1] * PAGE_SIZE
    prompt_prefix = jnp.zeros((N_POOL_LAYERS, x_prompt.shape[0], POOL_STATE, D_POOL), x_prompt.dtype)
    y_prompt, pool_prompt, k_prompt, v_prompt, gmlp_v_prompt = _trunk(
        x_prompt, 0, prompt_prefix, lambda j, q, k, v: _moba_prompt(q, k, v), p)
    y_sample, pool_sample, k_sample, v_sample, gmlp_v_sample = _trunk(
        x_sample, past_len, state_pool,
        lambda j, q, k, v: _moba_sample(q, k, v, cache_k[j], cache_v[j], page_table), p)
    return (y_prompt, y_sample, pool_prompt, pool_sample, k_prompt, v_prompt,
            k_sample, v_sample, gmlp_v_prompt, gmlp_v_sample)
```

```python
import functools

import jax
import jax.numpy as jnp
from jax import lax
from jax.experimental import pallas as pl
from jax.experimental.pallas import tpu as pltpu

F32 = jnp.float32
BF16 = jnp.bfloat16

RMS_EPS = 1e-6
LN_EPS = 1e-5
N_MIXERS = 3
POOL_WINDOWS = (2, 4, 8, 16)
POOL_STATE = max(POOL_WINDOWS) - 1
MOBA_BLOCK = 256
MOBA_TOPK = 3
GMLP_CHUNK = 128
GMLP_GROUPS = 8
MOE_GROUPS = 4
MOE_PER_GROUP = 4
PAGE_SIZE = 128
N_HEADS = 8

V7X_VMEM_LIMIT_BYTES = 56 * 1024 * 1024
NEG_BIG = -1e30


def _params(*sem):
    return pltpu.CompilerParams(dimension_semantics=sem, vmem_limit_bytes=V7X_VMEM_LIMIT_BYTES)


def _rms(x, g):
    return x * lax.rsqrt(jnp.mean(x * x, axis=-1, keepdims=True) + RMS_EPS) * g


def _dot(a, b):
    return jnp.dot(a, b, preferred_element_type=F32)


def _dot_nt(a, b, precision=None):
    return lax.dot_general(a, b, (((1,), (1,)), ((), ())), preferred_element_type=F32,
                           precision=precision)


def _full(shape):
    return pl.BlockSpec(shape, lambda *_: (0,) * len(shape))


def _norm_matmul_kernel(x_ref, g_ref, w_ref, *out_refs):
    h = _rms(x_ref[...], g_ref[...]).astype(BF16)
    f = w_ref.shape[1] // len(out_refs)
    for i, o_ref in enumerate(out_refs):
        o_ref[...] = _dot(h, w_ref[:, i * f:(i + 1) * f])


def _norm_matmul(x, g, w, n_out, tm):
    n, d = x.shape
    f = w.shape[1] // n_out
    return pl.pallas_call(
        _norm_matmul_kernel,
        grid=(n // tm,),
        in_specs=[pl.BlockSpec((tm, d), lambda i: (i, 0)), _full((1, d)), _full(w.shape)],
        out_specs=[pl.BlockSpec((tm, f), lambda i: (i, 0))] * n_out,
        out_shape=[jax.ShapeDtypeStruct((n, f), F32)] * n_out,
        compiler_params=_params("parallel"),
        name="norm_matmul",
    )(x, g, w)


def _matmul_residual_kernel(a_ref, w_ref, x_ref, y_ref):
    y_ref[...] = x_ref[...] + _dot(a_ref[...].astype(BF16), w_ref[...])


def _matmul_residual(a, w, x, tm):
    n, d = x.shape
    k = a.shape[1]
    return pl.pallas_call(
        _matmul_residual_kernel,
        grid=(n // tm,),
        in_specs=[pl.BlockSpec((tm, k), lambda i: (i, 0)), _full(w.shape),
                  pl.BlockSpec((tm, d), lambda i: (i, 0))],
        out_specs=pl.BlockSpec((tm, d), lambda i: (i, 0)),
        out_shape=jax.ShapeDtypeStruct((n, d), F32),
        compiler_params=_params("parallel"),
        name="matmul_residual",
    )(a, w, x)


def _pool_tail(r, x, wgrp_ref, scale_ref, wout_ref):
    gc = wgrp_ref.shape[1]
    rb = r.astype(BF16)
    t = jnp.concatenate([_dot(rb[:, g * gc:(g + 1) * gc], wgrp_ref[g])
                         for g in range(len(POOL_WINDOWS))], axis=-1)
    t = (t * scale_ref[...]).astype(BF16)
    return x + _dot(t, wout_ref[...])


def _pool_prompt_kernel(x_ref, g_ref, win_ref, wgrp_ref, scale_ref, wout_ref,
                        y_ref, st_ref, carry_ref, *, ts, ns):
    s = pl.program_id(1)
    halo = carry_ref.shape[0]
    gc = wgrp_ref.shape[1]

    @pl.when(s == 0)
    def _():
        carry_ref[...] = jnp.zeros_like(carry_ref)

    x = x_ref[...]
    h = _rms(x, g_ref[...]).astype(BF16)
    u = _dot(h, win_ref[...])
    ext = jnp.concatenate([carry_ref[...], u], axis=0)
    pos = s * ts + lax.broadcasted_iota(jnp.int32, (ts, 1), 0)
    parts = []
    acc = ext
    width = 1
    for w in POOL_WINDOWS:
        while width < w:
            acc = acc + pltpu.roll(acc, width, axis=0)
            width *= 2
        assert width == w
        cnt = jnp.minimum(pos + 1, w).astype(F32)
        parts.append(acc[halo:, :gc] / cnt)
        acc = acc[:, gc:]
    r = jnp.concatenate(parts, axis=-1) - u
    y_ref[...] = _pool_tail(r, x, wgrp_ref, scale_ref, wout_ref)
    carry_ref[...] = u[ts - halo:, :]

    @pl.when(s == ns - 1)
    def _():
        st_ref[...] = carry_ref[halo - POOL_STATE:, :]


def _pool_prompt(x, g, w_in, w_grp, scale, w_out, batch, seq, ts):
    n, d = x.shape
    ns = seq // ts
    halo = 16
    kern = functools.partial(_pool_prompt_kernel, ts=ts, ns=ns)
    return pl.pallas_call(
        kern,
        grid=(batch, ns),
        in_specs=[pl.BlockSpec((ts, d), lambda b, s: (b * ns + s, 0)), _full((1, d)),
                  _full(w_in.shape), _full(w_grp.shape), _full((1, d)), _full(w_out.shape)],
        out_specs=[pl.BlockSpec((ts, d), lambda b, s: (b * ns + s, 0)),
                   pl.BlockSpec((None, POOL_STATE, d), lambda b, s: (b, 0, 0))],
        out_shape=[jax.ShapeDtypeStruct((n, d), F32),
                   jax.ShapeDtypeStruct((batch, POOL_STATE, d), F32)],
        scratch_shapes=[pltpu.VMEM((halo, d), F32)],
        compiler_params=_params("parallel", "arbitrary"),
        name="pool_prompt",
    )(x, g, w_in, w_grp, scale, w_out)


def _pool_sample_kernel(x_ref, pref_ref, g_ref, win_ref, wgrp_ref, scale_ref, wout_ref,
                        y_ref, u_ref, *, n_t, pos0):
    bd = x_ref.shape[0] // n_t
    gc = wgrp_ref.shape[1]
    x = x_ref[...]
    h = _rms(x, g_ref[...]).astype(BF16)
    u = _dot(h, win_ref[...])
    u_ref[...] = u
    us = [u[t * bd:(t + 1) * bd] for t in range(n_t)]
    tail = [jnp.zeros_like(us[0])]
    for m in range(1, POOL_STATE + 1):
        tail.append(tail[-1] + pref_ref[POOL_STATE - m])
    rows = []
    for t in range(n_t):
        parts = []
        for gi, w in enumerate(POOL_WINDOWS):
            sl = slice(gi * gc, (gi + 1) * gc)
            n_new = min(w, t + 1)
            wsum = us[t][:, sl]
            for k in range(1, n_new):
                wsum = wsum + us[t - k][:, sl]
            if w - n_new:
                wsum = wsum + tail[w - n_new][:, sl]
            parts.append(wsum / float(min(pos0 + t + 1, w)))
        rows.append(jnp.concatenate(parts, axis=-1) - us[t])
    r = jnp.concatenate(rows, axis=0)
    y_ref[...] = _pool_tail(r, x, wgrp_ref, scale_ref, wout_ref)


def _pool_sample(x, pref_t, g, w_in, w_grp, scale, w_out, n_t, pos0):
    n, d = x.shape
    kern = functools.partial(_pool_sample_kernel, n_t=n_t, pos0=pos0)
    return pl.pallas_call(
        kern,
        grid=(1,),
        in_specs=[_full(x.shape), _full(pref_t.shape), _full((1, d)), _full(w_in.shape),
                  _full(w_grp.shape), _full((1, d)), _full(w_out.shape)],
        out_specs=[_full((n, d)), _full((n, d))],
        out_shape=[jax.ShapeDtypeStruct((n, d), F32)] * 2,
        compiler_params=_params("arbitrary"),
        name="pool_sample",
    )(x, pref_t, g, w_in, w_grp, scale, w_out)


def _top_blocks(gate, n_valid, n_blocks):
    idx = lax.broadcasted_iota(jnp.int32, gate.shape, 0)
    valid = idx < n_valid
    rows = []
    for j in range(n_blocks):
        gj = gate[j:j + 1, :]
        beats = valid & ((gate > gj) | ((gate == gj) & (idx < j)))
        rank = jnp.sum(beats.astype(F32), axis=0, keepdims=True)
        rows.append(jnp.where(rank < MOBA_TOPK, jnp.where(j < n_valid, 1.0, 0.0), 0.0))
    return rows


def _moba_prompt_kernel(q_ref, k_ref, v_ref, o_ref, kb_ref, vt_ref, kmean_ref, sel_ref, *, nb):
    qi = pl.program_id(2)
    blk = MOBA_BLOCK
    hd = q_ref.shape[1]
    scale = hd ** -0.5

    @pl.when(qi == 0)
    def _():
        k = k_ref[...]
        kb_ref[...] = k.astype(BF16)
        kmean_ref[...] = jnp.mean(k.reshape(nb, blk, hd), axis=1)
        vt_ref[...] = v_ref[...].T.astype(BF16)

    q = q_ref[...]
    gate = _dot_nt(kmean_ref[...], q, precision=lax.Precision.HIGHEST)
    for j, row in enumerate(_top_blocks(gate, qi, nb)):
        sel_ref[j:j + 1, :] = row
    qb = q.astype(BF16)
    key_i = lax.broadcasted_iota(jnp.int32, (blk, blk), 0)
    qry_i = lax.broadcasted_iota(jnp.int32, (blk, blk), 1)

    def scores(j):
        kj = kb_ref[pl.ds(pl.multiple_of(j * blk, blk), blk), :]
        return _dot_nt(kj, qb) * scale

    def pv(j, p):
        vj = vt_ref[:, pl.ds(pl.multiple_of(j * blk, blk), blk)]
        return _dot(vj, p.astype(BF16))

    s = jnp.where(key_i <= qry_i, scores(qi), NEG_BIG)
    m = jnp.max(s, axis=0, keepdims=True)
    p = jnp.exp(s - m)
    l = jnp.sum(p, axis=0, keepdims=True)
    acc = pv(qi, p)

    def body(j, carry):
        m, l, acc = carry
        s = jnp.where(sel_ref[pl.ds(j, 1), :] > 0.0, scores(j), NEG_BIG)
        m_new = jnp.maximum(m, jnp.max(s, axis=0, keepdims=True))
        alpha = jnp.exp(m - m_new)
        p = jnp.exp(s - m_new)
        return m_new, alpha * l + jnp.sum(p, axis=0, keepdims=True), alpha * acc + pv(j, p)

    m, l, acc = lax.fori_loop(0, qi, body, (m, l, acc))
    o_ref[...] = (acc / l).T.astype(o_ref.dtype)


def _moba_prompt(q, k, v, batch, seq, n_heads):
    n, d = q.shape
    hd = d // n_heads
    nb = seq // MOBA_BLOCK
    kern = functools.partial(_moba_prompt_kernel, nb=nb)
    return pl.pallas_call(
        kern,
        grid=(batch, n_heads, nb),
        in_specs=[pl.BlockSpec((MOBA_BLOCK, hd), lambda b, h, i: (b * nb + i, h)),
                  pl.BlockSpec((seq, hd), lambda b, h, i: (b, h)),
                  pl.BlockSpec((seq, hd), lambda b, h, i: (b, h))],
        out_specs=pl.BlockSpec((MOBA_BLOCK, hd), lambda b, h, i: (b * nb + i, h)),
        out_shape=jax.ShapeDtypeStruct((n, d), BF16),
        scratch_shapes=[pltpu.VMEM((seq, hd), BF16), pltpu.VMEM((hd, seq), BF16),
                        pltpu.VMEM((nb, hd), F32), pltpu.VMEM((nb, MOBA_BLOCK), F32)],
        compiler_params=_params("parallel", "parallel", "arbitrary"),
        name="moba_prompt",
    )(q, k, v)


def _moba_sample_kernel(pt_ref, q_ref, kn_ref, vn_ref, kp_ref, vp_ref, o_ref, s_ref, vall_ref,
                        *, n_t, n_pages):
    del pt_ref
    p_i = pl.program_id(1)
    nh, hd = q_ref.shape[2], q_ref.shape[3]
    scale = hd ** -0.5
    ppb = MOBA_BLOCK // PAGE_SIZE
    n_past = n_pages // ppb
    head_i = lax.broadcasted_iota(jnp.int32, (nh, hd), 0)

    qbd = jnp.concatenate(
        [jnp.concatenate([jnp.where(head_i == h, q_ref[t, 0], 0.0) for h in range(nh)], axis=-1)
         for t in range(n_t)], axis=0).astype(BF16)
    off = pl.multiple_of(p_i * PAGE_SIZE, PAGE_SIZE)
    s_ref[:, pl.ds(off, PAGE_SIZE)] = _dot_nt(qbd, kp_ref[0].astype(BF16))
    vall_ref[pl.ds(off, PAGE_SIZE), :] = vp_ref[0].astype(BF16)

    @pl.when(p_i == n_pages - 1)
    def _():
        s_all = s_ref[...]
        gate = [jnp.sum(s_all[:, j * MOBA_BLOCK:(j + 1) * MOBA_BLOCK], axis=-1, keepdims=True)
                * (1.0 / MOBA_BLOCK) for j in range(n_past)]
        sel = []
        for j in range(n_past):
            rank = jnp.zeros_like(gate[j])
            for i in range(n_past):
                if i != j:
                    beats = (gate[i] >= gate[j]) if i < j else (gate[i] > gate[j])
                    rank = rank + beats.astype(F32)
            sel.append(rank < MOBA_TOPK)
        s_past = jnp.concatenate(
            [jnp.where(sel[j], s_all[:, j * MOBA_BLOCK:(j + 1) * MOBA_BLOCK] * scale, NEG_BIG)
             for j in range(n_past)], axis=-1)
        s_new = [[jnp.sum(q_ref[t, 0] * kn_ref[t2, 0], axis=-1, keepdims=True) * scale
                  for t2 in range(t + 1)] for t in range(n_t)]
        m_past = jnp.max(s_past, axis=-1, keepdims=True)
        m = jnp.concatenate(
            [functools.reduce(jnp.maximum, s_new[t], m_past[t * nh:(t + 1) * nh]) for t in range(n_t)],
            axis=0)
        p_past = jnp.exp(s_past - m)
        l_past = jnp.sum(p_past, axis=-1, keepdims=True)
        full = _dot(p_past.astype(BF16), vall_ref[...])
        for t in range(n_t):
            rows = full[t * nh:(t + 1) * nh]
            out = functools.reduce(
                lambda a, b: a + b,
                [jnp.where(head_i == h, rows[:, h * hd:(h + 1) * hd], 0.0) for h in range(nh)])
            l = l_past[t * nh:(t + 1) * nh]
            for t2 in range(t + 1):
                p_new = jnp.exp(s_new[t][t2] - m[t * nh:(t + 1) * nh])
                l = l + p_new
                out = out + p_new * vn_ref[t2, 0]
            o_ref[t, 0] = out / l


def _moba_sample(q, k, v, cache_k, cache_v, page_table, n_t, n_heads):
    n, d = q.shape
    bd = n // n_t
    hd = d // n_heads
    n_pages = page_table.shape[1]
    past = n_pages * PAGE_SIZE
    assert n_pages % (MOBA_BLOCK // PAGE_SIZE) == 0 and n_t <= MOBA_BLOCK
    q4, k4, v4 = (a.reshape(n_t, bd, n_heads, hd) for a in (q, k, v))
    new_spec = pl.BlockSpec((n_t, 1, n_heads, hd), lambda b, p, pt: (0, b, 0, 0))
    page_spec = pl.BlockSpec((1, PAGE_SIZE, d), lambda b, p, pt: (pt[b, p], 0, 0))
    kern = functools.partial(_moba_sample_kernel, n_t=n_t, n_pages=n_pages)
    out = pl.pallas_call(
        kern,
        grid_spec=pltpu.PrefetchScalarGridSpec(
            num_scalar_prefetch=1,
            grid=(bd, n_pages),
            in_specs=[new_spec, new_spec, new_spec, page_spec, page_spec],
            out_specs=new_spec,
            scratch_shapes=[pltpu.VMEM((n_t * n_heads, past), F32), pltpu.VMEM((past, d), BF16)],
        ),
        out_shape=jax.ShapeDtypeStruct((n_t, bd, n_heads, hd), F32),
        compiler_params=_params("parallel", "arbitrary"),
        name="moba_sample",
    )(page_table, q4, k4, v4, cache_k, cache_v)
    return out.reshape(n, d)


def _gelu(z):
    return 0.5 * z * (1.0 + jnp.tanh(0.7978845608028654 * (z + 0.044715 * z * z * z)))


def _gmlp_front(x, g_ref, win_ref, lng_ref, lnb_ref):
    h = _rms(x, g_ref[...]).astype(BF16)
    z = _gelu(_dot(h, win_ref[...]))
    dg = z.shape[1] // 2
    u, v = z[:, :dg], z[:, dg:]
    mu = jnp.mean(v, axis=-1, keepdims=True)
    var = jnp.mean(jnp.square(v - mu), axis=-1, keepdims=True)
    vn = (v - mu) * lax.rsqrt(var + LN_EPS) * lng_ref[...] + lnb_ref[...]
    return u, vn


def _gmlp_prompt_kernel(x_ref, g_ref, win_ref, lng_ref, lnb_ref, ws_ref, bias_ref, wout_ref,
                        y_ref, vn_ref, *, tm):
    x = x_ref[...]
    u, vn = _gmlp_front(x, g_ref, win_ref, lng_ref, lnb_ref)
    vn_ref[...] = vn
    ck = GMLP_CHUNK
    gc = vn.shape[1] // GMLP_GROUPS
    row_i = lax.broadcasted_iota(jnp.int32, (ck, ck), 0)
    col_i = lax.broadcasted_iota(jnp.int32, (ck, ck), 1)
    wc = [jnp.where(row_i >= col_i, ws_ref[g], 0.0).astype(BF16) for g in range(GMLP_GROUPS)]
    vb = vn.astype(BF16)
    mix = jnp.concatenate(
        [jnp.concatenate([_dot(wc[g], vb[c * ck:(c + 1) * ck, g * gc:(g + 1) * gc])
                          for g in range(GMLP_GROUPS)], axis=-1) + bias_ref[...]
         for c in range(tm // ck)], axis=0)
    y_ref[...] = x + _dot((u * mix).astype(BF16), wout_ref[...])


def _gmlp_prompt(x, g, w_in, ln_g, ln_b, w_s, bias, w_out, tm):
    n, d = x.shape
    dg = w_out.shape[0]
    kern = functools.partial(_gmlp_prompt_kernel, tm=tm)
    return pl.pallas_call(
        kern,
        grid=(n // tm,),
        in_specs=[pl.BlockSpec((tm, d), lambda i: (i, 0)), _full((1, d)), _full(w_in.shape),
                  _full((1, dg)), _full((1, dg)), _full(w_s.shape), _full(bias.shape),
                  _full(w_out.shape)],
        out_specs=[pl.BlockSpec((tm, d), lambda i: (i, 0)), pl.BlockSpec((tm, dg), lambda i: (i, 0))],
        out_shape=[jax.ShapeDtypeStruct((n, d), F32), jax.ShapeDtypeStruct((n, dg), F32)],
        compiler_params=_params("parallel"),
        name="gmlp_prompt",
    )(x, g, w_in, ln_g, ln_b, w_s, bias, w_out)


def _gmlp_sample_kernel(ws_ref, bs_ref, x_ref, g_ref, win_ref, lng_ref, lnb_ref, wout_ref,
                        y_ref, vn_ref, *, n_t):
    x = x_ref[...]
    u, vn = _gmlp_front(x, g_ref, win_ref, lng_ref, lnb_ref)
    vn_ref[...] = vn
    bd = x.shape[0] // n_t
    gc = vn.shape[1] // GMLP_GROUPS
    rows = []
    for t in range(n_t):
        parts = []
        for g in range(GMLP_GROUPS):
            sl = slice(g * gc, (g + 1) * gc)
            acc = jnp.full((bd, gc), bs_ref[g * n_t + t], F32)
            for s in range(t + 1):
                acc = acc + ws_ref[(g * n_t + t) * n_t + s] * vn[s * bd:(s + 1) * bd, sl]
            parts.append(acc)
        rows.append(jnp.concatenate(parts, axis=-1))
    mix = jnp.concatenate(rows, axis=0)
    y_ref[...] = x + _dot((u * mix).astype(BF16), wout_ref[...])


def _gmlp_sample(x, g, w_in, ln_g, ln_b, ws_flat, bs_flat, w_out, n_t):
    n, d = x.shape
    dg = w_out.shape[0]
    kern = functools.partial(_gmlp_sample_kernel, n_t=n_t)
    smem = pl.BlockSpec(memory_space=pltpu.SMEM)
    return pl.pallas_call(
        kern,
        grid=(1,),
        in_specs=[smem, smem, _full(x.shape), _full((1, d)), _full(w_in.shape), _full((1, dg)),
                  _full((1, dg)), _full(w_out.shape)],
        out_specs=[_full((n, d)), _full((n, dg))],
        out_shape=[jax.ShapeDtypeStruct((n, d), F32), jax.ShapeDtypeStruct((n, dg), F32)],
        compiler_params=_params("arbitrary"),
        name="gmlp_sample",
    )(ws_flat, bs_flat, x, g, w_in, ln_g, ln_b, w_out)


def _router_kernel(x_ref, g_ref, rgw_ref, rgb_ref, rew_ref, reb_ref, hn_ref, gates_ref):
    hn = _rms(x_ref[...], g_ref[...])
    hn_ref[...] = hn.astype(BF16)
    hi = lax.Precision.HIGHEST
    lg = jnp.dot(hn, rgw_ref[...], preferred_element_type=F32, precision=hi) + rgb_ref[...]
    le = jnp.dot(hn, rew_ref[...], preferred_element_type=F32, precision=hi) + reb_ref[...]
    ne = MOE_GROUPS * MOE_PER_GROUP
    gi = lax.broadcasted_iota(jnp.int32, lg.shape, 1)
    g_max = jnp.max(lg, axis=-1, keepdims=True)
    g_idx = jnp.min(jnp.where(lg == g_max, gi, MOE_GROUPS), axis=-1, keepdims=True)
    g_p = 1.0 / jnp.sum(jnp.exp(lg - g_max), axis=-1, keepdims=True)
    ei = lax.broadcasted_iota(jnp.int32, le.shape, 1)
    first = g_idx * MOE_PER_GROUP
    in_group = (ei >= first) & (ei < first + MOE_PER_GROUP)
    l1 = jnp.where(in_group, le, -jnp.inf)
    m1 = jnp.max(l1, axis=-1, keepdims=True)
    e1 = jnp.min(jnp.where(l1 == m1, ei, ne), axis=-1, keepdims=True)
    l2 = jnp.where(ei == e1, -jnp.inf, l1)
    m2 = jnp.max(l2, axis=-1, keepdims=True)
    e2 = jnp.min(jnp.where(l2 == m2, ei, ne), axis=-1, keepdims=True)
    z = jnp.sum(jnp.exp(l1 - m1), axis=-1, keepdims=True)
    p1 = 1.0 / z
    p2 = jnp.exp(m2 - m1) / z
    w1 = g_p * p1 / (p1 + p2)
    w2 = g_p * p2 / (p1 + p2)
    gates_ref[...] = jnp.where(ei == e1, w1, 0.0) + jnp.where(ei == e2, w2, 0.0)


def _router(x, g, rgw, rgb, rew, reb, tm):
    n, d = x.shape
    ne = MOE_GROUPS * MOE_PER_GROUP
    return pl.pallas_call(
        _router_kernel,
        grid=(n // tm,),
        in_specs=[pl.BlockSpec((tm, d), lambda i: (i, 0)), _full((1, d)), _full(rgw.shape),
                  _full(rgb.shape), _full(rew.shape), _full(reb.shape)],
        out_specs=[pl.BlockSpec((tm, d), lambda i: (i, 0)), pl.BlockSpec((tm, ne), lambda i: (i, 0))],
        out_shape=[jax.ShapeDtypeStruct((n, d), BF16), jax.ShapeDtypeStruct((n, ne), F32)],
        compiler_params=_params("parallel"),
        name="moe_router",
    )(x, g, rgw, rgb, rew, reb)


def _moe_dense_kernel(hn_ref, gates_ref, wg_ref, wu_ref, wd_ref, x_ref, y_ref):
    e = pl.program_id(1)

    @pl.when(e == 0)
    def _():
        y_ref[...] = x_ref[...]

    hn = hn_ref[...]
    a = _dot(hn, wg_ref[...])
    b = _dot(hn, wu_ref[...])
    gates = gates_ref[...]
    lane = lax.broadcasted_iota(jnp.int32, gates.shape, 1)
    gate = jnp.sum(jnp.where(lane == e, gates, 0.0), axis=-1, keepdims=True)
    hid = (a * jax.nn.sigmoid(a)) * b * gate
    y_ref[...] += _dot(hid.astype(BF16), wd_ref[...])


def _moe_dense(hn, gates, wg, wu, wd, x, tm):
    n, d = x.shape
    ne, _, de = wg.shape
    return pl.pallas_call(
        _moe_dense_kernel,
        grid=(n // tm, ne),
        in_specs=[pl.BlockSpec((tm, d), lambda i, e: (i, 0)), pl.BlockSpec((tm, ne), lambda i, e: (i, 0)),
                  pl.BlockSpec((None, d, de), lambda i, e: (e, 0, 0)),
                  pl.BlockSpec((None, d, de), lambda i, e: (e, 0, 0)),
                  pl.BlockSpec((None, de, d), lambda i, e: (e, 0, 0)),
                  pl.BlockSpec((tm, d), lambda i, e: (i, 0))],
        out_specs=pl.BlockSpec((tm, d), lambda i, e: (i, 0)),
        out_shape=jax.ShapeDtypeStruct((n, d), F32),
        compiler_params=_params("parallel", "arbitrary"),
        name="moe_dense",
    )(hn, gates, wg, wu, wd, x)


def _final_norm_kernel(x_ref, g_ref, y_ref):
    y_ref[...] = _rms(x_ref[...], g_ref[...])


def _final_norm(x, g, tm):
    n, d = x.shape
    return pl.pallas_call(
        _final_norm_kernel,
        grid=(n // tm,),
        in_specs=[pl.BlockSpec((tm, d), lambda i: (i, 0)), _full((1, d))],
        out_specs=pl.BlockSpec((tm, d), lambda i: (i, 0)),
        out_shape=jax.ShapeDtypeStruct((n, d), F32),
        compiler_params=_params("parallel"),
        name="final_norm",
    )(x, g)


def _tile(n, pref):
    t = min(n, pref)
    assert n % t == 0, (n, t)
    return t


def kernel(x_prompt, x_sample, state_pool, cache_k, cache_v, page_table, norm_mix, norm_ffn, norm_final, pool_w_in, pool_w_grp, pool_scale, pool_w_out, attn_w_qkv, attn_w_out, gmlp_w_in, gmlp_ln_g, gmlp_ln_b, gmlp_w_s, gmlp_b_s, gmlp_w_out, moe_rg_w, moe_rg_b, moe_re_w, moe_re_b, moe_w_gate, moe_w_up, moe_w_down):
    batch, seq, d = x_prompt.shape
    dec_b, n_t, _ = x_sample.shape
    depth = norm_mix.shape[0]
    n_heads = N_HEADS
    hd = d // n_heads
    n_pages = page_table.shape[1]
    past_len = n_pages * PAGE_SIZE
    assert seq % MOBA_BLOCK == 0 and seq % GMLP_CHUNK == 0 and n_t <= GMLP_CHUNK
    assert past_len % GMLP_CHUNK == 0 and past_len >= POOL_STATE

    n_p, n_s = batch * seq, dec_b * n_t
    xp = x_prompt.reshape(n_p, d)
    xs = x_sample.transpose(1, 0, 2).reshape(n_s, d)
    tp = _tile(seq, 512)
    row = lambda a: a.reshape(1, -1)

    def to_batch_major(a):
        return a.reshape((n_t, dec_b) + a.shape[1:]).swapaxes(0, 1)

    pool_p, pool_s, k_p, v_p, k_s, v_s, gv_p, gv_s = [], [], [], [], [], [], [], []
    for i in range(depth):
        kind, j = i % N_MIXERS, i // N_MIXERS
        g_mix = row(norm_mix[i])
        if kind == 0:
            w_in, w_grp, w_out = (a.astype(BF16) for a in (pool_w_in[j], pool_w_grp[j], pool_w_out[j]))
            scale = row(pool_scale[j])
            xp, st = _pool_prompt(xp, g_mix, w_in, w_grp, scale, w_out, batch, seq, tp)
            pool_p.append(st)
            xs, u_s = _pool_sample(xs, state_pool[j].transpose(1, 0, 2), g_mix, w_in, w_grp, scale,
                                   w_out, n_t, past_len)
            pool_s.append(jnp.concatenate([state_pool[j][:, n_t:], to_batch_major(u_s)], axis=1))
        elif kind == 1:
            w_qkv, w_out = attn_w_qkv[j].astype(BF16), attn_w_out[j].astype(BF16)
            qp, kp, vp = _norm_matmul(xp, g_mix, w_qkv, 3, tp)
            qs, ks, vs = _norm_matmul(xs, g_mix, w_qkv, 3, _tile(n_s, 512))
            ap = _moba_prompt(qp, kp, vp, batch, seq, n_heads)
            n_phys = cache_k.shape[1]
            a_s = _moba_sample(qs, ks, vs, cache_k[j].reshape(n_phys, PAGE_SIZE, d),
                               cache_v[j].reshape(n_phys, PAGE_SIZE, d), page_table, n_t, n_heads)
            xp = _matmul_residual(ap, w_out, xp, tp)
            xs = _matmul_residual(a_s, w_out, xs, _tile(n_s, 512))
            k_p.append(kp.reshape(batch, seq, n_heads, hd))
            v_p.append(vp.reshape(batch, seq, n_heads, hd))
            k_s.append(to_batch_major(ks).reshape(dec_b, n_t, n_heads, hd))
            v_s.append(to_batch_major(vs).reshape(dec_b, n_t, n_heads, hd))
        else:
            w_in, w_out = gmlp_w_in[j].astype(BF16), gmlp_w_out[j].astype(BF16)
            ln_g, ln_b = row(gmlp_ln_g[j]), row(gmlp_ln_b[j])
            gc = w_out.shape[0] // GMLP_GROUPS
            bias = jnp.broadcast_to(gmlp_b_s[j].T[:, :, None], (GMLP_CHUNK, GMLP_GROUPS, gc))
            xp, vn_p = _gmlp_prompt(xp, g_mix, w_in, ln_g, ln_b, gmlp_w_s[j],
                                    bias.reshape(GMLP_CHUNK, GMLP_GROUPS * gc), w_out, tp)
            xs, vn_s = _gmlp_sample(xs, g_mix, w_in, ln_g, ln_b,
                                    gmlp_w_s[j][:, :n_t, :n_t].reshape(-1),
                                    gmlp_b_s[j][:, :n_t].reshape(-1), w_out, n_t)
            gv_p.append(vn_p.reshape(batch, seq, -1))
            gv_s.append(to_batch_major(vn_s))

        g_ffn = row(norm_ffn[i])
        router_w = (moe_rg_w[i], row(moe_rg_b[i]), moe_re_w[i], row(moe_re_b[i]))
        wg, wu, wd = (a.astype(BF16) for a in (moe_w_gate[i], moe_w_up[i], moe_w_down[i]))
        hn_p, gates_p = _router(xp, g_ffn, *router_w, tp)
        hn_s, gates_s = _router(xs, g_ffn, *router_w, _tile(n_s, 512))
        xp = _moe_dense(hn_p, gates_p, wg, wu, wd, xp, _tile(n_p, 1024))
        xs = _moe_dense(hn_s, gates_s, wg, wu, wd, xs, _tile(n_s, 512))

    g_fin = row(norm_final)
    y_p = _final_norm(xp, g_fin, tp).reshape(batch, seq, d)
    y_s = to_batch_major(_final_norm(xs, g_fin, _tile(n_s, 512)))
    return (y_p, y_s, jnp.stack(pool_p), jnp.stack(pool_s), jnp.stack(k_p), jnp.stack(v_p),
            jnp.stack(k_s), jnp.stack(v_s), jnp.stack(gv_p), jnp.stack(gv_s))
```

```python
import functools

import jax
import jax.numpy as jnp
from jax import lax
from jax.experimental import pallas as pl
from jax.experimental.pallas import tpu as pltpu

F32 = jnp.float32
BF16 = jnp.bfloat16

RMS_EPS = 1e-6
LN_EPS = 1e-5
N_MIXERS = 3
POOL_WINDOWS = (2, 4, 8, 16)
POOL_STATE = max(POOL_WINDOWS) - 1
MOBA_BLOCK = 256
MOBA_TOPK = 3
GMLP_CHUNK = 128
GMLP_GROUPS = 8
MOE_GROUPS = 4
MOE_PER_GROUP = 4
PAGE_SIZE = 128
N_HEADS = 8

V7X_VMEM_LIMIT_BYTES = 56 * 1024 * 1024
NEG_BIG = -1e30


def _params(*sem):
    return pltpu.CompilerParams(dimension_semantics=sem, vmem_limit_bytes=V7X_VMEM_LIMIT_BYTES)


def _rms(x, g):
    return x * lax.rsqrt(jnp.mean(x * x, axis=-1, keepdims=True) + RMS_EPS) * g


def _dot(a, b):
    return jnp.dot(a, b, preferred_element_type=F32)


def _dot_nt(a, b, precision=None):
    return lax.dot_general(a, b, (((1,), (1,)), ((), ())), preferred_element_type=F32,
                           precision=precision)


def _full(shape):
    return pl.BlockSpec(shape, lambda *_: (0,) * len(shape))


def _norm_matmul_kernel(x_ref, g_ref, w_ref, *out_refs):
    h = _rms(x_ref[...], g_ref[...]).astype(BF16)
    f = w_ref.shape[1] // len(out_refs)
    for i, o_ref in enumerate(out_refs):
        o_ref[...] = _dot(h, w_ref[:, i * f:(i + 1) * f])


def _norm_matmul(x, g, w, n_out, tm):
    n, d = x.shape
    f = w.shape[1] // n_out
    return pl.pallas_call(
        _norm_matmul_kernel,
        grid=(n // tm,),
        in_specs=[pl.BlockSpec((tm, d), lambda i: (i, 0)), _full((1, d)), _full(w.shape)],
        out_specs=[pl.BlockSpec((tm, f), lambda i: (i, 0))] * n_out,
        out_shape=[jax.ShapeDtypeStruct((n, f), F32)] * n_out,
        compiler_params=_params("parallel"),
        name="norm_matmul",
    )(x, g, w)


def _matmul_residual_kernel(a_ref, w_ref, x_ref, y_ref):
    y_ref[...] = x_ref[...] + _dot(a_ref[...].astype(BF16), w_ref[...])


def _matmul_residual(a, w, x, tm):
    n, d = x.shape
    k = a.shape[1]
    return pl.pallas_call(
        _matmul_residual_kernel,
        grid=(n // tm,),
        in_specs=[pl.BlockSpec((tm, k), lambda i: (i, 0)), _full(w.shape),
                  pl.BlockSpec((tm, d), lambda i: (i, 0))],
        out_specs=pl.BlockSpec((tm, d), lambda i: (i, 0)),
        out_shape=jax.ShapeDtypeStruct((n, d), F32),
        compiler_params=_params("parallel"),
        name="matmul_residual",
    )(a, w, x)


def _pool_tail(r, x, wgrp_ref, scale_ref, wout_ref):
    gc = wgrp_ref.shape[1]
    rb = r.astype(BF16)
    t = jnp.concatenate([_dot(rb[:, g * gc:(g + 1) * gc], wgrp_ref[g])
                         for g in range(len(POOL_WINDOWS))], axis=-1)
    t = (t * scale_ref[...]).astype(BF16)
    return x + _dot(t, wout_ref[...])


def _pool_prompt_kernel(x_ref, g_ref, win_ref, wgrp_ref, scale_ref, wout_ref,
                        y_ref, st_ref, carry_ref, *, ts, ns):
    s = pl.program_id(1)
    halo = carry_ref.shape[0]
    gc = wgrp_ref.shape[1]

    @pl.when(s == 0)
    def _():
        carry_ref[...] = jnp.zeros_like(carry_ref)

    x = x_ref[...]
    h = _rms(x, g_ref[...]).astype(BF16)
    u = _dot(h, win_ref[...])
    ext = jnp.concatenate([carry_ref[...], u], axis=0)
    pos = s * ts + lax.broadcasted_iota(jnp.int32, (ts, 1), 0)
    parts = []
    acc = ext
    width = 1
    for w in POOL_WINDOWS:
        while width < w:
            acc = acc + pltpu.roll(acc, width, axis=0)
            width *= 2
        assert width == w
        cnt = jnp.minimum(pos + 1, w).astype(F32)
        parts.append(acc[halo:, :gc] / cnt)
        acc = acc[:, gc:]
    r = jnp.concatenate(parts, axis=-1) - u
    y_ref[...] = _pool_tail(r, x, wgrp_ref, scale_ref, wout_ref)
    carry_ref[...] = u[ts - halo:, :]

    @pl.when(s == ns - 1)
    def _():
        st_ref[...] = carry_ref[halo - POOL_STATE:, :]


def _pool_prompt(x, g, w_in, w_grp, scale, w_out, batch, seq, ts):
    n, d = x.shape
    ns = seq // ts
    halo = 16
    kern = functools.partial(_pool_prompt_kernel, ts=ts, ns=ns)
    return pl.pallas_call(
        kern,
        grid=(batch, ns),
        in_specs=[pl.BlockSpec((ts, d), lambda b, s: (b * ns + s, 0)), _full((1, d)),
                  _full(w_in.shape), _full(w_grp.shape), _full((1, d)), _full(w_out.shape)],
        out_specs=[pl.BlockSpec((ts, d), lambda b, s: (b * ns + s, 0)),
                   pl.BlockSpec((None, POOL_STATE, d), lambda b, s: (b, 0, 0))],
        out_shape=[jax.ShapeDtypeStruct((n, d), F32),
                   jax.ShapeDtypeStruct((batch, POOL_STATE, d), F32)],
        scratch_shapes=[pltpu.VMEM((halo, d), F32)],
        compiler_params=_params("parallel", "arbitrary"),
        name="pool_prompt",
    )(x, g, w_in, w_grp, scale, w_out)


def _pool_sample_kernel(x_ref, pref_ref, g_ref, win_ref, wgrp_ref, scale_ref, wout_ref,
                        y_ref, u_ref, *, n_t, pos0):
    bd = x_ref.shape[0] // n_t
    gc = wgrp_ref.shape[1]
    x = x_ref[...]
    h = _rms(x, g_ref[...]).astype(BF16)
    u = _dot(h, win_ref[...])
    u_ref[...] = u
    us = [u[t * bd:(t + 1) * bd] for t in range(n_t)]
    tail = [jnp.zeros_like(us[0])]
    for m in range(1, POOL_STATE + 1):
        tail.append(tail[-1] + pref_ref[POOL_STATE - m])
    rows = []
    for t in range(n_t):
        parts = []
        for gi, w in enumerate(POOL_WINDOWS):
            sl = slice(gi * gc, (gi + 1) * gc)
            n_new = min(w, t + 1)
            wsum = us[t][:, sl]
            for k in range(1, n_new):
                wsum = wsum + us[t - k][:, sl]
            if w - n_new:
                wsum = wsum + tail[w - n_new][:, sl]
            parts.append(wsum / float(min(pos0 + t + 1, w)))
        rows.append(jnp.concatenate(parts, axis=-1) - us[t])
    r = jnp.concatenate(rows, axis=0)
    y_ref[...] = _pool_tail(r, x, wgrp_ref, scale_ref, wout_ref)


def _pool_sample(x, pref_t, g, w_in, w_grp, scale, w_out, n_t, pos0):
    n, d = x.shape
    kern = functools.partial(_pool_sample_kernel, n_t=n_t, pos0=pos0)
    return pl.pallas_call(
        kern,
        grid=(1,),
        in_specs=[_full(x.shape), _full(pref_t.shape), _full((1, d)), _full(w_in.shape),
                  _full(w_grp.shape), _full((1, d)), _full(w_out.shape)],
        out_specs=[_full((n, d)), _full((n, d))],
        out_shape=[jax.ShapeDtypeStruct((n, d), F32)] * 2,
        compiler_params=_params("arbitrary"),
        name="pool_sample",
    )(x, pref_t, g, w_in, w_grp, scale, w_out)


def _top_blocks(gate, n_valid, n_blocks):
    idx = lax.broadcasted_iota(jnp.int32, gate.shape, 0)
    valid = idx < n_valid
    rows = []
    for j in range(n_blocks):
        gj = gate[j:j + 1, :]
        beats = valid & ((gate > gj) | ((gate == gj) & (idx < j)))
        rank = jnp.sum(beats.astype(F32), axis=0, keepdims=True)
        rows.append(jnp.where(rank < MOBA_TOPK, jnp.where(j < n_valid, 1.0, 0.0), 0.0))
    return rows


def _moba_prompt_kernel(q_ref, k_ref, v_ref, o_ref, kb_ref, vt_ref, kmean_ref, sel_ref, *, nb):
    qi = pl.program_id(2)
    blk = MOBA_BLOCK
    hd = q_ref.shape[1]
    scale = hd ** -0.5

    @pl.when(qi == 0)
    def _():
        k = k_ref[...]
        kb_ref[...] = k.astype(BF16)
        kmean_ref[...] = jnp.mean(k.reshape(nb, blk, hd), axis=1)
        vt_ref[...] = v_ref[...].T.astype(BF16)

    q = q_ref[...]
    gate = _dot_nt(kmean_ref[...], q, precision=lax.Precision.HIGHEST)
    for j, row in enumerate(_top_blocks(gate, qi, nb)):
        sel_ref[j:j + 1, :] = row
    qb = q.astype(BF16)
    key_i = lax.broadcasted_iota(jnp.int32, (blk, blk), 0)
    qry_i = lax.broadcasted_iota(jnp.int32, (blk, blk), 1)

    def scores(j):
        kj = kb_ref[pl.ds(pl.multiple_of(j * blk, blk), blk), :]
        return _dot_nt(kj, qb) * scale

    def pv(j, p):
        vj = vt_ref[:, pl.ds(pl.multiple_of(j * blk, blk), blk)]
        return _dot(vj, p.astype(BF16))

    s = jnp.where(key_i <= qry_i, scores(qi), NEG_BIG)
    m = jnp.max(s, axis=0, keepdims=True)
    p = jnp.exp(s - m)
    l = jnp.sum(p, axis=0, keepdims=True)
    acc = pv(qi, p)

    def body(j, carry):
        m, l, acc = carry
        s = jnp.where(sel_ref[pl.ds(j, 1), :] > 0.0, scores(j), NEG_BIG)
        m_new = jnp.maximum(m, jnp.max(s, axis=0, keepdims=True))
        alpha = jnp.exp(m - m_new)
        p = jnp.exp(s - m_new)
        return m_new, alpha * l + jnp.sum(p, axis=0, keepdims=True), alpha * acc + pv(j, p)

    m, l, acc = lax.fori_loop(0, qi, body, (m, l, acc))
    o_ref[...] = (acc / l).T.astype(o_ref.dtype)


def _moba_prompt(q, k, v, batch, seq, n_heads):
    n, d = q.shape
    hd = d // n_heads
    nb = seq // MOBA_BLOCK
    kern = functools.partial(_moba_prompt_kernel, nb=nb)
    return pl.pallas_call(
        kern,
        grid=(batch, n_heads, nb),
        in_specs=[pl.BlockSpec((MOBA_BLOCK, hd), lambda b, h, i: (b * nb + i, h)),
                  pl.BlockSpec((seq, hd), lambda b, h, i: (b, h)),
                  pl.BlockSpec((seq, hd), lambda b, h, i: (b, h))],
        out_specs=pl.BlockSpec((MOBA_BLOCK, hd), lambda b, h, i: (b * nb + i, h)),
        out_shape=jax.ShapeDtypeStruct((n, d), BF16),
        scratch_shapes=[pltpu.VMEM((seq, hd), BF16), pltpu.VMEM((hd, seq), BF16),
                        pltpu.VMEM((nb, hd), F32), pltpu.VMEM((nb, MOBA_BLOCK), F32)],
        compiler_params=_params("parallel", "parallel", "arbitrary"),
        name="moba_prompt",
    )(q, k, v)


def _moba_sample_kernel(pt_ref, q_ref, kn_ref, vn_ref, kp_ref, vp_ref, o_ref, wbd_ref, s_ref, vall_ref,
                        *, n_t, n_pages):
    del pt_ref
    p_i = pl.program_id(1)
    nh, hd = q_ref.shape[2], q_ref.shape[3]
    lanes = wbd_ref.shape[1]
    scale = hd ** -0.5
    n_past = n_pages // (MOBA_BLOCK // PAGE_SIZE)
    head_i = lax.broadcasted_iota(jnp.int32, (nh, hd), 0)

    @pl.when(p_i == 0)
    def _():
        rows = [jnp.concatenate([jnp.where(head_i == h, q_ref[t, 0], 0.0) for h in range(nh)], axis=-1)
                for t in range(n_t)]
        rows.append(jnp.zeros((lanes - n_t * nh, nh * hd), F32))
        wbd_ref[...] = jnp.concatenate(rows, axis=0).T.astype(BF16)

    off = pl.multiple_of(p_i * PAGE_SIZE, PAGE_SIZE)
    acc = None
    for h in range(nh):
        kh = kp_ref[pl.ds(h, PAGE_SIZE, stride=nh), :].astype(BF16)
        part = _dot(kh, wbd_ref[h * hd:(h + 1) * hd, :])
        acc = part if acc is None else acc + part
        vall_ref[pl.ds(off, PAGE_SIZE), h * hd:(h + 1) * hd] = (
            vp_ref[pl.ds(h, PAGE_SIZE, stride=nh), :].astype(BF16))
    s_ref[pl.ds(off, PAGE_SIZE), :] = acc

    @pl.when(p_i == n_pages - 1)
    def _():
        s_all = s_ref[...]
        blocks = [s_all[j * MOBA_BLOCK:(j + 1) * MOBA_BLOCK] for j in range(n_past)]
        gate = [jnp.sum(b, axis=0, keepdims=True) * (1.0 / MOBA_BLOCK) for b in blocks]
        sel = []
        for j in range(n_past):
            rank = jnp.zeros_like(gate[j])
            for i in range(n_past):
                if i != j:
                    beats = (gate[i] >= gate[j]) if i < j else (gate[i] > gate[j])
                    rank = rank + beats.astype(F32)
            sel.append(rank < MOBA_TOPK)
        s_past = jnp.concatenate(
            [jnp.where(sel[j], blocks[j] * scale, NEG_BIG) for j in range(n_past)], axis=0)
        m_row = jnp.max(s_past, axis=0, keepdims=True)
        p_past = jnp.exp(s_past - m_row)
        l_row = jnp.sum(p_past, axis=0, keepdims=True)
        full = _dot(p_past.T.astype(BF16), vall_ref[...])
        stats = jnp.concatenate([m_row, l_row, jnp.zeros((lanes - 2, lanes), F32)], axis=0).T
        for t in range(n_t):
            rows = full[t * nh:(t + 1) * nh]
            out = functools.reduce(
                lambda a, b: a + b,
                [jnp.where(head_i == h, rows[:, h * hd:(h + 1) * hd], 0.0) for h in range(nh)])
            m_past = stats[t * nh:(t + 1) * nh, 0:1]
            l_past = stats[t * nh:(t + 1) * nh, 1:2]
            s_new = [jnp.sum(q_ref[t, 0] * kn_ref[t2, 0], axis=-1, keepdims=True) * scale
                     for t2 in range(t + 1)]
            m = functools.reduce(jnp.maximum, s_new, m_past)
            alpha = jnp.exp(m_past - m)
            out = alpha * out
            l = alpha * l_past
            for t2 in range(t + 1):
                p_new = jnp.exp(s_new[t2] - m)
                l = l + p_new
                out = out + p_new * vn_ref[t2, 0]
            o_ref[t, 0] = out / l


def _moba_sample(q, k, v, cache_k, cache_v, layer, page_table, n_t, n_heads):
    n, d = q.shape
    bd = n // n_t
    hd = d // n_heads
    n_pages = page_table.shape[1]
    past = n_pages * PAGE_SIZE
    lanes = 128
    assert n_pages % (MOBA_BLOCK // PAGE_SIZE) == 0 and n_t <= MOBA_BLOCK and n_t * n_heads <= lanes - 2
    q4, k4, v4 = (a.reshape(n_t, bd, n_heads, hd) for a in (q, k, v))
    n_layers, n_phys = cache_k.shape[:2]
    ck, cv = (c.reshape(n_layers, n_phys, PAGE_SIZE * n_heads, hd) for c in (cache_k, cache_v))
    new_spec = pl.BlockSpec((n_t, 1, n_heads, hd), lambda b, p, pt: (0, b, 0, 0))
    page_spec = pl.BlockSpec((None, None, PAGE_SIZE * n_heads, hd), lambda b, p, pt: (layer, pt[b, p], 0, 0))
    kern = functools.partial(_moba_sample_kernel, n_t=n_t, n_pages=n_pages)
    out = pl.pallas_call(
        kern,
        grid_spec=pltpu.PrefetchScalarGridSpec(
            num_scalar_prefetch=1,
            grid=(bd, n_pages),
            in_specs=[new_spec, new_spec, new_spec, page_spec, page_spec],
            out_specs=new_spec,
            scratch_shapes=[pltpu.VMEM((d, lanes), BF16), pltpu.VMEM((past, lanes), F32),
                            pltpu.VMEM((past, d), BF16)],
        ),
        out_shape=jax.ShapeDtypeStruct((n_t, bd, n_heads, hd), F32),
        compiler_params=_params("parallel", "arbitrary"),
        name="moba_sample",
    )(page_table, q4, k4, v4, ck, cv)
    return out.reshape(n, d)


def _gelu(z):
    return 0.5 * z * (1.0 + jnp.tanh(0.7978845608028654 * (z + 0.044715 * z * z * z)))


def _gmlp_front(x, g_ref, win_ref, lng_ref, lnb_ref):
    h = _rms(x, g_ref[...]).astype(BF16)
    z = _gelu(_dot(h, win_ref[...]))
    dg = z.shape[1] // 2
    u, v = z[:, :dg], z[:, dg:]
    mu = jnp.mean(v, axis=-1, keepdims=True)
    var = jnp.mean(jnp.square(v - mu), axis=-1, keepdims=True)
    vn = (v - mu) * lax.rsqrt(var + LN_EPS) * lng_ref[...] + lnb_ref[...]
    return u, vn


def _gmlp_prompt_kernel(x_ref, g_ref, win_ref, lng_ref, lnb_ref, ws_ref, bias_ref, wout_ref,
                        y_ref, vn_ref, *, tm):
    x = x_ref[...]
    u, vn = _gmlp_front(x, g_ref, win_ref, lng_ref, lnb_ref)
    vn_ref[...] = vn
    ck = GMLP_CHUNK
    gc = vn.shape[1] // GMLP_GROUPS
    row_i = lax.broadcasted_iota(jnp.int32, (ck, ck), 0)
    col_i = lax.broadcasted_iota(jnp.int32, (ck, ck), 1)
    wc = [jnp.where(row_i >= col_i, ws_ref[g], 0.0).astype(BF16) for g in range(GMLP_GROUPS)]
    vb = vn.astype(BF16)
    mix = jnp.concatenate(
        [jnp.concatenate([_dot(wc[g], vb[c * ck:(c + 1) * ck, g * gc:(g + 1) * gc])
                          for g in range(GMLP_GROUPS)], axis=-1) + bias_ref[...]
         for c in range(tm // ck)], axis=0)
    y_ref[...] = x + _dot((u * mix).astype(BF16), wout_ref[...])


def _gmlp_prompt(x, g, w_in, ln_g, ln_b, w_s, bias, w_out, tm):
    n, d = x.shape
    dg = w_out.shape[0]
    kern = functools.partial(_gmlp_prompt_kernel, tm=tm)
    return pl.pallas_call(
        kern,
        grid=(n // tm,),
        in_specs=[pl.BlockSpec((tm, d), lambda i: (i, 0)), _full((1, d)), _full(w_in.shape),
                  _full((1, dg)), _full((1, dg)), _full(w_s.shape), _full(bias.shape),
                  _full(w_out.shape)],
        out_specs=[pl.BlockSpec((tm, d), lambda i: (i, 0)), pl.BlockSpec((tm, dg), lambda i: (i, 0))],
        out_shape=[jax.ShapeDtypeStruct((n, d), F32), jax.ShapeDtypeStruct((n, dg), F32)],
        compiler_params=_params("parallel"),
        name="gmlp_prompt",
    )(x, g, w_in, ln_g, ln_b, w_s, bias, w_out)


def _gmlp_sample_kernel(ws_ref, bs_ref, x_ref, g_ref, win_ref, lng_ref, lnb_ref, wout_ref,
                        y_ref, vn_ref, *, n_t):
    x = x_ref[...]
    u, vn = _gmlp_front(x, g_ref, win_ref, lng_ref, lnb_ref)
    vn_ref[...] = vn
    bd = x.shape[0] // n_t
    gc = vn.shape[1] // GMLP_GROUPS
    rows = []
    for t in range(n_t):
        parts = []
        for g in range(GMLP_GROUPS):
            sl = slice(g * gc, (g + 1) * gc)
            acc = jnp.full((bd, gc), bs_ref[g * n_t + t], F32)
            for s in range(t + 1):
                acc = acc + ws_ref[(g * n_t + t) * n_t + s] * vn[s * bd:(s + 1) * bd, sl]
            parts.append(acc)
        rows.append(jnp.concatenate(parts, axis=-1))
    mix = jnp.concatenate(rows, axis=0)
    y_ref[...] = x + _dot((u * mix).astype(BF16), wout_ref[...])


def _gmlp_sample(x, g, w_in, ln_g, ln_b, ws_flat, bs_flat, w_out, n_t):
    n, d = x.shape
    dg = w_out.shape[0]
    kern = functools.partial(_gmlp_sample_kernel, n_t=n_t)
    smem = pl.BlockSpec(memory_space=pltpu.SMEM)
    return pl.pallas_call(
        kern,
        grid=(1,),
        in_specs=[smem, smem, _full(x.shape), _full((1, d)), _full(w_in.shape), _full((1, dg)),
                  _full((1, dg)), _full(w_out.shape)],
        out_specs=[_full((n, d)), _full((n, dg))],
        out_shape=[jax.ShapeDtypeStruct((n, d), F32), jax.ShapeDtypeStruct((n, dg), F32)],
        compiler_params=_params("arbitrary"),
        name="gmlp_sample",
    )(ws_flat, bs_flat, x, g, w_in, ln_g, ln_b, w_out)


ROUTE_LANES = 128
SLOT_ALIGN = 16


def _router_kernel(x_ref, g_ref, rw_ref, rb_ref, tri_ref, hn_ref, col_ref, row_ref, meta_ref):
    ne = MOE_GROUPS * MOE_PER_GROUP
    hb = _rms(x_ref[...], g_ref[...]).astype(BF16)
    hn_ref[...] = hb
    logits = _dot(hb, rw_ref[...]) + rb_ref[...]
    lane = lax.broadcasted_iota(jnp.int32, logits.shape, 1)
    lg = jnp.where((lane >= ne) & (lane < ne + MOE_GROUPS), logits, -jnp.inf)
    g_max = jnp.max(lg, axis=-1, keepdims=True)
    g_idx = jnp.min(jnp.where(lg == g_max, lane, ROUTE_LANES), axis=-1, keepdims=True) - ne
    g_p = 1.0 / jnp.sum(jnp.exp(lg - g_max), axis=-1, keepdims=True)
    first = g_idx * MOE_PER_GROUP
    l1 = jnp.where((lane >= first) & (lane < first + MOE_PER_GROUP), logits, -jnp.inf)
    m1 = jnp.max(l1, axis=-1, keepdims=True)
    e1 = jnp.min(jnp.where(l1 == m1, lane, ROUTE_LANES), axis=-1, keepdims=True)
    l2 = jnp.where(lane == e1, -jnp.inf, l1)
    m2 = jnp.max(l2, axis=-1, keepdims=True)
    e2 = jnp.min(jnp.where(l2 == m2, lane, ROUTE_LANES), axis=-1, keepdims=True)
    z = jnp.sum(jnp.exp(l1 - m1), axis=-1, keepdims=True)
    p1 = 1.0 / z
    p2 = jnp.exp(m2 - m1) / z
    w1 = g_p * p1 / (p1 + p2)
    w2 = g_p * p2 / (p1 + p2)
    hit1 = lane == e1
    hit2 = lane == e2
    routed = jnp.where(hit1, 1.0, 0.0) + jnp.where(hit2, 1.0, 0.0)
    rank = _dot(tri_ref[...], routed.astype(BF16))
    cnt = jnp.sum(routed, axis=0, keepdims=True)
    pad = jnp.floor((cnt + (SLOT_ALIGN - 1)) * (1.0 / SLOT_ALIGN)) * SLOT_ALIGN
    lane_r = lax.broadcasted_iota(jnp.int32, cnt.shape, 1)
    incl = pad
    shift = 1
    while shift < ne:
        incl = incl + jnp.where(lane_r >= shift, pltpu.roll(incl, shift, axis=1), 0.0)
        shift *= 2
    off = incl - pad
    slot = rank + off
    s1 = jnp.sum(jnp.where(hit1, slot, 0.0), axis=-1, keepdims=True)
    s2 = jnp.sum(jnp.where(hit2, slot, 0.0), axis=-1, keepdims=True)
    col = jnp.where(lane == 0, s1, jnp.where(lane == 1, s2, jnp.where(lane == 2, w1,
                                                                     jnp.where(lane == 3, w2, 0.0))))
    col_ref[...] = col
    row_ref[...] = col.T[:row_ref.shape[0]]
    meta = jnp.where(lane_r < ne, cnt, jnp.where(lane_r < 2 * ne, pltpu.roll(off, ne, axis=1), 0.0))
    meta_ref[...] = meta.astype(jnp.int32)


def _router(x, g, rw, rb, tri, tm):
    n, d = x.shape
    nt = n // tm
    return pl.pallas_call(
        _router_kernel,
        grid=(nt,),
        in_specs=[pl.BlockSpec((tm, d), lambda i: (i, 0)), _full((1, d)), _full(rw.shape),
                  _full(rb.shape), _full(tri.shape)],
        out_specs=[pl.BlockSpec((tm, d), lambda i: (i, 0)),
                   pl.BlockSpec((tm, ROUTE_LANES), lambda i: (i, 0)),
                   pl.BlockSpec((8, tm), lambda i: (0, i)),
                   pl.BlockSpec((None, 1, ROUTE_LANES), lambda i: (i, 0, 0))],
        out_shape=[jax.ShapeDtypeStruct((n, d), BF16), jax.ShapeDtypeStruct((n, ROUTE_LANES), F32),
                   jax.ShapeDtypeStruct((8, n), F32), jax.ShapeDtypeStruct((nt, 1, ROUTE_LANES), jnp.int32)],
        compiler_params=_params("arbitrary"),
        name="moe_router",
    )(x, g, rw, rb, tri)


def _moe_routed_kernel(cnt_ref, off_ref, hn_ref, col_ref, row_ref, wg_ref, wu_ref, wd_ref, x_ref,
                       y_ref, xs_ref, ys_ref, gs_ref, *, mc, sb):
    i = pl.program_id(0)
    e = pl.program_id(1)
    ne = pl.num_programs(1)
    tm = hn_ref.shape[0]
    n_blk = xs_ref.shape[0] // sb

    @pl.when(e == 0)
    def _():
        s1, s2, w1, w2 = (row_ref[r:r + 1, :] for r in range(4))
        hn = hn_ref[...]
        for blk in range(n_blk):
            sid = (lax.broadcasted_iota(jnp.int32, (sb, tm), 0) + blk * sb).astype(F32)
            h1 = s1 == sid
            h2 = s2 == sid
            onehot = (jnp.where(h1, 1.0, 0.0) + jnp.where(h2, 1.0, 0.0)).astype(BF16)
            xs_ref[blk * sb:(blk + 1) * sb, :] = _dot(onehot, hn).astype(BF16)
            gs_ref[blk * sb:(blk + 1) * sb, :] = jnp.sum(
                jnp.where(h1, w1, 0.0) + jnp.where(h2, w2, 0.0), axis=-1, keepdims=True)
        ys_ref[...] = jnp.zeros_like(ys_ref)

    count = cnt_ref[i * ne + e]
    first = off_ref[i * ne + e]

    def chunk(k, carry):
        r0 = pl.multiple_of(first + k * mc, SLOT_ALIGN)
        xg = xs_ref[pl.ds(r0, mc), :]
        a = _dot(xg, wg_ref[...])
        b = _dot(xg, wu_ref[...])
        hid = (a * jax.nn.sigmoid(a)) * b * gs_ref[pl.ds(r0, mc), :]
        ys_ref[pl.ds(r0, mc), :] = _dot(hid.astype(BF16), wd_ref[...]).astype(BF16)
        return carry

    lax.fori_loop(0, (count + mc - 1) // mc, chunk, 0)

    @pl.when(e == ne - 1)
    def _():
        s1 = col_ref[:, 0:1]
        s2 = col_ref[:, 1:2]
        acc = x_ref[...]
        for blk in range(n_blk):
            sid = (lax.broadcasted_iota(jnp.int32, (tm, sb), 1) + blk * sb).astype(F32)
            onehot = (jnp.where(s1 == sid, 1.0, 0.0) + jnp.where(s2 == sid, 1.0, 0.0)).astype(BF16)
            acc = acc + _dot(onehot, ys_ref[blk * sb:(blk + 1) * sb, :])
        y_ref[...] = acc


def _round_up(a, b):
    return -(-a // b) * b


def _moe_routed(cnt, off, hn, col, row, wg, wu, wd, x, tm):
    n, d = x.shape
    ne, _, de = wg.shape
    sb = 256
    mc = _round_up(2 * tm * 5 // (ne * 4), SLOT_ALIGN)
    n_slots = _round_up(2 * tm + (ne - 1) * SLOT_ALIGN + mc, sb)
    kern = functools.partial(_moe_routed_kernel, mc=mc, sb=sb)
    tile = lambda i, e, *_: (i, 0)
    expert = lambda i, e, *_: (e, 0, 0)
    return pl.pallas_call(
        kern,
        grid_spec=pltpu.PrefetchScalarGridSpec(
            num_scalar_prefetch=2,
            grid=(n // tm, ne),
            in_specs=[pl.BlockSpec((tm, d), tile), pl.BlockSpec((tm, ROUTE_LANES), tile),
                      pl.BlockSpec((8, tm), lambda i, e, *_: (0, i)),
                      pl.BlockSpec((None, d, de), expert), pl.BlockSpec((None, d, de), expert),
                      pl.BlockSpec((None, de, d), expert), pl.BlockSpec((tm, d), tile)],
            out_specs=pl.BlockSpec((tm, d), tile),
            scratch_shapes=[pltpu.VMEM((n_slots, d), BF16), pltpu.VMEM((n_slots, d), BF16),
                            pltpu.VMEM((n_slots, 1), F32)],
        ),
        out_shape=jax.ShapeDtypeStruct((n, d), F32),
        compiler_params=_params("arbitrary", "arbitrary"),
        name="moe_routed",
    )(cnt, off, hn, col, row, wg, wu, wd, x)


def _moe(x, g, rw, rb, wg, wu, wd, tm):
    ne = wg.shape[0]
    tri = jnp.tril(jnp.ones((tm, tm), BF16), -1)
    hn, col, row, meta = _router(x, g, rw, rb, tri, tm)
    cnt = meta[:, 0, :ne].reshape(-1)
    off = meta[:, 0, ne:2 * ne].reshape(-1)
    return _moe_routed(cnt, off, hn, col, row, wg, wu, wd, x, tm)


def _final_norm_kernel(x_ref, g_ref, y_ref):
    y_ref[...] = _rms(x_ref[...], g_ref[...])


def _final_norm(x, g, tm):
    n, d = x.shape
    return pl.pallas_call(
        _final_norm_kernel,
        grid=(n // tm,),
        in_specs=[pl.BlockSpec((tm, d), lambda i: (i, 0)), _full((1, d))],
        out_specs=pl.BlockSpec((tm, d), lambda i: (i, 0)),
        out_shape=jax.ShapeDtypeStruct((n, d), F32),
        compiler_params=_params("parallel"),
        name="final_norm",
    )(x, g)


def _tile(n, pref):
    t = min(n, pref)
    assert n % t == 0, (n, t)
    return t


def kernel(x_prompt, x_sample, state_pool, cache_k, cache_v, page_table, norm_mix, norm_ffn, norm_final, pool_w_in, pool_w_grp, pool_scale, pool_w_out, attn_w_qkv, attn_w_out, gmlp_w_in, gmlp_ln_g, gmlp_ln_b, gmlp_w_s, gmlp_b_s, gmlp_w_out, moe_rg_w, moe_rg_b, moe_re_w, moe_re_b, moe_w_gate, moe_w_up, moe_w_down):
    batch, seq, d = x_prompt.shape
    dec_b, n_t, _ = x_sample.shape
    depth = norm_mix.shape[0]
    n_heads = N_HEADS
    hd = d // n_heads
    n_pages = page_table.shape[1]
    past_len = n_pages * PAGE_SIZE
    assert seq % MOBA_BLOCK == 0 and seq % GMLP_CHUNK == 0 and n_t <= GMLP_CHUNK
    assert past_len % GMLP_CHUNK == 0 and past_len >= POOL_STATE

    n_p, n_s = batch * seq, dec_b * n_t
    xp = x_prompt.reshape(n_p, d)
    xs = x_sample.transpose(1, 0, 2).reshape(n_s, d)
    tp = _tile(seq, 512)
    row = lambda a: a.reshape(1, -1)

    def to_batch_major(a):
        return a.reshape((n_t, dec_b) + a.shape[1:]).swapaxes(0, 1)

    pool_p, pool_s, k_p, v_p, k_s, v_s, gv_p, gv_s = [], [], [], [], [], [], [], []
    for i in range(depth):
        kind, j = i % N_MIXERS, i // N_MIXERS
        g_mix = row(norm_mix[i])
        if kind == 0:
            w_in, w_grp, w_out = (a.astype(BF16) for a in (pool_w_in[j], pool_w_grp[j], pool_w_out[j]))
            scale = row(pool_scale[j])
            xp, st = _pool_prompt(xp, g_mix, w_in, w_grp, scale, w_out, batch, seq, tp)
            pool_p.append(st)
            xs, u_s = _pool_sample(xs, state_pool[j].transpose(1, 0, 2), g_mix, w_in, w_grp, scale,
                                   w_out, n_t, past_len)
            pool_s.append(jnp.concatenate([state_pool[j][:, n_t:], to_batch_major(u_s)], axis=1))
        elif kind == 1:
            w_qkv, w_out = attn_w_qkv[j].astype(BF16), attn_w_out[j].astype(BF16)
            qp, kp, vp = _norm_matmul(xp, g_mix, w_qkv, 3, tp)
            qs, ks, vs = _norm_matmul(xs, g_mix, w_qkv, 3, _tile(n_s, 512))
            ap = _moba_prompt(qp, kp, vp, batch, seq, n_heads)
            a_s = _moba_sample(qs, ks, vs, cache_k, cache_v, j, page_table, n_t, n_heads)
            xp = _matmul_residual(ap, w_out, xp, tp)
            xs = _matmul_residual(a_s, w_out, xs, _tile(n_s, 512))
            k_p.append(kp.reshape(batch, seq, n_heads, hd))
            v_p.append(vp.reshape(batch, seq, n_heads, hd))
            k_s.append(to_batch_major(ks).reshape(dec_b, n_t, n_heads, hd))
            v_s.append(to_batch_major(vs).reshape(dec_b, n_t, n_heads, hd))
        else:
            w_in, w_out = gmlp_w_in[j].astype(BF16), gmlp_w_out[j].astype(BF16)
            ln_g, ln_b = row(gmlp_ln_g[j]), row(gmlp_ln_b[j])
            gc = w_out.shape[0] // GMLP_GROUPS
            bias = jnp.broadcast_to(gmlp_b_s[j].T[:, :, None], (GMLP_CHUNK, GMLP_GROUPS, gc))
            xp, vn_p = _gmlp_prompt(xp, g_mix, w_in, ln_g, ln_b, gmlp_w_s[j],
                                    bias.reshape(GMLP_CHUNK, GMLP_GROUPS * gc), w_out, tp)
            xs, vn_s = _gmlp_sample(xs, g_mix, w_in, ln_g, ln_b,
                                    gmlp_w_s[j][:, :n_t, :n_t].reshape(-1),
                                    gmlp_b_s[j][:, :n_t].reshape(-1), w_out, n_t)
            gv_p.append(vn_p.reshape(batch, seq, -1))
            gv_s.append(to_batch_major(vn_s))

        g_ffn = row(norm_ffn[i])
        ne = moe_re_w.shape[2]
        rw = jnp.zeros((d, ROUTE_LANES), F32).at[:, :ne].set(moe_re_w[i]).at[:, ne:ne + MOE_GROUPS].set(moe_rg_w[i])
        rb = jnp.zeros((1, ROUTE_LANES), F32).at[0, :ne].set(moe_re_b[i]).at[0, ne:ne + MOE_GROUPS].set(moe_rg_b[i])
        wg, wu, wd = (a.astype(BF16) for a in (moe_w_gate[i], moe_w_up[i], moe_w_down[i]))
        xp = _moe(xp, g_ffn, rw.astype(BF16), rb, wg, wu, wd, _tile(n_p, 1024))
        xs = _moe(xs, g_ffn, rw.astype(BF16), rb, wg, wu, wd, _tile(n_s, 512))

    g_fin = row(norm_final)
    y_p = _final_norm(xp, g_fin, tp).reshape(batch, seq, d)
    y_s = to_batch_major(_final_norm(xs, g_fin, _tile(n_s, 512)))
    return (y_p, y_s, jnp.stack(pool_p), jnp.stack(pool_s), jnp.stack(k_p), jnp.stack(v_p),
            jnp.stack(k_s), jnp.stack(v_s), jnp.stack(gv_p), jnp.stack(gv_s))
```

```python
import functools

import jax
import jax.numpy as jnp
from jax import lax
from jax.experimental import pallas as pl
from jax.experimental.pallas import tpu as pltpu

F32 = jnp.float32
BF16 = jnp.bfloat16

RMS_EPS = 1e-6
LN_EPS = 1e-5
N_MIXERS = 3
POOL_WINDOWS = (2, 4, 8, 16)
POOL_STATE = max(POOL_WINDOWS) - 1
MOBA_BLOCK = 256
MOBA_TOPK = 3
GMLP_CHUNK = 128
GMLP_GROUPS = 8
MOE_GROUPS = 4
MOE_PER_GROUP = 4
PAGE_SIZE = 128
N_HEADS = 8

V7X_VMEM_LIMIT_BYTES = 56 * 1024 * 1024
NEG_BIG = -1e30
LOG2_E = 1.4426950408889634


def _params(*sem):
    return pltpu.CompilerParams(dimension_semantics=sem, vmem_limit_bytes=V7X_VMEM_LIMIT_BYTES)


def _rms(x, g):
    return x * lax.rsqrt(jnp.mean(x * x, axis=-1, keepdims=True) + RMS_EPS) * g


def _dot(a, b):
    return jnp.dot(a, b, preferred_element_type=F32)


def _dot_nt(a, b, precision=None):
    return lax.dot_general(a, b, (((1,), (1,)), ((), ())), preferred_element_type=F32,
                           precision=precision)


def _full(shape):
    return pl.BlockSpec(shape, lambda *_: (0,) * len(shape))


def _norm_matmul_kernel(x_ref, g_ref, w_ref, *out_refs):
    h = _rms(x_ref[...], g_ref[...]).astype(BF16)
    f = w_ref.shape[1] // len(out_refs)
    for i, o_ref in enumerate(out_refs):
        o_ref[...] = _dot(h, w_ref[:, i * f:(i + 1) * f])


def _norm_matmul(x, g, w, n_out, tm):
    n, d = x.shape
    f = w.shape[1] // n_out
    return pl.pallas_call(
        _norm_matmul_kernel,
        grid=(n // tm,),
        in_specs=[pl.BlockSpec((tm, d), lambda i: (i, 0)), _full((1, d)), _full(w.shape)],
        out_specs=[pl.BlockSpec((tm, f), lambda i: (i, 0))] * n_out,
        out_shape=[jax.ShapeDtypeStruct((n, f), F32)] * n_out,
        compiler_params=_params("parallel"),
        name="norm_matmul",
    )(x, g, w)


def _matmul_residual_kernel(a_ref, w_ref, x_ref, y_ref):
    y_ref[...] = x_ref[...] + _dot(a_ref[...].astype(BF16), w_ref[...])


def _matmul_residual(a, w, x, tm):
    n, d = x.shape
    k = a.shape[1]
    return pl.pallas_call(
        _matmul_residual_kernel,
        grid=(n // tm,),
        in_specs=[pl.BlockSpec((tm, k), lambda i: (i, 0)), _full(w.shape),
                  pl.BlockSpec((tm, d), lambda i: (i, 0))],
        out_specs=pl.BlockSpec((tm, d), lambda i: (i, 0)),
        out_shape=jax.ShapeDtypeStruct((n, d), F32),
        compiler_params=_params("parallel"),
        name="matmul_residual",
    )(a, w, x)


def _pool_tail(r, x, wgrp_ref, scale_ref, wout_ref):
    gc = wgrp_ref.shape[1]
    rb = r.astype(BF16)
    t = jnp.concatenate([_dot(rb[:, g * gc:(g + 1) * gc], wgrp_ref[g])
                         for g in range(len(POOL_WINDOWS))], axis=-1)
    t = (t * scale_ref[...]).astype(BF16)
    return x + _dot(t, wout_ref[...])


def _pool_prompt_kernel(x_ref, g_ref, win_ref, wgrp_ref, scale_ref, wout_ref,
                        y_ref, st_ref, carry_ref, *, ts, ns):
    s = pl.program_id(1)
    halo = carry_ref.shape[0]
    gc = wgrp_ref.shape[1]

    @pl.when(s == 0)
    def _():
        carry_ref[...] = jnp.zeros_like(carry_ref)

    x = x_ref[...]
    h = _rms(x, g_ref[...]).astype(BF16)
    u = _dot(h, win_ref[...])
    ext = jnp.concatenate([carry_ref[...], u], axis=0)
    pos = s * ts + lax.broadcasted_iota(jnp.int32, (ts, 1), 0)
    parts = []
    acc = ext
    width = 1
    for w in POOL_WINDOWS:
        while width < w:
            acc = acc + pltpu.roll(acc, width, axis=0)
            width *= 2
        assert width == w
        cnt = jnp.minimum(pos + 1, w).astype(F32)
        parts.append(acc[halo:, :gc] / cnt)
        acc = acc[:, gc:]
    r = jnp.concatenate(parts, axis=-1) - u
    y_ref[...] = _pool_tail(r, x, wgrp_ref, scale_ref, wout_ref)
    carry_ref[...] = u[ts - halo:, :]

    @pl.when(s == ns - 1)
    def _():
        st_ref[...] = carry_ref[halo - POOL_STATE:, :]


def _pool_prompt(x, g, w_in, w_grp, scale, w_out, batch, seq, ts):
    n, d = x.shape
    ns = seq // ts
    halo = 16
    kern = functools.partial(_pool_prompt_kernel, ts=ts, ns=ns)
    return pl.pallas_call(
        kern,
        grid=(batch, ns),
        in_specs=[pl.BlockSpec((ts, d), lambda b, s: (b * ns + s, 0)), _full((1, d)),
                  _full(w_in.shape), _full(w_grp.shape), _full((1, d)), _full(w_out.shape)],
        out_specs=[pl.BlockSpec((ts, d), lambda b, s: (b * ns + s, 0)),
                   pl.BlockSpec((None, POOL_STATE, d), lambda b, s: (b, 0, 0))],
        out_shape=[jax.ShapeDtypeStruct((n, d), F32),
                   jax.ShapeDtypeStruct((batch, POOL_STATE, d), F32)],
        scratch_shapes=[pltpu.VMEM((halo, d), F32)],
        compiler_params=_params("parallel", "arbitrary"),
        name="pool_prompt",
    )(x, g, w_in, w_grp, scale, w_out)


def _pool_sample_kernel(x_ref, pref_ref, g_ref, win_ref, wgrp_ref, scale_ref, wout_ref,
                        y_ref, u_ref, *, n_t, pos0):
    bd = x_ref.shape[0] // n_t
    gc = wgrp_ref.shape[1]
    x = x_ref[...]
    h = _rms(x, g_ref[...]).astype(BF16)
    u = _dot(h, win_ref[...])
    u_ref[...] = u
    us = [u[t * bd:(t + 1) * bd] for t in range(n_t)]
    tail = [jnp.zeros_like(us[0])]
    for m in range(1, POOL_STATE + 1):
        tail.append(tail[-1] + pref_ref[POOL_STATE - m])
    rows = []
    for t in range(n_t):
        parts = []
        for gi, w in enumerate(POOL_WINDOWS):
            sl = slice(gi * gc, (gi + 1) * gc)
            n_new = min(w, t + 1)
            wsum = us[t][:, sl]
            for k in range(1, n_new):
                wsum = wsum + us[t - k][:, sl]
            if w - n_new:
                wsum = wsum + tail[w - n_new][:, sl]
            parts.append(wsum / float(min(pos0 + t + 1, w)))
        rows.append(jnp.concatenate(parts, axis=-1) - us[t])
    r = jnp.concatenate(rows, axis=0)
    y_ref[...] = _pool_tail(r, x, wgrp_ref, scale_ref, wout_ref)


def _pool_sample(x, pref_t, g, w_in, w_grp, scale, w_out, n_t, pos0):
    n, d = x.shape
    kern = functools.partial(_pool_sample_kernel, n_t=n_t, pos0=pos0)
    return pl.pallas_call(
        kern,
        grid=(1,),
        in_specs=[_full(x.shape), _full(pref_t.shape), _full((1, d)), _full(w_in.shape),
                  _full(w_grp.shape), _full((1, d)), _full(w_out.shape)],
        out_specs=[_full((n, d)), _full((n, d))],
        out_shape=[jax.ShapeDtypeStruct((n, d), F32)] * 2,
        compiler_params=_params("arbitrary"),
        name="pool_sample",
    )(x, pref_t, g, w_in, w_grp, scale, w_out)


def _top_blocks(gate, n_valid, n_blocks):
    idx = lax.broadcasted_iota(jnp.int32, gate.shape, 0)
    valid = idx < n_valid
    rows = []
    for j in range(n_blocks):
        gj = gate[j:j + 1, :]
        beats = valid & ((gate > gj) | ((gate == gj) & (idx < j)))
        rank = jnp.sum(beats.astype(F32), axis=0, keepdims=True)
        rows.append(jnp.where(rank < MOBA_TOPK, jnp.where(j < n_valid, 1.0, 0.0), 0.0))
    return rows


def _moba_prompt_kernel(q_ref, k_ref, v_ref, o_ref, kb_ref, vt_ref, kmean_ref, sel_ref,
                        *, nb, group, hps, hd):
    qi = pl.program_id(2)
    blk = MOBA_BLOCK
    scale = hd ** -0.5
    heads = range(hps)
    lanes = lambda h: slice(h * hd, (h + 1) * hd)

    @pl.when(qi == 0)
    def _():
        k = k_ref[...]
        kb_ref[...] = k.astype(BF16)
        kmean_ref[...] = jnp.mean(k.reshape(nb, blk, hps * hd), axis=1)
        vt_ref[...] = v_ref[...].T.astype(BF16)

    q = q_ref[...]
    for h in heads:
        gate = _dot_nt(kmean_ref[:, lanes(h)], q[:, lanes(h)], precision=lax.Precision.HIGHEST)
        for j, row in enumerate(_top_blocks(gate, qi, nb)):
            sel_ref[h * nb + j:h * nb + j + 1, :] = row
    qb = [q[:, lanes(h)].astype(BF16) for h in heads]
    key_i = lax.broadcasted_iota(jnp.int32, (blk, blk), 0)
    qry_i = lax.broadcasted_iota(jnp.int32, (blk, blk), 1)
    scale2 = scale * LOG2_E

    def scores(h, j, n):
        kj = kb_ref[pl.ds(pl.multiple_of(j * blk, blk), n * blk), lanes(h)]
        return _dot_nt(kj, qb[h]) * scale2

    def pv(h, j, n, p):
        vj = vt_ref[lanes(h), pl.ds(pl.multiple_of(j * blk, blk), n * blk)]
        return _dot(vj, p.astype(BF16))

    carry = []
    for h in heads:
        s = jnp.where(key_i <= qry_i, scores(h, qi, 1), NEG_BIG)
        m = jnp.max(s, axis=0, keepdims=True)
        p = jnp.exp2(s - m)
        carry.append((m, jnp.sum(p, axis=0, keepdims=True), pv(h, qi, 1, p)))

    def body(g, carry):
        j = g * group
        out = []
        for h, (m, l, acc) in zip(heads, carry):
            keep = jnp.concatenate([jnp.broadcast_to(sel_ref[pl.ds(h * nb + j + r, 1), :], (blk, blk))
                                    for r in range(group)], axis=0)
            s = jnp.where(keep > 0.0, scores(h, j, group), NEG_BIG)
            m_new = jnp.maximum(m, jnp.max(s, axis=0, keepdims=True))
            alpha = jnp.exp2(m - m_new)
            p = jnp.exp2(s - m_new)
            out.append((m_new, alpha * l + jnp.sum(p, axis=0, keepdims=True),
                        alpha * acc + pv(h, j, group, p)))
        return tuple(out)

    carry = lax.fori_loop(0, (qi + group - 1) // group, body, tuple(carry))
    for h, (m, l, acc) in zip(heads, carry):
        o_ref[:, lanes(h)] = (acc / l).T.astype(o_ref.dtype)


def _moba_prompt(q, k, v, batch, seq, n_heads):
    n, d = q.shape
    hd = d // n_heads
    nb = seq // MOBA_BLOCK
    group = 2 if nb % 2 == 0 else 1
    hps = 2 if n_heads % 2 == 0 else 1
    kern = functools.partial(_moba_prompt_kernel, nb=nb, group=group, hps=hps, hd=hd)
    return pl.pallas_call(
        kern,
        grid=(batch, n_heads // hps, nb),
        in_specs=[pl.BlockSpec((MOBA_BLOCK, hps * hd), lambda b, h, i: (b * nb + i, h)),
                  pl.BlockSpec((seq, hps * hd), lambda b, h, i: (b, h)),
                  pl.BlockSpec((seq, hps * hd), lambda b, h, i: (b, h))],
        out_specs=pl.BlockSpec((MOBA_BLOCK, hps * hd), lambda b, h, i: (b * nb + i, h)),
        out_shape=jax.ShapeDtypeStruct((n, d), BF16),
        scratch_shapes=[pltpu.VMEM((seq, hps * hd), BF16), pltpu.VMEM((hps * hd, seq), BF16),
                        pltpu.VMEM((nb, hps * hd), F32), pltpu.VMEM((hps * nb, MOBA_BLOCK), F32)],
        compiler_params=_params("parallel", "parallel", "arbitrary"),
        name="moba_prompt",
    )(q, k, v)


def _moba_sample_kernel(pt_ref, q_ref, kn_ref, vn_ref, *refs, n_t, n_pages, pps):
    del pt_ref
    kp_refs, vp_refs = refs[:pps], refs[pps:2 * pps]
    o_ref, wbd_ref, s_ref, vall_ref = refs[2 * pps:]
    p_i = pl.program_id(1)
    nh, hd = q_ref.shape[2], q_ref.shape[3]
    lanes = wbd_ref.shape[1]
    scale = hd ** -0.5
    n_past = n_pages // (MOBA_BLOCK // PAGE_SIZE)
    head_i = lax.broadcasted_iota(jnp.int32, (nh, hd), 0)

    @pl.when(p_i == 0)
    def _():
        rows = [jnp.concatenate([jnp.where(head_i == h, q_ref[t, 0], 0.0) for h in range(nh)], axis=-1)
                for t in range(n_t)]
        rows.append(jnp.zeros((lanes - n_t * nh, nh * hd), F32))
        wbd_ref[...] = jnp.concatenate(rows, axis=0).T.astype(BF16)

    for g, (kp_ref, vp_ref) in enumerate(zip(kp_refs, vp_refs)):
        off = pl.multiple_of((p_i * pps + g) * PAGE_SIZE, PAGE_SIZE)
        acc = None
        for h in range(nh):
            kh = kp_ref[pl.ds(h, PAGE_SIZE, stride=nh), :].astype(BF16)
            part = _dot(kh, wbd_ref[h * hd:(h + 1) * hd, :])
            acc = part if acc is None else acc + part
            vall_ref[pl.ds(off, PAGE_SIZE), h * hd:(h + 1) * hd] = (
                vp_ref[pl.ds(h, PAGE_SIZE, stride=nh), :].astype(BF16))
        s_ref[pl.ds(off, PAGE_SIZE), :] = acc

    @pl.when(p_i == n_pages // pps - 1)
    def _():
        s_all = s_ref[...]
        blocks = [s_all[j * MOBA_BLOCK:(j + 1) * MOBA_BLOCK] for j in range(n_past)]
        gate = [jnp.sum(b, axis=0, keepdims=True) * (1.0 / MOBA_BLOCK) for b in blocks]
        sel = []
        for j in range(n_past):
            rank = jnp.zeros_like(gate[j])
            for i in range(n_past):
                if i != j:
                    beats = (gate[i] >= gate[j]) if i < j else (gate[i] > gate[j])
                    rank = rank + beats.astype(F32)
            sel.append(rank < MOBA_TOPK)
        s_past = jnp.concatenate(
            [jnp.where(sel[j], blocks[j] * scale, NEG_BIG) for j in range(n_past)], axis=0)
        m_row = jnp.max(s_past, axis=0, keepdims=True)
        p_past = jnp.exp(s_past - m_row)
        l_row = jnp.sum(p_past, axis=0, keepdims=True)
        full = _dot(p_past.T.astype(BF16), vall_ref[...])
        stats = jnp.concatenate([m_row, l_row, jnp.zeros((lanes - 2, lanes), F32)], axis=0).T
        for t in range(n_t):
            rows = full[t * nh:(t + 1) * nh]
            out = functools.reduce(
                lambda a, b: a + b,
                [jnp.where(head_i == h, rows[:, h * hd:(h + 1) * hd], 0.0) for h in range(nh)])
            m_past = stats[t * nh:(t + 1) * nh, 0:1]
            l_past = stats[t * nh:(t + 1) * nh, 1:2]
            s_new = [jnp.sum(q_ref[t, 0] * kn_ref[t2, 0], axis=-1, keepdims=True) * scale
                     for t2 in range(t + 1)]
            m = functools.reduce(jnp.maximum, s_new, m_past)
            alpha = jnp.exp(m_past - m)
            out = alpha * out
            l = alpha * l_past
            for t2 in range(t + 1):
                p_new = jnp.exp(s_new[t2] - m)
                l = l + p_new
                out = out + p_new * vn_ref[t2, 0]
            o_ref[t, 0] = out / l


def _moba_sample(q, k, v, cache_k, cache_v, layer, page_table, n_t, n_heads):
    n, d = q.shape
    bd = n // n_t
    hd = d // n_heads
    n_pages = page_table.shape[1]
    past = n_pages * PAGE_SIZE
    lanes = 128
    assert n_pages % (MOBA_BLOCK // PAGE_SIZE) == 0 and n_t <= MOBA_BLOCK and n_t * n_heads <= lanes - 2
    q4, k4, v4 = (a.reshape(n_t, bd, n_heads, hd) for a in (q, k, v))
    n_layers, n_phys = cache_k.shape[:2]
    ck, cv = (c.reshape(n_layers, n_phys, PAGE_SIZE * n_heads, hd) for c in (cache_k, cache_v))
    new_spec = pl.BlockSpec((n_t, 1, n_heads, hd), lambda b, p, pt: (0, b, 0, 0))
    pps = 8 if n_pages % 8 == 0 else 2
    page_specs = [pl.BlockSpec((None, None, PAGE_SIZE * n_heads, hd),
                               functools.partial(lambda g, b, p, pt: (layer, pt[b, p * pps + g], 0, 0), g))
                  for g in range(pps)]
    kern = functools.partial(_moba_sample_kernel, n_t=n_t, n_pages=n_pages, pps=pps)
    out = pl.pallas_call(
        kern,
        grid_spec=pltpu.PrefetchScalarGridSpec(
            num_scalar_prefetch=1,
            grid=(bd, n_pages // pps),
            in_specs=[new_spec, new_spec, new_spec] + page_specs + page_specs,
            out_specs=new_spec,
            scratch_shapes=[pltpu.VMEM((d, lanes), BF16), pltpu.VMEM((past, lanes), F32),
                            pltpu.VMEM((past, d), BF16)],
        ),
        out_shape=jax.ShapeDtypeStruct((n_t, bd, n_heads, hd), F32),
        compiler_params=_params("parallel", "arbitrary"),
        name="moba_sample",
    )(page_table, q4, k4, v4, *([ck] * pps), *([cv] * pps))
    return out.reshape(n, d)


def _gelu(z):
    return 0.5 * z * (1.0 + jnp.tanh(0.7978845608028654 * (z + 0.044715 * z * z * z)))


def _gmlp_front(x, g_ref, win_ref, lng_ref, lnb_ref):
    h = _rms(x, g_ref[...]).astype(BF16)
    z = _gelu(_dot(h, win_ref[...]))
    dg = z.shape[1] // 2
    u, v = z[:, :dg], z[:, dg:]
    mu = jnp.mean(v, axis=-1, keepdims=True)
    var = jnp.mean(jnp.square(v - mu), axis=-1, keepdims=True)
    vn = (v - mu) * lax.rsqrt(var + LN_EPS) * lng_ref[...] + lnb_ref[...]
    return u, vn


def _gmlp_prompt_kernel(x_ref, g_ref, win_ref, lng_ref, lnb_ref, ws_ref, bias_ref, wout_ref,
                        y_ref, vn_ref, *, tm):
    x = x_ref[...]
    u, vn = _gmlp_front(x, g_ref, win_ref, lng_ref, lnb_ref)
    vn_ref[...] = vn
    ck = GMLP_CHUNK
    gc = vn.shape[1] // GMLP_GROUPS
    row_i = lax.broadcasted_iota(jnp.int32, (ck, ck), 0)
    col_i = lax.broadcasted_iota(jnp.int32, (ck, ck), 1)
    wc = [jnp.where(row_i >= col_i, ws_ref[g], 0.0).astype(BF16) for g in range(GMLP_GROUPS)]
    vb = vn.astype(BF16)
    mix = jnp.concatenate(
        [jnp.concatenate([_dot(wc[g], vb[c * ck:(c + 1) * ck, g * gc:(g + 1) * gc])
                          for g in range(GMLP_GROUPS)], axis=-1) + bias_ref[...]
         for c in range(tm // ck)], axis=0)
    y_ref[...] = x + _dot((u * mix).astype(BF16), wout_ref[...])


def _gmlp_prompt(x, g, w_in, ln_g, ln_b, w_s, bias, w_out, tm):
    n, d = x.shape
    dg = w_out.shape[0]
    kern = functools.partial(_gmlp_prompt_kernel, tm=tm)
    return pl.pallas_call(
        kern,
        grid=(n // tm,),
        in_specs=[pl.BlockSpec((tm, d), lambda i: (i, 0)), _full((1, d)), _full(w_in.shape),
                  _full((1, dg)), _full((1, dg)), _full(w_s.shape), _full(bias.shape),
                  _full(w_out.shape)],
        out_specs=[pl.BlockSpec((tm, d), lambda i: (i, 0)), pl.BlockSpec((tm, dg), lambda i: (i, 0))],
        out_shape=[jax.ShapeDtypeStruct((n, d), F32), jax.ShapeDtypeStruct((n, dg), F32)],
        compiler_params=_params("parallel"),
        name="gmlp_prompt",
    )(x, g, w_in, ln_g, ln_b, w_s, bias, w_out)


def _gmlp_sample_kernel(ws_ref, bs_ref, x_ref, g_ref, win_ref, lng_ref, lnb_ref, wout_ref,
                        y_ref, vn_ref, *, n_t):
    x = x_ref[...]
    u, vn = _gmlp_front(x, g_ref, win_ref, lng_ref, lnb_ref)
    vn_ref[...] = vn
    bd = x.shape[0] // n_t
    gc = vn.shape[1] // GMLP_GROUPS
    rows = []
    for t in range(n_t):
        parts = []
        for g in range(GMLP_GROUPS):
            sl = slice(g * gc, (g + 1) * gc)
            acc = jnp.full((bd, gc), bs_ref[g * n_t + t], F32)
            for s in range(t + 1):
                acc = acc + ws_ref[(g * n_t + t) * n_t + s] * vn[s * bd:(s + 1) * bd, sl]
            parts.append(acc)
        rows.append(jnp.concatenate(parts, axis=-1))
    mix = jnp.concatenate(rows, axis=0)
    y_ref[...] = x + _dot((u * mix).astype(BF16), wout_ref[...])


def _gmlp_sample(x, g, w_in, ln_g, ln_b, ws_flat, bs_flat, w_out, n_t):
    n, d = x.shape
    dg = w_out.shape[0]
    kern = functools.partial(_gmlp_sample_kernel, n_t=n_t)
    smem = pl.BlockSpec(memory_space=pltpu.SMEM)
    return pl.pallas_call(
        kern,
        grid=(1,),
        in_specs=[smem, smem, _full(x.shape), _full((1, d)), _full(w_in.shape), _full((1, dg)),
                  _full((1, dg)), _full(w_out.shape)],
        out_specs=[_full((n, d)), _full((n, dg))],
        out_shape=[jax.ShapeDtypeStruct((n, d), F32), jax.ShapeDtypeStruct((n, dg), F32)],
        compiler_params=_params("arbitrary"),
        name="gmlp_sample",
    )(ws_flat, bs_flat, x, g, w_in, ln_g, ln_b, w_out)


ROUTE_LANES = 128
SLOT_ALIGN = 16


def _router_kernel(x_ref, g_ref, rw_ref, rb_ref, tri_ref, hn_ref, col_ref, row_ref, meta_ref):
    ne = MOE_GROUPS * MOE_PER_GROUP
    hb = _rms(x_ref[...], g_ref[...]).astype(BF16)
    hn_ref[...] = hb
    logits = _dot(hb, rw_ref[...]) + rb_ref[...]
    lane = lax.broadcasted_iota(jnp.int32, logits.shape, 1)
    lg = jnp.where((lane >= ne) & (lane < ne + MOE_GROUPS), logits, -jnp.inf)
    g_max = jnp.max(lg, axis=-1, keepdims=True)
    g_idx = jnp.min(jnp.where(lg == g_max, lane, ROUTE_LANES), axis=-1, keepdims=True) - ne
    g_p = 1.0 / jnp.sum(jnp.exp(lg - g_max), axis=-1, keepdims=True)
    first = g_idx * MOE_PER_GROUP
    l1 = jnp.where((lane >= first) & (lane < first + MOE_PER_GROUP), logits, -jnp.inf)
    m1 = jnp.max(l1, axis=-1, keepdims=True)
    e1 = jnp.min(jnp.where(l1 == m1, lane, ROUTE_LANES), axis=-1, keepdims=True)
    l2 = jnp.where(lane == e1, -jnp.inf, l1)
    m2 = jnp.max(l2, axis=-1, keepdims=True)
    e2 = jnp.min(jnp.where(l2 == m2, lane, ROUTE_LANES), axis=-1, keepdims=True)
    z = jnp.sum(jnp.exp(l1 - m1), axis=-1, keepdims=True)
    p1 = 1.0 / z
    p2 = jnp.exp(m2 - m1) / z
    w1 = g_p * p1 / (p1 + p2)
    w2 = g_p * p2 / (p1 + p2)
    hit1 = lane == e1
    hit2 = lane == e2
    routed = jnp.where(hit1, 1.0, 0.0) + jnp.where(hit2, 1.0, 0.0)
    rank = _dot(tri_ref[...], routed.astype(BF16))
    cnt = jnp.sum(routed, axis=0, keepdims=True)
    pad = jnp.floor((cnt + (SLOT_ALIGN - 1)) * (1.0 / SLOT_ALIGN)) * SLOT_ALIGN
    lane_r = lax.broadcasted_iota(jnp.int32, cnt.shape, 1)
    incl = pad
    shift = 1
    while shift < ne:
        incl = incl + jnp.where(lane_r >= shift, pltpu.roll(incl, shift, axis=1), 0.0)
        shift *= 2
    off = incl - pad
    slot = rank + off
    s1 = jnp.sum(jnp.where(hit1, slot, 0.0), axis=-1, keepdims=True)
    s2 = jnp.sum(jnp.where(hit2, slot, 0.0), axis=-1, keepdims=True)
    col = jnp.where(lane == 0, s1, jnp.where(lane == 1, s2, jnp.where(lane == 2, w1,
                                                                     jnp.where(lane == 3, w2, 0.0))))
    col_ref[...] = col
    row_ref[...] = col.T[:row_ref.shape[0]]
    meta = jnp.where(lane_r < ne, cnt, jnp.where(lane_r < 2 * ne, pltpu.roll(off, ne, axis=1), 0.0))
    meta_ref[...] = meta.astype(jnp.int32)


def _router(x, g, rw, rb, tri, tm):
    n, d = x.shape
    nt = n // tm
    return pl.pallas_call(
        _router_kernel,
        grid=(nt,),
        in_specs=[pl.BlockSpec((tm, d), lambda i: (i, 0)), _full((1, d)), _full(rw.shape),
                  _full(rb.shape), _full(tri.shape)],
        out_specs=[pl.BlockSpec((tm, d), lambda i: (i, 0)),
                   pl.BlockSpec((tm, ROUTE_LANES), lambda i: (i, 0)),
                   pl.BlockSpec((8, tm), lambda i: (0, i)),
                   pl.BlockSpec((None, 1, ROUTE_LANES), lambda i: (i, 0, 0))],
        out_shape=[jax.ShapeDtypeStruct((n, d), BF16), jax.ShapeDtypeStruct((n, ROUTE_LANES), F32),
                   jax.ShapeDtypeStruct((8, n), F32), jax.ShapeDtypeStruct((nt, 1, ROUTE_LANES), jnp.int32)],
        compiler_params=_params("arbitrary"),
        name="moe_router",
    )(x, g, rw, rb, tri)


def _moe_routed_kernel(cnt_ref, off_ref, g_lo_ref, g_hi_ref, s_lo_ref, s_hi_ref,
                       hn_ref, col_ref, row_ref, wg_ref, wu_ref, wd_ref, x_ref,
                       y_ref, xs_ref, ys_ref, gs_ref, *, mc, sb):
    i = pl.program_id(0)
    e = pl.program_id(1)
    ne = pl.num_programs(1)
    tm = hn_ref.shape[0]
    step = i * ne + e

    @pl.when(e == 0)
    def _():
        y_ref[...] = x_ref[...]
        ys_ref[...] = jnp.zeros_like(ys_ref)

    def gather(blk, carry):
        s1, s2, w1, w2 = (row_ref[r:r + 1, :] for r in range(4))
        r0 = pl.multiple_of(blk * sb, sb)
        sid = (lax.broadcasted_iota(jnp.int32, (sb, tm), 0) + r0).astype(F32)
        h1 = s1 == sid
        h2 = s2 == sid
        onehot = (jnp.where(h1, 1.0, 0.0) + jnp.where(h2, 1.0, 0.0)).astype(BF16)
        xs_ref[pl.ds(r0, sb), :] = _dot(onehot, hn_ref[...]).astype(BF16)
        gs_ref[pl.ds(r0, sb), :] = jnp.sum(
            jnp.where(h1, w1, 0.0) + jnp.where(h2, w2, 0.0), axis=-1, keepdims=True)
        return carry

    lax.fori_loop(g_lo_ref[step], g_hi_ref[step], gather, 0)

    first = off_ref[step]

    def chunk(k, carry):
        r0 = pl.multiple_of(first + k * mc, SLOT_ALIGN)
        xg = xs_ref[pl.ds(r0, mc), :]
        a = _dot(xg, wg_ref[...])
        b = _dot(xg, wu_ref[...])
        hid = (a * jax.nn.sigmoid(a)) * b * gs_ref[pl.ds(r0, mc), :]
        ys_ref[pl.ds(r0, mc), :] = _dot(hid.astype(BF16), wd_ref[...]).astype(BF16)
        return carry

    lax.fori_loop(0, (cnt_ref[step] + mc - 1) // mc, chunk, 0)

    def scatter(blk, carry):
        r0 = pl.multiple_of(blk * sb, sb)
        sid = (lax.broadcasted_iota(jnp.int32, (tm, sb), 1) + r0).astype(F32)
        onehot = (jnp.where(col_ref[:, 0:1] == sid, 1.0, 0.0)
                  + jnp.where(col_ref[:, 1:2] == sid, 1.0, 0.0)).astype(BF16)
        y_ref[...] += _dot(onehot, ys_ref[pl.ds(r0, sb), :])
        return carry

    lax.fori_loop(s_lo_ref[step], s_hi_ref[step], scatter, 0)


def _round_up(a, b):
    return -(-a // b) * b


def _moe_block_plan(cnt, off, mc, sb):
    reach = off + (cnt + mc - 1) // mc * mc
    g_hi = (lax.cummax(reach, axis=1) + sb - 1) // sb
    g_lo = jnp.pad(g_hi, ((0, 0), (1, 0)))[:, :-1]
    total = off[:, -1:] + _round_up_array(cnt[:, -1:], SLOT_ALIGN)
    s_hi = jnp.concatenate([off[:, 1:] // sb, (total + sb - 1) // sb], axis=1)
    s_lo = jnp.pad(s_hi, ((0, 0), (1, 0)))[:, :-1]
    return [a.reshape(-1).astype(jnp.int32) for a in (g_lo, g_hi, s_lo, s_hi)]


def _round_up_array(a, b):
    return (a + b - 1) // b * b


def _moe_routed(cnt, off, hn, col, row, wg, wu, wd, x, tm):
    n, d = x.shape
    ne, _, de = wg.shape
    sb = 256
    mc = _round_up(2 * tm * 5 // (ne * 4), SLOT_ALIGN)
    n_slots = _round_up(2 * tm + (ne - 1) * SLOT_ALIGN + mc, sb)
    plan = _moe_block_plan(cnt, off, mc, sb)
    cnt, off = cnt.reshape(-1), off.reshape(-1)
    kern = functools.partial(_moe_routed_kernel, mc=mc, sb=sb)
    tile = lambda i, e, *_: (i, 0)
    expert = lambda i, e, *_: (e, 0, 0)
    return pl.pallas_call(
        kern,
        grid_spec=pltpu.PrefetchScalarGridSpec(
            num_scalar_prefetch=6,
            grid=(n // tm, ne),
            in_specs=[pl.BlockSpec((tm, d), tile), pl.BlockSpec((tm, ROUTE_LANES), tile),
                      pl.BlockSpec((8, tm), lambda i, e, *_: (0, i)),
                      pl.BlockSpec((None, d, de), expert), pl.BlockSpec((None, d, de), expert),
                      pl.BlockSpec((None, de, d), expert), pl.BlockSpec((tm, d), tile)],
            out_specs=pl.BlockSpec((tm, d), tile),
            scratch_shapes=[pltpu.VMEM((n_slots, d), BF16), pltpu.VMEM((n_slots, d), BF16),
                            pltpu.VMEM((n_slots, 1), F32)],
        ),
        out_shape=jax.ShapeDtypeStruct((n, d), F32),
        compiler_params=_params("arbitrary", "arbitrary"),
        name="moe_routed",
    )(cnt, off, *plan, hn, col, row, wg, wu, wd, x)


def _moe(x, g, rw, rb, wg, wu, wd, tm):
    ne = wg.shape[0]
    tri = jnp.tril(jnp.ones((tm, tm), BF16), -1)
    hn, col, row, meta = _router(x, g, rw, rb, tri, tm)
    return _moe_routed(meta[:, 0, :ne], meta[:, 0, ne:2 * ne], hn, col, row, wg, wu, wd, x, tm)


def _final_norm_kernel(x_ref, g_ref, y_ref):
    y_ref[...] = _rms(x_ref[...], g_ref[...])


def _final_norm(x, g, tm):
    n, d = x.shape
    return pl.pallas_call(
        _final_norm_kernel,
        grid=(n // tm,),
        in_specs=[pl.BlockSpec((tm, d), lambda i: (i, 0)), _full((1, d))],
        out_specs=pl.BlockSpec((tm, d), lambda i: (i, 0)),
        out_shape=jax.ShapeDtypeStruct((n, d), F32),
        compiler_params=_params("parallel"),
        name="final_norm",
    )(x, g)


def _tile(n, pref):
    t = min(n, pref)
    assert n % t == 0, (n, t)
    return t


def kernel(x_prompt, x_sample, state_pool, cache_k, cache_v, page_table, norm_mix, norm_ffn, norm_final, pool_w_in, pool_w_grp, pool_scale, pool_w_out, attn_w_qkv, attn_w_out, gmlp_w_in, gmlp_ln_g, gmlp_ln_b, gmlp_w_s, gmlp_b_s, gmlp_w_out, moe_rg_w, moe_rg_b, moe_re_w, moe_re_b, moe_w_gate, moe_w_up, moe_w_down):
    batch, seq, d = x_prompt.shape
    dec_b, n_t, _ = x_sample.shape
    depth = norm_mix.shape[0]
    n_heads = N_HEADS
    hd = d // n_heads
    n_pages = page_table.shape[1]
    past_len = n_pages * PAGE_SIZE
    assert seq % MOBA_BLOCK == 0 and seq % GMLP_CHUNK == 0 and n_t <= GMLP_CHUNK
    assert past_len % GMLP_CHUNK == 0 and past_len >= POOL_STATE

    n_p, n_s = batch * seq, dec_b * n_t
    xp = x_prompt.reshape(n_p, d)
    xs = x_sample.transpose(1, 0, 2).reshape(n_s, d)
    tp = _tile(seq, 512)
    row = lambda a: a.reshape(1, -1)

    def to_batch_major(a):
        return a.reshape((n_t, dec_b) + a.shape[1:]).swapaxes(0, 1)

    pool_p, pool_s, k_p, v_p, k_s, v_s, gv_p, gv_s = [], [], [], [], [], [], [], []
    for i in range(depth):
        kind, j = i % N_MIXERS, i // N_MIXERS
        g_mix = row(norm_mix[i])
        if kind == 0:
            w_in, w_grp, w_out = (a.astype(BF16) for a in (pool_w_in[j], pool_w_grp[j], pool_w_out[j]))
            scale = row(pool_scale[j])
            xp, st = _pool_prompt(xp, g_mix, w_in, w_grp, scale, w_out, batch, seq, tp)
            pool_p.append(st)
            xs, u_s = _pool_sample(xs, state_pool[j].transpose(1, 0, 2), g_mix, w_in, w_grp, scale,
                                   w_out, n_t, past_len)
            pool_s.append(jnp.concatenate([state_pool[j][:, n_t:], to_batch_major(u_s)], axis=1))
        elif kind == 1:
            w_qkv, w_out = attn_w_qkv[j].astype(BF16), attn_w_out[j].astype(BF16)
            qp, kp, vp = _norm_matmul(xp, g_mix, w_qkv, 3, tp)
            qs, ks, vs = _norm_matmul(xs, g_mix, w_qkv, 3, _tile(n_s, 512))
            ap = _moba_prompt(qp, kp, vp, batch, seq, n_heads)
            a_s = _moba_sample(qs, ks, vs, cache_k, cache_v, j, page_table, n_t, n_heads)
            xp = _matmul_residual(ap, w_out, xp, tp)
            xs = _matmul_residual(a_s, w_out, xs, _tile(n_s, 512))
            k_p.append(kp.reshape(batch, seq, n_heads, hd))
            v_p.append(vp.reshape(batch, seq, n_heads, hd))
            k_s.append(to_batch_major(ks).reshape(dec_b, n_t, n_heads, hd))
            v_s.append(to_batch_major(vs).reshape(dec_b, n_t, n_heads, hd))
        else:
            w_in, w_out = gmlp_w_in[j].astype(BF16), gmlp_w_out[j].astype(BF16)
            ln_g, ln_b = row(gmlp_ln_g[j]), row(gmlp_ln_b[j])
            gc = w_out.shape[0] // GMLP_GROUPS
            bias = jnp.broadcast_to(gmlp_b_s[j].T[:, :, None], (GMLP_CHUNK, GMLP_GROUPS, gc))
            xp, vn_p = _gmlp_prompt(xp, g_mix, w_in, ln_g, ln_b, gmlp_w_s[j],
                                    bias.reshape(GMLP_CHUNK, GMLP_GROUPS * gc), w_out, tp)
            xs, vn_s = _gmlp_sample(xs, g_mix, w_in, ln_g, ln_b,
                                    gmlp_w_s[j][:, :n_t, :n_t].reshape(-1),
                                    gmlp_b_s[j][:, :n_t].reshape(-1), w_out, n_t)
            gv_p.append(vn_p.reshape(batch, seq, -1))
            gv_s.append(to_batch_major(vn_s))

        g_ffn = row(norm_ffn[i])
        ne = moe_re_w.shape[2]
        rw = jnp.zeros((d, ROUTE_LANES), F32).at[:, :ne].set(moe_re_w[i]).at[:, ne:ne + MOE_GROUPS].set(moe_rg_w[i])
        rb = jnp.zeros((1, ROUTE_LANES), F32).at[0, :ne].set(moe_re_b[i]).at[0, ne:ne + MOE_GROUPS].set(moe_rg_b[i])
        wg, wu, wd = (a.astype(BF16) for a in (moe_w_gate[i], moe_w_up[i], moe_w_down[i]))
        xp = _moe(xp, g_ffn, rw.astype(BF16), rb, wg, wu, wd, _tile(n_p, 1024))
        xs = _moe(xs, g_ffn, rw.astype(BF16), rb, wg, wu, wd, _tile(n_s, 512))

    g_fin = row(norm_final)
    y_p = _final_norm(xp, g_fin, tp).reshape(batch, seq, d)
    y_s = to_batch_major(_final_norm(xs, g_fin, _tile(n_s, 512)))
    return (y_p, y_s, jnp.stack(pool_p), jnp.stack(pool_s), jnp.stack(k_p), jnp.stack(v_p),
            jnp.stack(k_s), jnp.stack(v_s), jnp.stack(gv_p), jnp.stack(gv_s))
```

```python
import functools

import jax
import jax.numpy as jnp
from jax import lax
from jax.experimental import pallas as pl
from jax.experimental.pallas import tpu as pltpu

F32 = jnp.float32
BF16 = jnp.bfloat16

RMS_EPS = 1e-6
LN_EPS = 1e-5
N_MIXERS = 3
POOL_WINDOWS = (2, 4, 8, 16)
POOL_STATE = max(POOL_WINDOWS) - 1
MOBA_BLOCK = 256
MOBA_TOPK = 3
GMLP_CHUNK = 128
GMLP_GROUPS = 8
MOE_GROUPS = 4
MOE_PER_GROUP = 4
PAGE_SIZE = 128
N_HEADS = 8

V7X_VMEM_LIMIT_BYTES = 56 * 1024 * 1024
NEG_BIG = -1e30
LOG2_E = 1.4426950408889634


def _params(*sem):
    return pltpu.CompilerParams(dimension_semantics=sem, vmem_limit_bytes=V7X_VMEM_LIMIT_BYTES)


def _rms(x, g):
    return x * lax.rsqrt(jnp.mean(x * x, axis=-1, keepdims=True) + RMS_EPS) * g


def _dot(a, b):
    return jnp.dot(a, b, preferred_element_type=F32)


def _dot_nt(a, b, precision=None):
    return lax.dot_general(a, b, (((1,), (1,)), ((), ())), preferred_element_type=F32,
                           precision=precision)


def _full(shape):
    return pl.BlockSpec(shape, lambda *_: (0,) * len(shape))


def _norm_matmul_kernel(x_ref, g_ref, w_ref, *out_refs):
    h = _rms(x_ref[...], g_ref[...]).astype(BF16)
    f = w_ref.shape[1] // len(out_refs)
    for i, o_ref in enumerate(out_refs):
        o_ref[...] = _dot(h, w_ref[:, i * f:(i + 1) * f])


def _norm_matmul(x, g, w, n_out, tm):
    n, d = x.shape
    f = w.shape[1] // n_out
    return pl.pallas_call(
        _norm_matmul_kernel,
        grid=(n // tm,),
        in_specs=[pl.BlockSpec((tm, d), lambda i: (i, 0)), _full((1, d)), _full(w.shape)],
        out_specs=[pl.BlockSpec((tm, f), lambda i: (i, 0))] * n_out,
        out_shape=[jax.ShapeDtypeStruct((n, f), F32)] * n_out,
        compiler_params=_params("parallel"),
        name="norm_matmul",
    )(x, g, w)


def _matmul_residual_kernel(a_ref, w_ref, x_ref, y_ref):
    y_ref[...] = x_ref[...] + _dot(a_ref[...].astype(BF16), w_ref[...])


def _matmul_residual(a, w, x, tm):
    n, d = x.shape
    k = a.shape[1]
    return pl.pallas_call(
        _matmul_residual_kernel,
        grid=(n // tm,),
        in_specs=[pl.BlockSpec((tm, k), lambda i: (i, 0)), _full(w.shape),
                  pl.BlockSpec((tm, d), lambda i: (i, 0))],
        out_specs=pl.BlockSpec((tm, d), lambda i: (i, 0)),
        out_shape=jax.ShapeDtypeStruct((n, d), F32),
        compiler_params=_params("parallel"),
        name="matmul_residual",
    )(a, w, x)


def _pool_tail(r, x, wgrp_ref, scale_ref, wout_ref):
    gc = wgrp_ref.shape[1]
    rb = r.astype(BF16)
    t = jnp.concatenate([_dot(rb[:, g * gc:(g + 1) * gc], wgrp_ref[g])
                         for g in range(len(POOL_WINDOWS))], axis=-1)
    t = (t * scale_ref[...]).astype(BF16)
    return x + _dot(t, wout_ref[...])


def _pool_prompt_kernel(x_ref, g_ref, win_ref, wgrp_ref, scale_ref, wout_ref,
                        y_ref, st_ref, carry_ref, *, ts, ns):
    s = pl.program_id(1)
    halo = carry_ref.shape[0]
    gc = wgrp_ref.shape[1]

    @pl.when(s == 0)
    def _():
        carry_ref[...] = jnp.zeros_like(carry_ref)

    x = x_ref[...]
    h = _rms(x, g_ref[...]).astype(BF16)
    u = _dot(h, win_ref[...])
    ext = jnp.concatenate([carry_ref[...], u], axis=0)
    pos = s * ts + lax.broadcasted_iota(jnp.int32, (ts, 1), 0)
    parts = []
    acc = ext
    width = 1
    for w in POOL_WINDOWS:
        while width < w:
            acc = acc + pltpu.roll(acc, width, axis=0)
            width *= 2
        assert width == w
        cnt = jnp.minimum(pos + 1, w).astype(F32)
        parts.append(acc[halo:, :gc] / cnt)
        acc = acc[:, gc:]
    r = jnp.concatenate(parts, axis=-1) - u
    y_ref[...] = _pool_tail(r, x, wgrp_ref, scale_ref, wout_ref)
    carry_ref[...] = u[ts - halo:, :]

    @pl.when(s == ns - 1)
    def _():
        st_ref[...] = carry_ref[halo - POOL_STATE:, :]


def _pool_prompt(x, g, w_in, w_grp, scale, w_out, batch, seq, ts):
    n, d = x.shape
    ns = seq // ts
    halo = 16
    kern = functools.partial(_pool_prompt_kernel, ts=ts, ns=ns)
    return pl.pallas_call(
        kern,
        grid=(batch, ns),
        in_specs=[pl.BlockSpec((ts, d), lambda b, s: (b * ns + s, 0)), _full((1, d)),
                  _full(w_in.shape), _full(w_grp.shape), _full((1, d)), _full(w_out.shape)],
        out_specs=[pl.BlockSpec((ts, d), lambda b, s: (b * ns + s, 0)),
                   pl.BlockSpec((None, POOL_STATE, d), lambda b, s: (b, 0, 0))],
        out_shape=[jax.ShapeDtypeStruct((n, d), F32),
                   jax.ShapeDtypeStruct((batch, POOL_STATE, d), F32)],
        scratch_shapes=[pltpu.VMEM((halo, d), F32)],
        compiler_params=_params("parallel", "arbitrary"),
        name="pool_prompt",
    )(x, g, w_in, w_grp, scale, w_out)


def _pool_sample_kernel(x_ref, pref_ref, g_ref, win_ref, wgrp_ref, scale_ref, wout_ref,
                        y_ref, u_ref, *, n_t, pos0):
    bd = x_ref.shape[0] // n_t
    gc = wgrp_ref.shape[1]
    x = x_ref[...]
    h = _rms(x, g_ref[...]).astype(BF16)
    u = _dot(h, win_ref[...])
    u_ref[...] = u
    us = [u[t * bd:(t + 1) * bd] for t in range(n_t)]
    tail = [jnp.zeros_like(us[0])]
    for m in range(1, POOL_STATE + 1):
        tail.append(tail[-1] + pref_ref[POOL_STATE - m])
    rows = []
    for t in range(n_t):
        parts = []
        for gi, w in enumerate(POOL_WINDOWS):
            sl = slice(gi * gc, (gi + 1) * gc)
            n_new = min(w, t + 1)
            wsum = us[t][:, sl]
            for k in range(1, n_new):
                wsum = wsum + us[t - k][:, sl]
            if w - n_new:
                wsum = wsum + tail[w - n_new][:, sl]
            parts.append(wsum / float(min(pos0 + t + 1, w)))
        rows.append(jnp.concatenate(parts, axis=-1) - us[t])
    r = jnp.concatenate(rows, axis=0)
    y_ref[...] = _pool_tail(r, x, wgrp_ref, scale_ref, wout_ref)


def _pool_sample(x, pref_t, g, w_in, w_grp, scale, w_out, n_t, pos0):
    n, d = x.shape
    kern = functools.partial(_pool_sample_kernel, n_t=n_t, pos0=pos0)
    return pl.pallas_call(
        kern,
        grid=(1,),
        in_specs=[_full(x.shape), _full(pref_t.shape), _full((1, d)), _full(w_in.shape),
                  _full(w_grp.shape), _full((1, d)), _full(w_out.shape)],
        out_specs=[_full((n, d)), _full((n, d))],
        out_shape=[jax.ShapeDtypeStruct((n, d), F32)] * 2,
        compiler_params=_params("arbitrary"),
        name="pool_sample",
    )(x, pref_t, g, w_in, w_grp, scale, w_out)


def _top_blocks(gate, n_valid, n_blocks):
    idx = lax.broadcasted_iota(jnp.int32, gate.shape, 0)
    valid = idx < n_valid
    rows = []
    for j in range(n_blocks):
        gj = gate[j:j + 1, :]
        beats = valid & ((gate > gj) | ((gate == gj) & (idx < j)))
        rank = jnp.sum(beats.astype(F32), axis=0, keepdims=True)
        rows.append(jnp.where(rank < MOBA_TOPK, jnp.where(j < n_valid, 1.0, 0.0), 0.0))
    return rows


def _moba_prompt_kernel(q_ref, k_ref, v_ref, o_ref, kb_ref, vt_ref, kmean_ref, sel_ref, s_ref,
                        *, nb, group, hps, hd):
    qi = pl.program_id(2)
    blk = MOBA_BLOCK
    scale = hd ** -0.5
    heads = range(hps)
    lanes = lambda h: slice(h * hd, (h + 1) * hd)

    @pl.when(qi == 0)
    def _():
        k = k_ref[...]
        kb_ref[...] = k.astype(BF16)
        kmean_ref[...] = jnp.mean(k.reshape(nb, blk, hps * hd), axis=1)
        vt_ref[...] = v_ref[...].T.astype(BF16)

    q = q_ref[...]
    for h in heads:
        gate = _dot_nt(kmean_ref[:, lanes(h)], q[:, lanes(h)], precision=lax.Precision.HIGHEST)
        for j, row in enumerate(_top_blocks(gate, qi, nb)):
            sel_ref[h * nb + j:h * nb + j + 1, :] = row
    qb = [q[:, lanes(h)].astype(BF16) for h in heads]
    key_i = lax.broadcasted_iota(jnp.int32, (blk, blk), 0)
    qry_i = lax.broadcasted_iota(jnp.int32, (blk, blk), 1)
    scale2 = scale * LOG2_E

    def scores(h, j, n):
        kj = kb_ref[pl.ds(pl.multiple_of(j * blk, blk), n * blk), lanes(h)]
        return _dot_nt(kj, qb[h]) * scale2

    def pv(h, j, n, p):
        vj = vt_ref[lanes(h), pl.ds(pl.multiple_of(j * blk, blk), n * blk)]
        return _dot(vj, p.astype(BF16))

    n_trips = (qi + group - 1) // group
    own = nb * blk
    ms = []
    for h in heads:
        s = jnp.where(key_i <= qry_i, scores(h, qi, 1), NEG_BIG)
        s_ref[h, own:own + blk, :] = s
        ms.append(jnp.max(s, axis=0, keepdims=True))

    def score_pass(g, ms):
        j = g * group
        out = []
        for h, m in zip(heads, ms):
            keep = jnp.concatenate([jnp.broadcast_to(sel_ref[pl.ds(h * nb + j + r, 1), :], (blk, blk))
                                    for r in range(group)], axis=0)
            s = jnp.where(keep > 0.0, scores(h, j, group), NEG_BIG)
            s_ref[h, pl.ds(pl.multiple_of(j * blk, blk), group * blk), :] = s
            out.append(jnp.maximum(m, jnp.max(s, axis=0, keepdims=True)))
        return tuple(out)

    ms = lax.fori_loop(0, n_trips, score_pass, tuple(ms))

    carry = []
    for h in heads:
        p = jnp.exp2(s_ref[h, own:own + blk, :] - ms[h])
        carry.append((jnp.sum(p, axis=0, keepdims=True), pv(h, qi, 1, p)))

    def value_pass(g, carry):
        j = g * group
        out = []
        for h, (l, acc) in zip(heads, carry):
            p = jnp.exp2(s_ref[h, pl.ds(pl.multiple_of(j * blk, blk), group * blk), :] - ms[h])
            out.append((l + jnp.sum(p, axis=0, keepdims=True), acc + pv(h, j, group, p)))
        return tuple(out)

    carry = lax.fori_loop(0, n_trips, value_pass, tuple(carry))
    for h, (l, acc) in zip(heads, carry):
        o_ref[:, lanes(h)] = (acc / l).T.astype(o_ref.dtype)


def _moba_prompt(q, k, v, batch, seq, n_heads):
    n, d = q.shape
    hd = d // n_heads
    nb = seq // MOBA_BLOCK
    group = 4 if nb % 4 == 0 else 1
    hps = 2 if n_heads % 2 == 0 else 1
    kern = functools.partial(_moba_prompt_kernel, nb=nb, group=group, hps=hps, hd=hd)
    return pl.pallas_call(
        kern,
        grid=(batch, n_heads // hps, nb),
        in_specs=[pl.BlockSpec((MOBA_BLOCK, hps * hd), lambda b, h, i: (b * nb + i, h)),
                  pl.BlockSpec((seq, hps * hd), lambda b, h, i: (b, h)),
                  pl.BlockSpec((seq, hps * hd), lambda b, h, i: (b, h))],
        out_specs=pl.BlockSpec((MOBA_BLOCK, hps * hd), lambda b, h, i: (b * nb + i, h)),
        out_shape=jax.ShapeDtypeStruct((n, d), BF16),
        scratch_shapes=[pltpu.VMEM((seq, hps * hd), BF16), pltpu.VMEM((hps * hd, seq), BF16),
                        pltpu.VMEM((nb, hps * hd), F32), pltpu.VMEM((hps * nb, MOBA_BLOCK), F32),
                        pltpu.VMEM((hps, (nb + 1) * MOBA_BLOCK, MOBA_BLOCK), F32)],
        compiler_params=_params("parallel", "parallel", "arbitrary"),
        name="moba_prompt",
    )(q, k, v)


def _moba_sample_kernel(pt_ref, q_ref, kn_ref, vn_ref, *refs, n_t, n_pages, pps):
    del pt_ref
    kp_refs, vp_refs = refs[:pps], refs[pps:2 * pps]
    o_ref, wbd_ref, s_ref, vall_ref = refs[2 * pps:]
    p_i = pl.program_id(1)
    nh, hd = q_ref.shape[2], q_ref.shape[3]
    lanes = wbd_ref.shape[1]
    scale = hd ** -0.5
    n_past = n_pages // (MOBA_BLOCK // PAGE_SIZE)
    head_i = lax.broadcasted_iota(jnp.int32, (nh, hd), 0)

    @pl.when(p_i == 0)
    def _():
        rows = [jnp.concatenate([jnp.where(head_i == h, q_ref[t, 0], 0.0) for h in range(nh)], axis=-1)
                for t in range(n_t)]
        rows.append(jnp.zeros((lanes - n_t * nh, nh * hd), F32))
        wbd_ref[...] = jnp.concatenate(rows, axis=0).T.astype(BF16)

    for g, (kp_ref, vp_ref) in enumerate(zip(kp_refs, vp_refs)):
        off = pl.multiple_of((p_i * pps + g) * PAGE_SIZE, PAGE_SIZE)
        acc = None
        for h in range(nh):
            kh = kp_ref[pl.ds(h, PAGE_SIZE, stride=nh), :].astype(BF16)
            part = _dot(kh, wbd_ref[h * hd:(h + 1) * hd, :])
            acc = part if acc is None else acc + part
            vall_ref[pl.ds(off, PAGE_SIZE), h * hd:(h + 1) * hd] = (
                vp_ref[pl.ds(h, PAGE_SIZE, stride=nh), :].astype(BF16))
        s_ref[pl.ds(off, PAGE_SIZE), :] = acc

    @pl.when(p_i == n_pages // pps - 1)
    def _():
        s_all = s_ref[...]
        blocks = [s_all[j * MOBA_BLOCK:(j + 1) * MOBA_BLOCK] for j in range(n_past)]
        gate = [jnp.sum(b, axis=0, keepdims=True) * (1.0 / MOBA_BLOCK) for b in blocks]
        sel = []
        for j in range(n_past):
            rank = jnp.zeros_like(gate[j])
            for i in range(n_past):
                if i != j:
                    beats = (gate[i] >= gate[j]) if i < j else (gate[i] > gate[j])
                    rank = rank + beats.astype(F32)
            sel.append(rank < MOBA_TOPK)
        s_past = jnp.concatenate(
            [jnp.where(sel[j], blocks[j] * scale, NEG_BIG) for j in range(n_past)], axis=0)
        m_row = jnp.max(s_past, axis=0, keepdims=True)
        p_past = jnp.exp(s_past - m_row)
        l_row = jnp.sum(p_past, axis=0, keepdims=True)
        full = _dot(p_past.T.astype(BF16), vall_ref[...])
        stats = jnp.concatenate([m_row, l_row, jnp.zeros((lanes - 2, lanes), F32)], axis=0).T
        for t in range(n_t):
            rows = full[t * nh:(t + 1) * nh]
            out = functools.reduce(
                lambda a, b: a + b,
                [jnp.where(head_i == h, rows[:, h * hd:(h + 1) * hd], 0.0) for h in range(nh)])
            m_past = stats[t * nh:(t + 1) * nh, 0:1]
            l_past = stats[t * nh:(t + 1) * nh, 1:2]
            s_new = [jnp.sum(q_ref[t, 0] * kn_ref[t2, 0], axis=-1, keepdims=True) * scale
                     for t2 in range(t + 1)]
            m = functools.reduce(jnp.maximum, s_new, m_past)
            alpha = jnp.exp(m_past - m)
            out = alpha * out
            l = alpha * l_past
            for t2 in range(t + 1):
                p_new = jnp.exp(s_new[t2] - m)
                l = l + p_new
                out = out + p_new * vn_ref[t2, 0]
            o_ref[t, 0] = out / l


def _moba_sample(q, k, v, cache_k, cache_v, layer, page_table, n_t, n_heads):
    n, d = q.shape
    bd = n // n_t
    hd = d // n_heads
    n_pages = page_table.shape[1]
    past = n_pages * PAGE_SIZE
    lanes = 128
    assert n_pages % (MOBA_BLOCK // PAGE_SIZE) == 0 and n_t <= MOBA_BLOCK and n_t * n_heads <= lanes - 2
    q4, k4, v4 = (a.reshape(n_t, bd, n_heads, hd) for a in (q, k, v))
    n_layers, n_phys = cache_k.shape[:2]
    ck, cv = (c.reshape(n_layers, n_phys, PAGE_SIZE * n_heads, hd) for c in (cache_k, cache_v))
    new_spec = pl.BlockSpec((n_t, 1, n_heads, hd), lambda b, p, pt: (0, b, 0, 0))
    pps = 8 if n_pages % 8 == 0 else 2
    page_specs = [pl.BlockSpec((None, None, PAGE_SIZE * n_heads, hd),
                               functools.partial(lambda g, b, p, pt: (layer, pt[b, p * pps + g], 0, 0), g))
                  for g in range(pps)]
    kern = functools.partial(_moba_sample_kernel, n_t=n_t, n_pages=n_pages, pps=pps)
    out = pl.pallas_call(
        kern,
        grid_spec=pltpu.PrefetchScalarGridSpec(
            num_scalar_prefetch=1,
            grid=(bd, n_pages // pps),
            in_specs=[new_spec, new_spec, new_spec] + page_specs + page_specs,
            out_specs=new_spec,
            scratch_shapes=[pltpu.VMEM((d, lanes), BF16), pltpu.VMEM((past, lanes), F32),
                            pltpu.VMEM((past, d), BF16)],
        ),
        out_shape=jax.ShapeDtypeStruct((n_t, bd, n_heads, hd), F32),
        compiler_params=_params("parallel", "arbitrary"),
        name="moba_sample",
    )(page_table, q4, k4, v4, *([ck] * pps), *([cv] * pps))
    return out.reshape(n, d)


def _gelu(z):
    return 0.5 * z * (1.0 + jnp.tanh(0.7978845608028654 * (z + 0.044715 * z * z * z)))


def _gmlp_front(x, g_ref, win_ref, lng_ref, lnb_ref):
    h = _rms(x, g_ref[...]).astype(BF16)
    z = _gelu(_dot(h, win_ref[...]))
    dg = z.shape[1] // 2
    u, v = z[:, :dg], z[:, dg:]
    mu = jnp.mean(v, axis=-1, keepdims=True)
    var = jnp.mean(jnp.square(v - mu), axis=-1, keepdims=True)
    vn = (v - mu) * lax.rsqrt(var + LN_EPS) * lng_ref[...] + lnb_ref[...]
    return u, vn


def _gmlp_prompt_kernel(x_ref, g_ref, win_ref, lng_ref, lnb_ref, ws_ref, bias_ref, wout_ref,
                        y_ref, vn_ref, *, tm):
    x = x_ref[...]
    u, vn = _gmlp_front(x, g_ref, win_ref, lng_ref, lnb_ref)
    vn_ref[...] = vn
    ck = GMLP_CHUNK
    gc = vn.shape[1] // GMLP_GROUPS
    row_i = lax.broadcasted_iota(jnp.int32, (ck, ck), 0)
    col_i = lax.broadcasted_iota(jnp.int32, (ck, ck), 1)
    wc = [jnp.where(row_i >= col_i, ws_ref[g], 0.0).astype(BF16) for g in range(GMLP_GROUPS)]
    vb = vn.astype(BF16)
    mix = jnp.concatenate(
        [jnp.concatenate([_dot(wc[g], vb[c * ck:(c + 1) * ck, g * gc:(g + 1) * gc])
                          for g in range(GMLP_GROUPS)], axis=-1) + bias_ref[...]
         for c in range(tm // ck)], axis=0)
    y_ref[...] = x + _dot((u * mix).astype(BF16), wout_ref[...])


def _gmlp_prompt(x, g, w_in, ln_g, ln_b, w_s, bias, w_out, tm):
    n, d = x.shape
    dg = w_out.shape[0]
    kern = functools.partial(_gmlp_prompt_kernel, tm=tm)
    return pl.pallas_call(
        kern,
        grid=(n // tm,),
        in_specs=[pl.BlockSpec((tm, d), lambda i: (i, 0)), _full((1, d)), _full(w_in.shape),
                  _full((1, dg)), _full((1, dg)), _full(w_s.shape), _full(bias.shape),
                  _full(w_out.shape)],
        out_specs=[pl.BlockSpec((tm, d), lambda i: (i, 0)), pl.BlockSpec((tm, dg), lambda i: (i, 0))],
        out_shape=[jax.ShapeDtypeStruct((n, d), F32), jax.ShapeDtypeStruct((n, dg), F32)],
        compiler_params=_params("parallel"),
        name="gmlp_prompt",
    )(x, g, w_in, ln_g, ln_b, w_s, bias, w_out)


def _gmlp_sample_kernel(ws_ref, bs_ref, x_ref, g_ref, win_ref, lng_ref, lnb_ref, wout_ref,
                        y_ref, vn_ref, *, n_t):
    x = x_ref[...]
    u, vn = _gmlp_front(x, g_ref, win_ref, lng_ref, lnb_ref)
    vn_ref[...] = vn
    bd = x.shape[0] // n_t
    gc = vn.shape[1] // GMLP_GROUPS
    rows = []
    for t in range(n_t):
        parts = []
        for g in range(GMLP_GROUPS):
            sl = slice(g * gc, (g + 1) * gc)
            acc = jnp.full((bd, gc), bs_ref[g * n_t + t], F32)
            for s in range(t + 1):
                acc = acc + ws_ref[(g * n_t + t) * n_t + s] * vn[s * bd:(s + 1) * bd, sl]
            parts.append(acc)
        rows.append(jnp.concatenate(parts, axis=-1))
    mix = jnp.concatenate(rows, axis=0)
    y_ref[...] = x + _dot((u * mix).astype(BF16), wout_ref[...])


def _gmlp_sample(x, g, w_in, ln_g, ln_b, ws_flat, bs_flat, w_out, n_t):
    n, d = x.shape
    dg = w_out.shape[0]
    kern = functools.partial(_gmlp_sample_kernel, n_t=n_t)
    smem = pl.BlockSpec(memory_space=pltpu.SMEM)
    return pl.pallas_call(
        kern,
        grid=(1,),
        in_specs=[smem, smem, _full(x.shape), _full((1, d)), _full(w_in.shape), _full((1, dg)),
                  _full((1, dg)), _full(w_out.shape)],
        out_specs=[_full((n, d)), _full((n, dg))],
        out_shape=[jax.ShapeDtypeStruct((n, d), F32), jax.ShapeDtypeStruct((n, dg), F32)],
        compiler_params=_params("arbitrary"),
        name="gmlp_sample",
    )(ws_flat, bs_flat, x, g, w_in, ln_g, ln_b, w_out)


ROUTE_LANES = 128
SLOT_ALIGN = 16


def _router_kernel(x_ref, g_ref, rw_ref, rb_ref, tri_ref, hn_ref, col_ref, row_ref, meta_ref):
    ne = MOE_GROUPS * MOE_PER_GROUP
    hb = _rms(x_ref[...], g_ref[...]).astype(BF16)
    hn_ref[...] = hb
    logits = _dot(hb, rw_ref[...]) + rb_ref[...]
    lane = lax.broadcasted_iota(jnp.int32, logits.shape, 1)
    lg = jnp.where((lane >= ne) & (lane < ne + MOE_GROUPS), logits, -jnp.inf)
    g_max = jnp.max(lg, axis=-1, keepdims=True)
    g_idx = jnp.min(jnp.where(lg == g_max, lane, ROUTE_LANES), axis=-1, keepdims=True) - ne
    g_p = 1.0 / jnp.sum(jnp.exp(lg - g_max), axis=-1, keepdims=True)
    first = g_idx * MOE_PER_GROUP
    l1 = jnp.where((lane >= first) & (lane < first + MOE_PER_GROUP), logits, -jnp.inf)
    m1 = jnp.max(l1, axis=-1, keepdims=True)
    e1 = jnp.min(jnp.where(l1 == m1, lane, ROUTE_LANES), axis=-1, keepdims=True)
    l2 = jnp.where(lane == e1, -jnp.inf, l1)
    m2 = jnp.max(l2, axis=-1, keepdims=True)
    e2 = jnp.min(jnp.where(l2 == m2, lane, ROUTE_LANES), axis=-1, keepdims=True)
    z = jnp.sum(jnp.exp(l1 - m1), axis=-1, keepdims=True)
    p1 = 1.0 / z
    p2 = jnp.exp(m2 - m1) / z
    w1 = g_p * p1 / (p1 + p2)
    w2 = g_p * p2 / (p1 + p2)
    hit1 = lane == e1
    hit2 = lane == e2
    routed = jnp.where(hit1, 1.0, 0.0) + jnp.where(hit2, 1.0, 0.0)
    rank = _dot(tri_ref[...], routed.astype(BF16))
    cnt = jnp.sum(routed, axis=0, keepdims=True)
    pad = jnp.floor((cnt + (SLOT_ALIGN - 1)) * (1.0 / SLOT_ALIGN)) * SLOT_ALIGN
    lane_r = lax.broadcasted_iota(jnp.int32, cnt.shape, 1)
    incl = pad
    shift = 1
    while shift < ne:
        incl = incl + jnp.where(lane_r >= shift, pltpu.roll(incl, shift, axis=1), 0.0)
        shift *= 2
    off = incl - pad
    slot = rank + off
    s1 = jnp.sum(jnp.where(hit1, slot, 0.0), axis=-1, keepdims=True)
    s2 = jnp.sum(jnp.where(hit2, slot, 0.0), axis=-1, keepdims=True)
    col = jnp.where(lane == 0, s1, jnp.where(lane == 1, s2, jnp.where(lane == 2, w1,
                                                                     jnp.where(lane == 3, w2, 0.0))))
    col_ref[...] = col
    row_ref[...] = col.T[:row_ref.shape[0]]
    meta = jnp.where(lane_r < ne, cnt, jnp.where(lane_r < 2 * ne, pltpu.roll(off, ne, axis=1), 0.0))
    meta_ref[...] = meta.astype(jnp.int32)


def _router(x, g, rw, rb, tri, tm):
    n, d = x.shape
    nt = n // tm
    return pl.pallas_call(
        _router_kernel,
        grid=(nt,),
        in_specs=[pl.BlockSpec((tm, d), lambda i: (i, 0)), _full((1, d)), _full(rw.shape),
                  _full(rb.shape), _full(tri.shape)],
        out_specs=[pl.BlockSpec((tm, d), lambda i: (i, 0)),
                   pl.BlockSpec((tm, ROUTE_LANES), lambda i: (i, 0)),
                   pl.BlockSpec((8, tm), lambda i: (0, i)),
                   pl.BlockSpec((None, 1, ROUTE_LANES), lambda i: (i, 0, 0))],
        out_shape=[jax.ShapeDtypeStruct((n, d), BF16), jax.ShapeDtypeStruct((n, ROUTE_LANES), F32),
                   jax.ShapeDtypeStruct((8, n), F32), jax.ShapeDtypeStruct((nt, 1, ROUTE_LANES), jnp.int32)],
        compiler_params=_params("arbitrary"),
        name="moe_router",
    )(x, g, rw, rb, tri)


def _moe_routed_kernel(cnt_ref, off_ref, g_lo_ref, g_hi_ref, s_lo_ref, s_hi_ref,
                       hn_ref, col_ref, row_ref, wg_ref, wu_ref, wd_ref, x_ref,
                       y_ref, xs_ref, ys_ref, gs_ref, *, mc, sb):
    i = pl.program_id(0)
    eg = pl.program_id(1)
    eps = wg_ref.shape[0]
    ne = pl.num_programs(1) * eps
    tm = hn_ref.shape[0]

    @pl.when(eg == 0)
    def _():
        y_ref[...] = x_ref[...]
        ys_ref[...] = jnp.zeros_like(ys_ref)

    def gather(blk, carry):
        s1, s2, w1, w2 = (row_ref[r:r + 1, :] for r in range(4))
        r0 = pl.multiple_of(blk * sb, sb)
        sid = (lax.broadcasted_iota(jnp.int32, (sb, tm), 0) + r0).astype(F32)
        h1 = s1 == sid
        h2 = s2 == sid
        onehot = (jnp.where(h1, 1.0, 0.0) + jnp.where(h2, 1.0, 0.0)).astype(BF16)
        xs_ref[pl.ds(r0, sb), :] = _dot(onehot, hn_ref[...]).astype(BF16)
        gs_ref[pl.ds(r0, sb), :] = jnp.sum(
            jnp.where(h1, w1, 0.0) + jnp.where(h2, w2, 0.0), axis=-1, keepdims=True)
        return carry

    step0 = i * ne + eg * eps
    step1 = step0 + eps - 1
    lax.fori_loop(g_lo_ref[step0], g_hi_ref[step1], gather, 0)

    for r in range(eps):
        first = off_ref[step0 + r]

        def chunk(k, carry, first=first, r=r):
            r0 = pl.multiple_of(first + k * mc, SLOT_ALIGN)
            xg = xs_ref[pl.ds(r0, mc), :]
            a = _dot(xg, wg_ref[r])
            b = _dot(xg, wu_ref[r])
            hid = (a * jax.nn.sigmoid(a)) * b * gs_ref[pl.ds(r0, mc), :]
            ys_ref[pl.ds(r0, mc), :] = _dot(hid.astype(BF16), wd_ref[r]).astype(BF16)
            return carry

        lax.fori_loop(0, (cnt_ref[step0 + r] + mc - 1) // mc, chunk, 0)

    def scatter(blk, carry):
        r0 = pl.multiple_of(blk * sb, sb)
        sid = (lax.broadcasted_iota(jnp.int32, (tm, sb), 1) + r0).astype(F32)
        onehot = (jnp.where(col_ref[:, 0:1] == sid, 1.0, 0.0)
                  + jnp.where(col_ref[:, 1:2] == sid, 1.0, 0.0)).astype(BF16)
        y_ref[...] += _dot(onehot, ys_ref[pl.ds(r0, sb), :])
        return carry

    lax.fori_loop(s_lo_ref[step0], s_hi_ref[step1], scatter, 0)


def _round_up(a, b):
    return -(-a // b) * b


def _moe_block_plan(cnt, off, mc, sb):
    reach = off + (cnt + mc - 1) // mc * mc
    g_hi = (lax.cummax(reach, axis=1) + sb - 1) // sb
    g_lo = jnp.pad(g_hi, ((0, 0), (1, 0)))[:, :-1]
    total = off[:, -1:] + _round_up_array(cnt[:, -1:], SLOT_ALIGN)
    s_hi = jnp.concatenate([off[:, 1:] // sb, (total + sb - 1) // sb], axis=1)
    s_lo = jnp.pad(s_hi, ((0, 0), (1, 0)))[:, :-1]
    return [a.reshape(-1).astype(jnp.int32) for a in (g_lo, g_hi, s_lo, s_hi)]


def _round_up_array(a, b):
    return (a + b - 1) // b * b


def _moe_routed(cnt, off, hn, col, row, wg, wu, wd, x, tm):
    n, d = x.shape
    ne, _, de = wg.shape
    sb = 256
    mc = _round_up(2 * tm * 5 // (ne * 4), SLOT_ALIGN)
    n_slots = _round_up(2 * tm + (ne - 1) * SLOT_ALIGN + mc, sb)
    plan = _moe_block_plan(cnt, off, mc, sb)
    cnt, off = cnt.reshape(-1), off.reshape(-1)
    eps = 2 if ne % 2 == 0 else 1
    kern = functools.partial(_moe_routed_kernel, mc=mc, sb=sb)
    tile = lambda i, e, *_: (i, 0)
    expert = lambda i, e, *_: (e, 0, 0)
    return pl.pallas_call(
        kern,
        grid_spec=pltpu.PrefetchScalarGridSpec(
            num_scalar_prefetch=6,
            grid=(n // tm, ne // eps),
            in_specs=[pl.BlockSpec((tm, d), tile), pl.BlockSpec((tm, ROUTE_LANES), tile),
                      pl.BlockSpec((8, tm), lambda i, e, *_: (0, i)),
                      pl.BlockSpec((eps, d, de), expert), pl.BlockSpec((eps, d, de), expert),
                      pl.BlockSpec((eps, de, d), expert), pl.BlockSpec((tm, d), tile)],
            out_specs=pl.BlockSpec((tm, d), tile),
            scratch_shapes=[pltpu.VMEM((n_slots, d), BF16), pltpu.VMEM((n_slots, d), BF16),
                            pltpu.VMEM((n_slots, 1), F32)],
        ),
        out_shape=jax.ShapeDtypeStruct((n, d), F32),
        compiler_params=_params("arbitrary", "arbitrary"),
        name="moe_routed",
    )(cnt, off, *plan, hn, col, row, wg, wu, wd, x)


def _moe(x, g, rw, rb, wg, wu, wd, tm):
    ne = wg.shape[0]
    tri = jnp.tril(jnp.ones((tm, tm), BF16), -1)
    hn, col, row, meta = _router(x, g, rw, rb, tri, tm)
    return _moe_routed(meta[:, 0, :ne], meta[:, 0, ne:2 * ne], hn, col, row, wg, wu, wd, x, tm)


def _final_norm_kernel(x_ref, g_ref, y_ref):
    y_ref[...] = _rms(x_ref[...], g_ref[...])


def _final_norm(x, g, tm):
    n, d = x.shape
    return pl.pallas_call(
        _final_norm_kernel,
        grid=(n // tm,),
        in_specs=[pl.BlockSpec((tm, d), lambda i: (i, 0)), _full((1, d))],
        out_specs=pl.BlockSpec((tm, d), lambda i: (i, 0)),
        out_shape=jax.ShapeDtypeStruct((n, d), F32),
        compiler_params=_params("parallel"),
        name="final_norm",
    )(x, g)


def _tile(n, pref):
    t = min(n, pref)
    assert n % t == 0, (n, t)
    return t


def kernel(x_prompt, x_sample, state_pool, cache_k, cache_v, page_table, norm_mix, norm_ffn, norm_final, pool_w_in, pool_w_grp, pool_scale, pool_w_out, attn_w_qkv, attn_w_out, gmlp_w_in, gmlp_ln_g, gmlp_ln_b, gmlp_w_s, gmlp_b_s, gmlp_w_out, moe_rg_w, moe_rg_b, moe_re_w, moe_re_b, moe_w_gate, moe_w_up, moe_w_down):
    batch, seq, d = x_prompt.shape
    dec_b, n_t, _ = x_sample.shape
    depth = norm_mix.shape[0]
    n_heads = N_HEADS
    hd = d // n_heads
    n_pages = page_table.shape[1]
    past_len = n_pages * PAGE_SIZE
    assert seq % MOBA_BLOCK == 0 and seq % GMLP_CHUNK == 0 and n_t <= GMLP_CHUNK
    assert past_len % GMLP_CHUNK == 0 and past_len >= POOL_STATE

    n_p, n_s = batch * seq, dec_b * n_t
    xp = x_prompt.reshape(n_p, d)
    xs = x_sample.transpose(1, 0, 2).reshape(n_s, d)
    tp = _tile(seq, 512)
    row = lambda a: a.reshape(1, -1)

    def to_batch_major(a):
        return a.reshape((n_t, dec_b) + a.shape[1:]).swapaxes(0, 1)

    pool_p, pool_s, k_p, v_p, k_s, v_s, gv_p, gv_s = [], [], [], [], [], [], [], []
    for i in range(depth):
        kind, j = i % N_MIXERS, i // N_MIXERS
        g_mix = row(norm_mix[i])
        if kind == 0:
            w_in, w_grp, w_out = (a.astype(BF16) for a in (pool_w_in[j], pool_w_grp[j], pool_w_out[j]))
            scale = row(pool_scale[j])
            xp, st = _pool_prompt(xp, g_mix, w_in, w_grp, scale, w_out, batch, seq, tp)
            pool_p.append(st)
            xs, u_s = _pool_sample(xs, state_pool[j].transpose(1, 0, 2), g_mix, w_in, w_grp, scale,
                                   w_out, n_t, past_len)
            pool_s.append(jnp.concatenate([state_pool[j][:, n_t:], to_batch_major(u_s)], axis=1))
        elif kind == 1:
            w_qkv, w_out = attn_w_qkv[j].astype(BF16), attn_w_out[j].astype(BF16)
            qp, kp, vp = _norm_matmul(xp, g_mix, w_qkv, 3, tp)
            qs, ks, vs = _norm_matmul(xs, g_mix, w_qkv, 3, _tile(n_s, 512))
            ap = _moba_prompt(qp, kp, vp, batch, seq, n_heads)
            a_s = _moba_sample(qs, ks, vs, cache_k, cache_v, j, page_table, n_t, n_heads)
            xp = _matmul_residual(ap, w_out, xp, tp)
            xs = _matmul_residual(a_s, w_out, xs, _tile(n_s, 512))
            k_p.append(kp.reshape(batch, seq, n_heads, hd))
            v_p.append(vp.reshape(batch, seq, n_heads, hd))
            k_s.append(to_batch_major(ks).reshape(dec_b, n_t, n_heads, hd))
            v_s.append(to_batch_major(vs).reshape(dec_b, n_t, n_heads, hd))
        else:
            w_in, w_out = gmlp_w_in[j].astype(BF16), gmlp_w_out[j].astype(BF16)
            ln_g, ln_b = row(gmlp_ln_g[j]), row(gmlp_ln_b[j])
            gc = w_out.shape[0] // GMLP_GROUPS
            bias = jnp.broadcast_to(gmlp_b_s[j].T[:, :, None], (GMLP_CHUNK, GMLP_GROUPS, gc))
            xp, vn_p = _gmlp_prompt(xp, g_mix, w_in, ln_g, ln_b, gmlp_w_s[j],
                                    bias.reshape(GMLP_CHUNK, GMLP_GROUPS * gc), w_out, tp)
            xs, vn_s = _gmlp_sample(xs, g_mix, w_in, ln_g, ln_b,
                                    gmlp_w_s[j][:, :n_t, :n_t].reshape(-1),
                                    gmlp_b_s[j][:, :n_t].reshape(-1), w_out, n_t)
            gv_p.append(vn_p.reshape(batch, seq, -1))
            gv_s.append(to_batch_major(vn_s))

        g_ffn = row(norm_ffn[i])
        ne = moe_re_w.shape[2]
        rw = jnp.zeros((d, ROUTE_LANES), F32).at[:, :ne].set(moe_re_w[i]).at[:, ne:ne + MOE_GROUPS].set(moe_rg_w[i])
        rb = jnp.zeros((1, ROUTE_LANES), F32).at[0, :ne].set(moe_re_b[i]).at[0, ne:ne + MOE_GROUPS].set(moe_rg_b[i])
        wg, wu, wd = (a.astype(BF16) for a in (moe_w_gate[i], moe_w_up[i], moe_w_down[i]))
        xp = _moe(xp, g_ffn, rw.astype(BF16), rb, wg, wu, wd, _tile(n_p, 1024))
        xs = _moe(xs, g_ffn, rw.astype(BF16), rb, wg, wu, wd, _tile(n_s, 512))

    g_fin = row(norm_final)
    y_p = _final_norm(xp, g_fin, tp).reshape(batch, seq, d)
    y_s = to_batch_major(_final_norm(xs, g_fin, _tile(n_s, 512)))
    return (y_p, y_s, jnp.stack(pool_p), jnp.stack(pool_s), jnp.stack(k_p), jnp.stack(v_p),
            jnp.stack(k_s), jnp.stack(v_s), jnp.stack(gv_p), jnp.stack(gv_s))
```

```python
import functools

import jax
import jax.numpy as jnp
from jax import lax
from jax.experimental import pallas as pl
from jax.experimental.pallas import tpu as pltpu

F32 = jnp.float32
BF16 = jnp.bfloat16

RMS_EPS = 1e-6
LN_EPS = 1e-5
N_MIXERS = 3
POOL_WINDOWS = (2, 4, 8, 16)
POOL_STATE = max(POOL_WINDOWS) - 1
MOBA_BLOCK = 256
MOBA_TOPK = 3
GMLP_CHUNK = 128
GMLP_GROUPS = 8
MOE_GROUPS = 4
MOE_PER_GROUP = 4
PAGE_SIZE = 128
N_HEADS = 8

V7X_VMEM_LIMIT_BYTES = 56 * 1024 * 1024
NEG_BIG = -1e30
LOG2_E = 1.4426950408889634


def _params(*sem):
    return pltpu.CompilerParams(dimension_semantics=sem, vmem_limit_bytes=V7X_VMEM_LIMIT_BYTES)


def _rms(x, g):
    return x * lax.rsqrt(jnp.mean(x * x, axis=-1, keepdims=True) + RMS_EPS) * g


def _dot(a, b):
    return jnp.dot(a, b, preferred_element_type=F32)


def _dot_nt(a, b, precision=None):
    return lax.dot_general(a, b, (((1,), (1,)), ((), ())), preferred_element_type=F32,
                           precision=precision)


def _full(shape):
    return pl.BlockSpec(shape, lambda *_: (0,) * len(shape))


def _norm_matmul_kernel(x_ref, g_ref, w_ref, *out_refs):
    h = _rms(x_ref[...], g_ref[...]).astype(BF16)
    f = w_ref.shape[1] // len(out_refs)
    for i, o_ref in enumerate(out_refs):
        o_ref[...] = _dot(h, w_ref[:, i * f:(i + 1) * f])


def _norm_matmul(x, g, w, n_out, tm):
    n, d = x.shape
    f = w.shape[1] // n_out
    return pl.pallas_call(
        _norm_matmul_kernel,
        grid=(n // tm,),
        in_specs=[pl.BlockSpec((tm, d), lambda i: (i, 0)), _full((1, d)), _full(w.shape)],
        out_specs=[pl.BlockSpec((tm, f), lambda i: (i, 0))] * n_out,
        out_shape=[jax.ShapeDtypeStruct((n, f), F32)] * n_out,
        compiler_params=_params("parallel"),
        name="norm_matmul",
    )(x, g, w)


def _matmul_residual_kernel(a_ref, w_ref, x_ref, y_ref):
    y_ref[...] = x_ref[...] + _dot(a_ref[...].astype(BF16), w_ref[...])


def _matmul_residual(a, w, x, tm):
    n, d = x.shape
    k = a.shape[1]
    return pl.pallas_call(
        _matmul_residual_kernel,
        grid=(n // tm,),
        in_specs=[pl.BlockSpec((tm, k), lambda i: (i, 0)), _full(w.shape),
                  pl.BlockSpec((tm, d), lambda i: (i, 0))],
        out_specs=pl.BlockSpec((tm, d), lambda i: (i, 0)),
        out_shape=jax.ShapeDtypeStruct((n, d), F32),
        compiler_params=_params("parallel"),
        name="matmul_residual",
    )(a, w, x)


def _pool_tail(r, x, wgrp_ref, scale_ref, wout_ref):
    gc = wgrp_ref.shape[1]
    rb = r.astype(BF16)
    t = jnp.concatenate([_dot(rb[:, g * gc:(g + 1) * gc], wgrp_ref[g])
                         for g in range(len(POOL_WINDOWS))], axis=-1)
    t = (t * scale_ref[...]).astype(BF16)
    return x + _dot(t, wout_ref[...])


def _pool_prompt_kernel(x_ref, g_ref, win_ref, wgrp_ref, scale_ref, wout_ref,
                        y_ref, st_ref, carry_ref, *, ts, ns):
    s = pl.program_id(1)
    halo = carry_ref.shape[0]
    gc = wgrp_ref.shape[1]

    @pl.when(s == 0)
    def _():
        carry_ref[...] = jnp.zeros_like(carry_ref)

    x = x_ref[...]
    h = _rms(x, g_ref[...]).astype(BF16)
    u = _dot(h, win_ref[...])
    ext = jnp.concatenate([carry_ref[...], u], axis=0)
    pos = s * ts + lax.broadcasted_iota(jnp.int32, (ts, 1), 0)
    parts = []
    acc = ext
    width = 1
    for w in POOL_WINDOWS:
        while width < w:
            acc = acc + pltpu.roll(acc, width, axis=0)
            width *= 2
        assert width == w
        cnt = jnp.minimum(pos + 1, w).astype(F32)
        parts.append(acc[halo:, :gc] / cnt)
        acc = acc[:, gc:]
    r = jnp.concatenate(parts, axis=-1) - u
    y_ref[...] = _pool_tail(r, x, wgrp_ref, scale_ref, wout_ref)
    carry_ref[...] = u[ts - halo:, :]

    @pl.when(s == ns - 1)
    def _():
        st_ref[...] = carry_ref[halo - POOL_STATE:, :]


def _pool_prompt(x, g, w_in, w_grp, scale, w_out, batch, seq, ts):
    n, d = x.shape
    ns = seq // ts
    halo = 16
    kern = functools.partial(_pool_prompt_kernel, ts=ts, ns=ns)
    return pl.pallas_call(
        kern,
        grid=(batch, ns),
        in_specs=[pl.BlockSpec((ts, d), lambda b, s: (b * ns + s, 0)), _full((1, d)),
                  _full(w_in.shape), _full(w_grp.shape), _full((1, d)), _full(w_out.shape)],
        out_specs=[pl.BlockSpec((ts, d), lambda b, s: (b * ns + s, 0)),
                   pl.BlockSpec((None, POOL_STATE, d), lambda b, s: (b, 0, 0))],
        out_shape=[jax.ShapeDtypeStruct((n, d), F32),
                   jax.ShapeDtypeStruct((batch, POOL_STATE, d), F32)],
        scratch_shapes=[pltpu.VMEM((halo, d), F32)],
        compiler_params=_params("parallel", "arbitrary"),
        name="pool_prompt",
    )(x, g, w_in, w_grp, scale, w_out)


def _pool_sample_kernel(x_ref, pref_ref, g_ref, win_ref, wgrp_ref, scale_ref, wout_ref,
                        y_ref, u_ref, *, n_t, pos0):
    bd = x_ref.shape[0] // n_t
    gc = wgrp_ref.shape[1]
    x = x_ref[...]
    h = _rms(x, g_ref[...]).astype(BF16)
    u = _dot(h, win_ref[...])
    u_ref[...] = u
    us = [u[t * bd:(t + 1) * bd] for t in range(n_t)]
    tail = [jnp.zeros_like(us[0])]
    for m in range(1, POOL_STATE + 1):
        tail.append(tail[-1] + pref_ref[POOL_STATE - m])
    rows = []
    for t in range(n_t):
        parts = []
        for gi, w in enumerate(POOL_WINDOWS):
            sl = slice(gi * gc, (gi + 1) * gc)
            n_new = min(w, t + 1)
            wsum = us[t][:, sl]
            for k in range(1, n_new):
                wsum = wsum + us[t - k][:, sl]
            if w - n_new:
                wsum = wsum + tail[w - n_new][:, sl]
            parts.append(wsum / float(min(pos0 + t + 1, w)))
        rows.append(jnp.concatenate(parts, axis=-1) - us[t])
    r = jnp.concatenate(rows, axis=0)
    y_ref[...] = _pool_tail(r, x, wgrp_ref, scale_ref, wout_ref)


def _pool_sample(x, pref_t, g, w_in, w_grp, scale, w_out, n_t, pos0):
    n, d = x.shape
    kern = functools.partial(_pool_sample_kernel, n_t=n_t, pos0=pos0)
    return pl.pallas_call(
        kern,
        grid=(1,),
        in_specs=[_full(x.shape), _full(pref_t.shape), _full((1, d)), _full(w_in.shape),
                  _full(w_grp.shape), _full((1, d)), _full(w_out.shape)],
        out_specs=[_full((n, d)), _full((n, d))],
        out_shape=[jax.ShapeDtypeStruct((n, d), F32)] * 2,
        compiler_params=_params("arbitrary"),
        name="pool_sample",
    )(x, pref_t, g, w_in, w_grp, scale, w_out)


def _top_blocks(gate, n_valid, n_blocks):
    idx = lax.broadcasted_iota(jnp.int32, gate.shape, 0)
    valid = idx < n_valid
    rows = []
    for j in range(n_blocks):
        gj = gate[j:j + 1, :]
        beats = valid & ((gate > gj) | ((gate == gj) & (idx < j)))
        rank = jnp.sum(beats.astype(F32), axis=0, keepdims=True)
        rows.append(jnp.where(rank < MOBA_TOPK, jnp.where(j < n_valid, 1.0, 0.0), 0.0))
    return rows


def _moba_prompt_kernel(q_ref, k_ref, v_ref, o_ref, kb_ref, vt_ref, kmean_ref, sel_ref, s_ref,
                        *, nb, group, hps, hd):
    qi = pl.program_id(2)
    blk = MOBA_BLOCK
    scale = hd ** -0.5
    heads = range(hps)
    lanes = lambda h: slice(h * hd, (h + 1) * hd)

    @pl.when(qi == 0)
    def _():
        k = k_ref[...]
        kb_ref[...] = k.astype(BF16)
        kmean_ref[...] = jnp.mean(k.reshape(nb, blk, hps * hd), axis=1)
        vt_ref[...] = v_ref[...].T.astype(BF16)

    q = q_ref[...]
    qb = [q[:, lanes(h)].astype(BF16) for h in heads]
    for h in heads:
        gate = _dot_nt(kmean_ref[:, lanes(h)].astype(BF16), qb[h])
        for j, row in enumerate(_top_blocks(gate, qi, nb)):
            sel_ref[h * nb + j:h * nb + j + 1, :] = row
    key_i = lax.broadcasted_iota(jnp.int32, (blk, blk), 0)
    qry_i = lax.broadcasted_iota(jnp.int32, (blk, blk), 1)
    scale2 = scale * LOG2_E

    def scores(h, j, n):
        kj = kb_ref[pl.ds(pl.multiple_of(j * blk, blk), n * blk), lanes(h)]
        return _dot_nt(kj, qb[h]) * scale2

    def pv(h, j, n, p):
        vj = vt_ref[lanes(h), pl.ds(pl.multiple_of(j * blk, blk), n * blk)]
        return _dot(vj, p.astype(BF16))

    n_trips = (qi + group - 1) // group
    own = nb * blk
    ms = []
    for h in heads:
        s = jnp.where(key_i <= qry_i, scores(h, qi, 1), NEG_BIG)
        s_ref[h, own:own + blk, :] = s
        ms.append(jnp.max(s, axis=0, keepdims=True))

    def score_pass(g, ms):
        j = g * group
        out = []
        for h, m in zip(heads, ms):
            keep = jnp.concatenate([jnp.broadcast_to(sel_ref[pl.ds(h * nb + j + r, 1), :], (blk, blk))
                                    for r in range(group)], axis=0)
            s = jnp.where(keep > 0.0, scores(h, j, group), NEG_BIG)
            s_ref[h, pl.ds(pl.multiple_of(j * blk, blk), group * blk), :] = s
            out.append(jnp.maximum(m, jnp.max(s, axis=0, keepdims=True)))
        return tuple(out)

    ms = lax.fori_loop(0, n_trips, score_pass, tuple(ms))

    carry = []
    for h in heads:
        p = jnp.exp2(s_ref[h, own:own + blk, :] - ms[h])
        carry.append((jnp.sum(p, axis=0, keepdims=True), pv(h, qi, 1, p)))

    def value_pass(g, carry):
        j = g * group
        out = []
        for h, (l, acc) in zip(heads, carry):
            p = jnp.exp2(s_ref[h, pl.ds(pl.multiple_of(j * blk, blk), group * blk), :] - ms[h])
            out.append((l + jnp.sum(p, axis=0, keepdims=True), acc + pv(h, j, group, p)))
        return tuple(out)

    carry = lax.fori_loop(0, n_trips, value_pass, tuple(carry))
    for h, (l, acc) in zip(heads, carry):
        o_ref[:, lanes(h)] = (acc / l).T.astype(o_ref.dtype)


def _moba_prompt(q, k, v, batch, seq, n_heads):
    n, d = q.shape
    hd = d // n_heads
    nb = seq // MOBA_BLOCK
    group = 4 if nb % 4 == 0 else 1
    hps = 2 if n_heads % 2 == 0 else 1
    kern = functools.partial(_moba_prompt_kernel, nb=nb, group=group, hps=hps, hd=hd)
    return pl.pallas_call(
        kern,
        grid=(batch, n_heads // hps, nb),
        in_specs=[pl.BlockSpec((MOBA_BLOCK, hps * hd), lambda b, h, i: (b * nb + i, h)),
                  pl.BlockSpec((seq, hps * hd), lambda b, h, i: (b, h)),
                  pl.BlockSpec((seq, hps * hd), lambda b, h, i: (b, h))],
        out_specs=pl.BlockSpec((MOBA_BLOCK, hps * hd), lambda b, h, i: (b * nb + i, h)),
        out_shape=jax.ShapeDtypeStruct((n, d), BF16),
        scratch_shapes=[pltpu.VMEM((seq, hps * hd), BF16), pltpu.VMEM((hps * hd, seq), BF16),
                        pltpu.VMEM((nb, hps * hd), F32), pltpu.VMEM((hps * nb, MOBA_BLOCK), F32),
                        pltpu.VMEM((hps, (nb + 1) * MOBA_BLOCK, MOBA_BLOCK), F32)],
        compiler_params=_params("parallel", "parallel", "arbitrary"),
        name="moba_prompt",
    )(q, k, v)


def _moba_sample_kernel(pt_ref, q_ref, kn_ref, vn_ref, *refs, n_t, n_pages, pps):
    del pt_ref
    kp_refs, vp_refs = refs[:pps], refs[pps:2 * pps]
    o_ref, wbd_ref, s_ref, vall_ref = refs[2 * pps:]
    p_i = pl.program_id(1)
    nh, hd = q_ref.shape[2], q_ref.shape[3]
    lanes = wbd_ref.shape[1]
    scale = hd ** -0.5
    n_past = n_pages // (MOBA_BLOCK // PAGE_SIZE)
    head_i = lax.broadcasted_iota(jnp.int32, (nh, hd), 0)

    @pl.when(p_i == 0)
    def _():
        rows = [jnp.concatenate([jnp.where(head_i == h, q_ref[t, 0], 0.0) for h in range(nh)], axis=-1)
                for t in range(n_t)]
        rows.append(jnp.zeros((lanes - n_t * nh, nh * hd), F32))
        wbd_ref[...] = jnp.concatenate(rows, axis=0).T.astype(BF16)

    for g, (kp_ref, vp_ref) in enumerate(zip(kp_refs, vp_refs)):
        off = pl.multiple_of((p_i * pps + g) * PAGE_SIZE, PAGE_SIZE)
        acc = None
        for h in range(nh):
            kh = kp_ref[pl.ds(h, PAGE_SIZE, stride=nh), :].astype(BF16)
            part = _dot(kh, wbd_ref[h * hd:(h + 1) * hd, :])
            acc = part if acc is None else acc + part
            vall_ref[pl.ds(off, PAGE_SIZE), h * hd:(h + 1) * hd] = (
                vp_ref[pl.ds(h, PAGE_SIZE, stride=nh), :].astype(BF16))
        s_ref[pl.ds(off, PAGE_SIZE), :] = acc

    @pl.when(p_i == n_pages // pps - 1)
    def _():
        s_all = s_ref[...]
        blocks = [s_all[j * MOBA_BLOCK:(j + 1) * MOBA_BLOCK] for j in range(n_past)]
        gate = [jnp.sum(b, axis=0, keepdims=True) * (1.0 / MOBA_BLOCK) for b in blocks]
        sel = []
        for j in range(n_past):
            rank = jnp.zeros_like(gate[j])
            for i in range(n_past):
                if i != j:
                    beats = (gate[i] >= gate[j]) if i < j else (gate[i] > gate[j])
                    rank = rank + beats.astype(F32)
            sel.append(rank < MOBA_TOPK)
        s_past = jnp.concatenate(
            [jnp.where(sel[j], blocks[j] * scale, NEG_BIG) for j in range(n_past)], axis=0)
        m_row = jnp.max(s_past, axis=0, keepdims=True)
        p_past = jnp.exp(s_past - m_row)
        l_row = jnp.sum(p_past, axis=0, keepdims=True)
        full = _dot(p_past.T.astype(BF16), vall_ref[...])
        stats = jnp.concatenate([m_row, l_row, jnp.zeros((lanes - 2, lanes), F32)], axis=0).T
        for t in range(n_t):
            rows = full[t * nh:(t + 1) * nh]
            out = functools.reduce(
                lambda a, b: a + b,
                [jnp.where(head_i == h, rows[:, h * hd:(h + 1) * hd], 0.0) for h in range(nh)])
            m_past = stats[t * nh:(t + 1) * nh, 0:1]
            l_past = stats[t * nh:(t + 1) * nh, 1:2]
            s_new = [jnp.sum(q_ref[t, 0] * kn_ref[t2, 0], axis=-1, keepdims=True) * scale
                     for t2 in range(t + 1)]
            m = functools.reduce(jnp.maximum, s_new, m_past)
            alpha = jnp.exp(m_past - m)
            out = alpha * out
            l = alpha * l_past
            for t2 in range(t + 1):
                p_new = jnp.exp(s_new[t2] - m)
                l = l + p_new
                out = out + p_new * vn_ref[t2, 0]
            o_ref[t, 0] = out / l


def _moba_sample(q, k, v, cache_k, cache_v, layer, page_table, n_t, n_heads):
    n, d = q.shape
    bd = n // n_t
    hd = d // n_heads
    n_pages = page_table.shape[1]
    past = n_pages * PAGE_SIZE
    lanes = 128
    assert n_pages % (MOBA_BLOCK // PAGE_SIZE) == 0 and n_t <= MOBA_BLOCK and n_t * n_heads <= lanes - 2
    q4, k4, v4 = (a.reshape(n_t, bd, n_heads, hd) for a in (q, k, v))
    n_layers, n_phys = cache_k.shape[:2]
    ck, cv = (c.reshape(n_layers, n_phys, PAGE_SIZE * n_heads, hd) for c in (cache_k, cache_v))
    new_spec = pl.BlockSpec((n_t, 1, n_heads, hd), lambda b, p, pt: (0, b, 0, 0))
    pps = 8 if n_pages % 8 == 0 else 2
    page_specs = [pl.BlockSpec((None, None, PAGE_SIZE * n_heads, hd),
                               functools.partial(lambda g, b, p, pt: (layer, pt[b, p * pps + g], 0, 0), g))
                  for g in range(pps)]
    kern = functools.partial(_moba_sample_kernel, n_t=n_t, n_pages=n_pages, pps=pps)
    out = pl.pallas_call(
        kern,
        grid_spec=pltpu.PrefetchScalarGridSpec(
            num_scalar_prefetch=1,
            grid=(bd, n_pages // pps),
            in_specs=[new_spec, new_spec, new_spec] + page_specs + page_specs,
            out_specs=new_spec,
            scratch_shapes=[pltpu.VMEM((d, lanes), BF16), pltpu.VMEM((past, lanes), F32),
                            pltpu.VMEM((past, d), BF16)],
        ),
        out_shape=jax.ShapeDtypeStruct((n_t, bd, n_heads, hd), F32),
        compiler_params=_params("parallel", "arbitrary"),
        name="moba_sample",
    )(page_table, q4, k4, v4, *([ck] * pps), *([cv] * pps))
    return out.reshape(n, d)


def _gelu(z):
    return 0.5 * z * (1.0 + jnp.tanh(0.7978845608028654 * (z + 0.044715 * z * z * z)))


def _gmlp_front(x, g_ref, win_ref, lng_ref, lnb_ref):
    h = _rms(x, g_ref[...]).astype(BF16)
    z = _gelu(_dot(h, win_ref[...]))
    dg = z.shape[1] // 2
    u, v = z[:, :dg], z[:, dg:]
    mu = jnp.mean(v, axis=-1, keepdims=True)
    var = jnp.mean(jnp.square(v - mu), axis=-1, keepdims=True)
    vn = (v - mu) * lax.rsqrt(var + LN_EPS) * lng_ref[...] + lnb_ref[...]
    return u, vn


def _gmlp_prompt_kernel(x_ref, g_ref, win_ref, lng_ref, lnb_ref, ws_ref, bias_ref, wout_ref,
                        y_ref, vn_ref, *, tm):
    x = x_ref[...]
    u, vn = _gmlp_front(x, g_ref, win_ref, lng_ref, lnb_ref)
    vn_ref[...] = vn
    ck = GMLP_CHUNK
    gc = vn.shape[1] // GMLP_GROUPS
    row_i = lax.broadcasted_iota(jnp.int32, (ck, ck), 0)
    col_i = lax.broadcasted_iota(jnp.int32, (ck, ck), 1)
    wc = [jnp.where(row_i >= col_i, ws_ref[g], 0.0).astype(BF16) for g in range(GMLP_GROUPS)]
    vb = vn.astype(BF16)
    mix = jnp.concatenate(
        [jnp.concatenate([_dot(wc[g], vb[c * ck:(c + 1) * ck, g * gc:(g + 1) * gc])
                          for g in range(GMLP_GROUPS)], axis=-1) + bias_ref[...]
         for c in range(tm // ck)], axis=0)
    y_ref[...] = x + _dot((u * mix).astype(BF16), wout_ref[...])


def _gmlp_prompt(x, g, w_in, ln_g, ln_b, w_s, bias, w_out, tm):
    n, d = x.shape
    dg = w_out.shape[0]
    kern = functools.partial(_gmlp_prompt_kernel, tm=tm)
    return pl.pallas_call(
        kern,
        grid=(n // tm,),
        in_specs=[pl.BlockSpec((tm, d), lambda i: (i, 0)), _full((1, d)), _full(w_in.shape),
                  _full((1, dg)), _full((1, dg)), _full(w_s.shape), _full(bias.shape),
                  _full(w_out.shape)],
        out_specs=[pl.BlockSpec((tm, d), lambda i: (i, 0)), pl.BlockSpec((tm, dg), lambda i: (i, 0))],
        out_shape=[jax.ShapeDtypeStruct((n, d), F32), jax.ShapeDtypeStruct((n, dg), F32)],
        compiler_params=_params("parallel"),
        name="gmlp_prompt",
    )(x, g, w_in, ln_g, ln_b, w_s, bias, w_out)


def _gmlp_sample_kernel(ws_ref, bs_ref, x_ref, g_ref, win_ref, lng_ref, lnb_ref, wout_ref,
                        y_ref, vn_ref, *, n_t):
    x = x_ref[...]
    u, vn = _gmlp_front(x, g_ref, win_ref, lng_ref, lnb_ref)
    vn_ref[...] = vn
    bd = x.shape[0] // n_t
    gc = vn.shape[1] // GMLP_GROUPS
    rows = []
    for t in range(n_t):
        parts = []
        for g in range(GMLP_GROUPS):
            sl = slice(g * gc, (g + 1) * gc)
            acc = jnp.full((bd, gc), bs_ref[g * n_t + t], F32)
            for s in range(t + 1):
                acc = acc + ws_ref[(g * n_t + t) * n_t + s] * vn[s * bd:(s + 1) * bd, sl]
            parts.append(acc)
        rows.append(jnp.concatenate(parts, axis=-1))
    mix = jnp.concatenate(rows, axis=0)
    y_ref[...] = x + _dot((u * mix).astype(BF16), wout_ref[...])


def _gmlp_sample(x, g, w_in, ln_g, ln_b, ws_flat, bs_flat, w_out, n_t):
    n, d = x.shape
    dg = w_out.shape[0]
    kern = functools.partial(_gmlp_sample_kernel, n_t=n_t)
    smem = pl.BlockSpec(memory_space=pltpu.SMEM)
    return pl.pallas_call(
        kern,
        grid=(1,),
        in_specs=[smem, smem, _full(x.shape), _full((1, d)), _full(w_in.shape), _full((1, dg)),
                  _full((1, dg)), _full(w_out.shape)],
        out_specs=[_full((n, d)), _full((n, dg))],
        out_shape=[jax.ShapeDtypeStruct((n, d), F32), jax.ShapeDtypeStruct((n, dg), F32)],
        compiler_params=_params("arbitrary"),
        name="gmlp_sample",
    )(ws_flat, bs_flat, x, g, w_in, ln_g, ln_b, w_out)


ROUTE_LANES = 128
SLOT_ALIGN = 16


def _router_kernel(x_ref, g_ref, rw_ref, rb_ref, tri_ref, hn_ref, col_ref, row_ref, meta_ref):
    ne = MOE_GROUPS * MOE_PER_GROUP
    hb = _rms(x_ref[...], g_ref[...]).astype(BF16)
    hn_ref[...] = hb
    logits = _dot(hb, rw_ref[...]) + rb_ref[...]
    lane = lax.broadcasted_iota(jnp.int32, logits.shape, 1)
    lg = jnp.where((lane >= ne) & (lane < ne + MOE_GROUPS), logits, -jnp.inf)
    g_max = jnp.max(lg, axis=-1, keepdims=True)
    g_idx = jnp.min(jnp.where(lg == g_max, lane, ROUTE_LANES), axis=-1, keepdims=True) - ne
    g_p = 1.0 / jnp.sum(jnp.exp(lg - g_max), axis=-1, keepdims=True)
    first = g_idx * MOE_PER_GROUP
    l1 = jnp.where((lane >= first) & (lane < first + MOE_PER_GROUP), logits, -jnp.inf)
    m1 = jnp.max(l1, axis=-1, keepdims=True)
    e1 = jnp.min(jnp.where(l1 == m1, lane, ROUTE_LANES), axis=-1, keepdims=True)
    l2 = jnp.where(lane == e1, -jnp.inf, l1)
    m2 = jnp.max(l2, axis=-1, keepdims=True)
    e2 = jnp.min(jnp.where(l2 == m2, lane, ROUTE_LANES), axis=-1, keepdims=True)
    z = jnp.sum(jnp.exp(l1 - m1), axis=-1, keepdims=True)
    p1 = 1.0 / z
    p2 = jnp.exp(m2 - m1) / z
    w1 = g_p * p1 / (p1 + p2)
    w2 = g_p * p2 / (p1 + p2)
    hit1 = lane == e1
    hit2 = lane == e2
    routed = jnp.where(hit1, 1.0, 0.0) + jnp.where(hit2, 1.0, 0.0)
    rank = _dot(tri_ref[...], routed.astype(BF16))
    cnt = jnp.sum(routed, axis=0, keepdims=True)
    pad = jnp.floor((cnt + (SLOT_ALIGN - 1)) * (1.0 / SLOT_ALIGN)) * SLOT_ALIGN
    lane_r = lax.broadcasted_iota(jnp.int32, cnt.shape, 1)
    incl = pad
    shift = 1
    while shift < ne:
        incl = incl + jnp.where(lane_r >= shift, pltpu.roll(incl, shift, axis=1), 0.0)
        shift *= 2
    off = incl - pad
    slot = rank + off
    s1 = jnp.sum(jnp.where(hit1, slot, 0.0), axis=-1, keepdims=True)
    s2 = jnp.sum(jnp.where(hit2, slot, 0.0), axis=-1, keepdims=True)
    col = jnp.where(lane == 0, s1, jnp.where(lane == 1, s2, jnp.where(lane == 2, w1,
                                                                     jnp.where(lane == 3, w2, 0.0))))
    col_ref[...] = col
    row_ref[...] = col.T[:row_ref.shape[0]]
    meta = jnp.where(lane_r < ne, cnt, jnp.where(lane_r < 2 * ne, pltpu.roll(off, ne, axis=1), 0.0))
    meta_ref[...] = meta.astype(jnp.int32)


def _router(x, g, rw, rb, tri, tm):
    n, d = x.shape
    nt = n // tm
    return pl.pallas_call(
        _router_kernel,
        grid=(nt,),
        in_specs=[pl.BlockSpec((tm, d), lambda i: (i, 0)), _full((1, d)), _full(rw.shape),
                  _full(rb.shape), _full(tri.shape)],
        out_specs=[pl.BlockSpec((tm, d), lambda i: (i, 0)),
                   pl.BlockSpec((tm, ROUTE_LANES), lambda i: (i, 0)),
                   pl.BlockSpec((8, tm), lambda i: (0, i)),
                   pl.BlockSpec((None, 1, ROUTE_LANES), lambda i: (i, 0, 0))],
        out_shape=[jax.ShapeDtypeStruct((n, d), BF16), jax.ShapeDtypeStruct((n, ROUTE_LANES), F32),
                   jax.ShapeDtypeStruct((8, n), F32), jax.ShapeDtypeStruct((nt, 1, ROUTE_LANES), jnp.int32)],
        compiler_params=_params("arbitrary"),
        name="moe_router",
    )(x, g, rw, rb, tri)


def _moe_routed_kernel(cnt_ref, off_ref, g_lo_ref, g_hi_ref, s_lo_ref, s_hi_ref,
                       hn_ref, col_ref, row_ref, x_ref, gfin_ref, *refs, mc, sb, ws, final_norm):
    nw = ws * ws
    wg_refs, wu_refs, wd_refs = refs[:nw], refs[nw:2 * nw], refs[2 * nw:3 * nw]
    y_ref, xs_ref, ys_ref, gs_ref = refs[3 * nw:]
    i = pl.program_id(0)
    eg = pl.program_id(1)
    eps = wg_refs[0].shape[0]
    ne = pl.num_programs(1) * eps
    tm = hn_ref.shape[0]
    dk = hn_ref.shape[1] // ws

    def up_proj(xg, w_refs, r):
        cols = []
        for c in range(ws):
            acc = _dot(xg[:, :dk], w_refs[c][r])
            for k in range(1, ws):
                acc = acc + _dot(xg[:, k * dk:(k + 1) * dk], w_refs[k * ws + c][r])
            cols.append(acc)
        return jnp.concatenate(cols, axis=-1)

    @pl.when(eg == 0)
    def _():
        y_ref[...] = x_ref[...]
        ys_ref[...] = jnp.zeros_like(ys_ref)

    def gather(blk, carry):
        s1, s2, w1, w2 = (row_ref[r:r + 1, :] for r in range(4))
        r0 = pl.multiple_of(blk * sb, sb)
        sid = (lax.broadcasted_iota(jnp.int32, (sb, tm), 0) + r0).astype(F32)
        h1 = s1 == sid
        h2 = s2 == sid
        onehot = (jnp.where(h1, 1.0, 0.0) + jnp.where(h2, 1.0, 0.0)).astype(BF16)
        xs_ref[pl.ds(r0, sb), :] = _dot(onehot, hn_ref[...]).astype(BF16)
        gs_ref[pl.ds(r0, sb), :] = jnp.sum(
            jnp.where(h1, w1, 0.0) + jnp.where(h2, w2, 0.0), axis=-1, keepdims=True)
        return carry

    step0 = i * ne + eg * eps
    step1 = step0 + eps - 1
    lax.fori_loop(g_lo_ref[step0], g_hi_ref[step1], gather, 0)

    for r in range(eps):
        first = off_ref[step0 + r]

        def chunk(k, carry, first=first, r=r):
            r0 = pl.multiple_of(first + k * mc, SLOT_ALIGN)
            xg = xs_ref[pl.ds(r0, mc), :]
            a = up_proj(xg, wg_refs, r)
            b = up_proj(xg, wu_refs, r)
            hid = ((a * jax.nn.sigmoid(a)) * b * gs_ref[pl.ds(r0, mc), :]).astype(BF16)
            ys_ref[pl.ds(r0, mc), :] = jnp.concatenate(
                [_dot(hid, w_ref[r]) for w_ref in wd_refs], axis=-1).astype(BF16)
            return carry

        lax.fori_loop(0, (cnt_ref[step0 + r] + mc - 1) // mc, chunk, 0)

    def scatter(blk, carry):
        r0 = pl.multiple_of(blk * sb, sb)
        sid = (lax.broadcasted_iota(jnp.int32, (tm, sb), 1) + r0).astype(F32)
        onehot = (jnp.where(col_ref[:, 0:1] == sid, 1.0, 0.0)
                  + jnp.where(col_ref[:, 1:2] == sid, 1.0, 0.0)).astype(BF16)
        y_ref[...] += _dot(onehot, ys_ref[pl.ds(r0, sb), :])
        return carry

    lax.fori_loop(s_lo_ref[step0], s_hi_ref[step1], scatter, 0)

    if final_norm:
        @pl.when(eg == pl.num_programs(1) - 1)
        def _():
            y_ref[...] = _rms(y_ref[...], gfin_ref[...])


def _round_up(a, b):
    return -(-a // b) * b


def _moe_block_plan(cnt, off, mc, sb):
    reach = off + (cnt + mc - 1) // mc * mc
    g_hi = (lax.cummax(reach, axis=1) + sb - 1) // sb
    g_lo = jnp.pad(g_hi, ((0, 0), (1, 0)))[:, :-1]
    total = off[:, -1:] + _round_up_array(cnt[:, -1:], SLOT_ALIGN)
    s_hi = jnp.concatenate([off[:, 1:] // sb, (total + sb - 1) // sb], axis=1)
    s_lo = jnp.pad(s_hi, ((0, 0), (1, 0)))[:, :-1]
    return [a.reshape(-1).astype(jnp.int32) for a in (g_lo, g_hi, s_lo, s_hi)]


def _round_up_array(a, b):
    return (a + b - 1) // b * b


def _moe_routed(cnt, off, hn, col, row, wg, wu, wd, layer, x, g_final, tm):
    n, d = x.shape
    _, ne, _, de = wg.shape
    sb = 256
    mc = _round_up(2 * tm * 5 // (ne * 4), SLOT_ALIGN)
    n_slots = _round_up(2 * tm + (ne - 1) * SLOT_ALIGN + mc, sb)
    plan = _moe_block_plan(cnt, off, mc, sb)
    cnt, off = cnt.reshape(-1), off.reshape(-1)
    eps = 2 if ne % 2 == 0 else 1
    ws = 2
    final_norm = g_final is not None
    kern = functools.partial(_moe_routed_kernel, mc=mc, sb=sb, ws=ws, final_norm=final_norm)
    tile = lambda i, e, *_: (i, 0)

    def piece(shape, k, c):
        return pl.BlockSpec((None, eps) + shape, lambda i, e, *_: (layer, e, k, c))

    up_specs = [piece((d // ws, de // ws), k, c) for k in range(ws) for c in range(ws)]
    down_specs = [piece((de, d // (ws * ws)), 0, c) for c in range(ws * ws)]
    if g_final is None:
        g_final = jnp.ones((1, d), F32)
    return pl.pallas_call(
        kern,
        grid_spec=pltpu.PrefetchScalarGridSpec(
            num_scalar_prefetch=6,
            grid=(n // tm, ne // eps),
            in_specs=[pl.BlockSpec((tm, d), tile), pl.BlockSpec((tm, ROUTE_LANES), tile),
                      pl.BlockSpec((8, tm), lambda i, e, *_: (0, i)), pl.BlockSpec((tm, d), tile),
                      pl.BlockSpec((1, d), lambda i, e, *_: (0, 0))] + up_specs + up_specs + down_specs,
            out_specs=pl.BlockSpec((tm, d), tile),
            scratch_shapes=[pltpu.VMEM((n_slots, d), BF16), pltpu.VMEM((n_slots, d), BF16),
                            pltpu.VMEM((n_slots, 1), F32)],
        ),
        out_shape=jax.ShapeDtypeStruct((n, d), F32),
        compiler_params=_params("arbitrary", "arbitrary"),
        name="moe_routed",
    )(cnt, off, *plan, hn, col, row, x, g_final, *([wg] * (ws * ws)), *([wu] * (ws * ws)),
      *([wd] * (ws * ws)))


def _moe(x, g, rw, rb, wg, wu, wd, layer, g_final, tm):
    ne = wg.shape[1]
    tri = jnp.tril(jnp.ones((tm, tm), BF16), -1)
    hn, col, row, meta = _router(x, g, rw, rb, tri, tm)
    return _moe_routed(meta[:, 0, :ne], meta[:, 0, ne:2 * ne], hn, col, row, wg, wu, wd, layer, x,
                       g_final, tm)


def _tile(n, pref):
    t = min(n, pref)
    assert n % t == 0, (n, t)
    return t


def kernel(x_prompt, x_sample, state_pool, cache_k, cache_v, page_table, norm_mix, norm_ffn, norm_final, pool_w_in, pool_w_grp, pool_scale, pool_w_out, attn_w_qkv, attn_w_out, gmlp_w_in, gmlp_ln_g, gmlp_ln_b, gmlp_w_s, gmlp_b_s, gmlp_w_out, moe_rg_w, moe_rg_b, moe_re_w, moe_re_b, moe_w_gate, moe_w_up, moe_w_down):
    batch, seq, d = x_prompt.shape
    dec_b, n_t, _ = x_sample.shape
    depth = norm_mix.shape[0]
    n_heads = N_HEADS
    hd = d // n_heads
    n_pages = page_table.shape[1]
    past_len = n_pages * PAGE_SIZE
    assert seq % MOBA_BLOCK == 0 and seq % GMLP_CHUNK == 0 and n_t <= GMLP_CHUNK
    assert past_len % GMLP_CHUNK == 0 and past_len >= POOL_STATE

    n_p, n_s = batch * seq, dec_b * n_t
    xp = x_prompt.reshape(n_p, d)
    xs = x_sample.transpose(1, 0, 2).reshape(n_s, d)
    tp = _tile(seq, 512)
    row = lambda a: a.reshape(1, -1)

    def to_batch_major(a):
        return a.reshape((n_t, dec_b) + a.shape[1:]).swapaxes(0, 1)

    wg, wu, wd = (a.astype(BF16) for a in (moe_w_gate, moe_w_up, moe_w_down))
    pool_p, pool_s, k_p, v_p, k_s, v_s, gv_p, gv_s = [], [], [], [], [], [], [], []
    for i in range(depth):
        kind, j = i % N_MIXERS, i // N_MIXERS
        g_mix = row(norm_mix[i])
        if kind == 0:
            w_in, w_grp, w_out = (a.astype(BF16) for a in (pool_w_in[j], pool_w_grp[j], pool_w_out[j]))
            scale = row(pool_scale[j])
            xp, st = _pool_prompt(xp, g_mix, w_in, w_grp, scale, w_out, batch, seq, tp)
            pool_p.append(st)
            xs, u_s = _pool_sample(xs, state_pool[j].transpose(1, 0, 2), g_mix, w_in, w_grp, scale,
                                   w_out, n_t, past_len)
            pool_s.append(jnp.concatenate([state_pool[j][:, n_t:], to_batch_major(u_s)], axis=1))
        elif kind == 1:
            w_qkv, w_out = attn_w_qkv[j].astype(BF16), attn_w_out[j].astype(BF16)
            qp, kp, vp = _norm_matmul(xp, g_mix, w_qkv, 3, tp)
            qs, ks, vs = _norm_matmul(xs, g_mix, w_qkv, 3, _tile(n_s, 512))
            ap = _moba_prompt(qp, kp, vp, batch, seq, n_heads)
            a_s = _moba_sample(qs, ks, vs, cache_k, cache_v, j, page_table, n_t, n_heads)
            xp = _matmul_residual(ap, w_out, xp, tp)
            xs = _matmul_residual(a_s, w_out, xs, _tile(n_s, 512))
            k_p.append(kp.reshape(batch, seq, n_heads, hd))
            v_p.append(vp.reshape(batch, seq, n_heads, hd))
            k_s.append(to_batch_major(ks).reshape(dec_b, n_t, n_heads, hd))
            v_s.append(to_batch_major(vs).reshape(dec_b, n_t, n_heads, hd))
        else:
            w_in, w_out = gmlp_w_in[j].astype(BF16), gmlp_w_out[j].astype(BF16)
            ln_g, ln_b = row(gmlp_ln_g[j]), row(gmlp_ln_b[j])
            gc = w_out.shape[0] // GMLP_GROUPS
            bias = jnp.broadcast_to(gmlp_b_s[j].T[:, :, None], (GMLP_CHUNK, GMLP_GROUPS, gc))
            xp, vn_p = _gmlp_prompt(xp, g_mix, w_in, ln_g, ln_b, gmlp_w_s[j],
                                    bias.reshape(GMLP_CHUNK, GMLP_GROUPS * gc), w_out, tp)
            xs, vn_s = _gmlp_sample(xs, g_mix, w_in, ln_g, ln_b,
                                    gmlp_w_s[j][:, :n_t, :n_t].reshape(-1),
                                    gmlp_b_s[j][:, :n_t].reshape(-1), w_out, n_t)
            gv_p.append(vn_p.reshape(batch, seq, -1))
            gv_s.append(to_batch_major(vn_s))

        g_ffn = row(norm_ffn[i])
        ne = moe_re_w.shape[2]
        fill = jnp.zeros((d, ROUTE_LANES - ne - MOE_GROUPS), F32)
        rw = jnp.concatenate([moe_re_w[i], moe_rg_w[i], fill], axis=1).astype(BF16)
        rb = jnp.concatenate([moe_re_b[i], moe_rg_b[i], fill[0]]).reshape(1, ROUTE_LANES)
        g_fin = row(norm_final) if i == depth - 1 else None
        xp = _moe(xp, g_ffn, rw, rb, wg, wu, wd, i, g_fin, _tile(n_p, 1024))
        xs = _moe(xs, g_ffn, rw, rb, wg, wu, wd, i, g_fin, _tile(n_s, 512))

    y_p = xp.reshape(batch, seq, d)
    y_s = to_batch_major(xs)
    return (y_p, y_s, jnp.stack(pool_p), jnp.stack(pool_s), jnp.stack(k_p), jnp.stack(v_p),
            jnp.stack(k_s), jnp.stack(v_s), jnp.stack(gv_p), jnp.stack(gv_s))
```

```python
import functools

import jax
import jax.numpy as jnp
from jax import lax
from jax.experimental import pallas as pl
from jax.experimental.pallas import tpu as pltpu

F32 = jnp.float32
BF16 = jnp.bfloat16

RMS_EPS = 1e-6
LN_EPS = 1e-5
N_MIXERS = 3
POOL_WINDOWS = (2, 4, 8, 16)
POOL_STATE = max(POOL_WINDOWS) - 1
MOBA_BLOCK = 256
MOBA_TOPK = 3
GMLP_CHUNK = 128
GMLP_GROUPS = 8
MOE_GROUPS = 4
MOE_PER_GROUP = 4
PAGE_SIZE = 128
N_HEADS = 8

V7X_VMEM_LIMIT_BYTES = 56 * 1024 * 1024
NEG_BIG = -1e30
LOG2_E = 1.4426950408889634


def _params(*sem):
    return pltpu.CompilerParams(dimension_semantics=sem, vmem_limit_bytes=V7X_VMEM_LIMIT_BYTES)


def _rms(x, g):
    return x * lax.rsqrt(jnp.mean(x * x, axis=-1, keepdims=True) + RMS_EPS) * g


def _dot(a, b):
    return jnp.dot(a, b, preferred_element_type=F32)


def _dot_nt(a, b, precision=None):
    return lax.dot_general(a, b, (((1,), (1,)), ((), ())), preferred_element_type=F32,
                           precision=precision)


def _full(shape):
    return pl.BlockSpec(shape, lambda *_: (0,) * len(shape))


def _norm_matmul_kernel(x_ref, g_ref, w_ref, *out_refs):
    h = _rms(x_ref[...], g_ref[...]).astype(BF16)
    f = w_ref.shape[1] // len(out_refs)
    for i, o_ref in enumerate(out_refs):
        o_ref[...] = _dot(h, w_ref[:, i * f:(i + 1) * f])


def _norm_matmul(x, g, w, n_out, tm):
    n, d = x.shape
    f = w.shape[1] // n_out
    return pl.pallas_call(
        _norm_matmul_kernel,
        grid=(n // tm,),
        in_specs=[pl.BlockSpec((tm, d), lambda i: (i, 0)), _full((1, d)), _full(w.shape)],
        out_specs=[pl.BlockSpec((tm, f), lambda i: (i, 0))] * n_out,
        out_shape=[jax.ShapeDtypeStruct((n, f), F32)] * n_out,
        compiler_params=_params("parallel"),
        name="norm_matmul",
    )(x, g, w)


def _matmul_residual_kernel(a_ref, w_ref, x_ref, y_ref):
    y_ref[...] = x_ref[...] + _dot(a_ref[...].astype(BF16), w_ref[...])


def _matmul_residual(a, w, x, tm):
    n, d = x.shape
    k = a.shape[1]
    return pl.pallas_call(
        _matmul_residual_kernel,
        grid=(n // tm,),
        in_specs=[pl.BlockSpec((tm, k), lambda i: (i, 0)), _full(w.shape),
                  pl.BlockSpec((tm, d), lambda i: (i, 0))],
        out_specs=pl.BlockSpec((tm, d), lambda i: (i, 0)),
        out_shape=jax.ShapeDtypeStruct((n, d), F32),
        compiler_params=_params("parallel"),
        name="matmul_residual",
    )(a, w, x)


def _pool_tail(r, x, wgrp_ref, scale_ref, wout_ref):
    gc = wgrp_ref.shape[1]
    rb = r.astype(BF16)
    t = jnp.concatenate([_dot(rb[:, g * gc:(g + 1) * gc], wgrp_ref[g])
                         for g in range(len(POOL_WINDOWS))], axis=-1)
    t = (t * scale_ref[...]).astype(BF16)
    return x + _dot(t, wout_ref[...])


def _pool_prompt_kernel(x_ref, g_ref, win_ref, wgrp_ref, scale_ref, wout_ref,
                        y_ref, st_ref, carry_ref, *, ts, ns):
    s = pl.program_id(1)
    halo = carry_ref.shape[0]
    gc = wgrp_ref.shape[1]

    @pl.when(s == 0)
    def _():
        carry_ref[...] = jnp.zeros_like(carry_ref)

    x = x_ref[...]
    h = _rms(x, g_ref[...]).astype(BF16)
    u = _dot(h, win_ref[...])
    ext = jnp.concatenate([carry_ref[...], u], axis=0)
    pos = s * ts + lax.broadcasted_iota(jnp.int32, (ts, 1), 0)
    parts = []
    acc = ext
    width = 1
    for w in POOL_WINDOWS:
        while width < w:
            acc = acc + pltpu.roll(acc, width, axis=0)
            width *= 2
        assert width == w
        cnt = jnp.minimum(pos + 1, w).astype(F32)
        parts.append(acc[halo:, :gc] / cnt)
        acc = acc[:, gc:]
    r = jnp.concatenate(parts, axis=-1) - u
    y_ref[...] = _pool_tail(r, x, wgrp_ref, scale_ref, wout_ref)
    carry_ref[...] = u[ts - halo:, :]

    @pl.when(s == ns - 1)
    def _():
        st_ref[...] = carry_ref[halo - POOL_STATE:, :]


def _pool_prompt(x, g, w_in, w_grp, scale, w_out, batch, seq, ts):
    n, d = x.shape
    ns = seq // ts
    halo = 16
    kern = functools.partial(_pool_prompt_kernel, ts=ts, ns=ns)
    return pl.pallas_call(
        kern,
        grid=(batch, ns),
        in_specs=[pl.BlockSpec((ts, d), lambda b, s: (b * ns + s, 0)), _full((1, d)),
                  _full(w_in.shape), _full(w_grp.shape), _full((1, d)), _full(w_out.shape)],
        out_specs=[pl.BlockSpec((ts, d), lambda b, s: (b * ns + s, 0)),
                   pl.BlockSpec((None, POOL_STATE, d), lambda b, s: (b, 0, 0))],
        out_shape=[jax.ShapeDtypeStruct((n, d), F32),
                   jax.ShapeDtypeStruct((batch, POOL_STATE, d), F32)],
        scratch_shapes=[pltpu.VMEM((halo, d), F32)],
        compiler_params=_params("parallel", "arbitrary"),
        name="pool_prompt",
    )(x, g, w_in, w_grp, scale, w_out)


def _pool_sample_kernel(x_ref, pref_ref, g_ref, win_ref, wgrp_ref, scale_ref, wout_ref,
                        y_ref, u_ref, *, n_t, pos0):
    bd = x_ref.shape[0] // n_t
    gc = wgrp_ref.shape[1]
    x = x_ref[...]
    h = _rms(x, g_ref[...]).astype(BF16)
    u = _dot(h, win_ref[...])
    u_ref[...] = u
    us = [u[t * bd:(t + 1) * bd] for t in range(n_t)]
    tail = [jnp.zeros_like(us[0])]
    for m in range(1, POOL_STATE + 1):
        tail.append(tail[-1] + pref_ref[POOL_STATE - m])
    rows = []
    for t in range(n_t):
        parts = []
        for gi, w in enumerate(POOL_WINDOWS):
            sl = slice(gi * gc, (gi + 1) * gc)
            n_new = min(w, t + 1)
            wsum = us[t][:, sl]
            for k in range(1, n_new):
                wsum = wsum + us[t - k][:, sl]
            if w - n_new:
                wsum = wsum + tail[w - n_new][:, sl]
            parts.append(wsum / float(min(pos0 + t + 1, w)))
        rows.append(jnp.concatenate(parts, axis=-1) - us[t])
    r = jnp.concatenate(rows, axis=0)
    y_ref[...] = _pool_tail(r, x, wgrp_ref, scale_ref, wout_ref)


def _pool_sample(x, pref_t, g, w_in, w_grp, scale, w_out, n_t, pos0):
    n, d = x.shape
    kern = functools.partial(_pool_sample_kernel, n_t=n_t, pos0=pos0)
    return pl.pallas_call(
        kern,
        grid=(1,),
        in_specs=[_full(x.shape), _full(pref_t.shape), _full((1, d)), _full(w_in.shape),
                  _full(w_grp.shape), _full((1, d)), _full(w_out.shape)],
        out_specs=[_full((n, d)), _full((n, d))],
        out_shape=[jax.ShapeDtypeStruct((n, d), F32)] * 2,
        compiler_params=_params("arbitrary"),
        name="pool_sample",
    )(x, pref_t, g, w_in, w_grp, scale, w_out)


def _top_blocks(gate, n_valid, n_blocks):
    idx = lax.broadcasted_iota(jnp.int32, gate.shape, 0)
    valid = idx < n_valid
    rows = []
    for j in range(n_blocks):
        gj = gate[j:j + 1, :]
        beats = valid & ((gate > gj) | ((gate == gj) & (idx < j)))
        rank = jnp.sum(beats.astype(F32), axis=0, keepdims=True)
        rows.append(jnp.where(rank < MOBA_TOPK, jnp.where(j < n_valid, 1.0, 0.0), 0.0))
    return rows


def _moba_prompt_kernel(q_ref, k_ref, v_ref, o_ref, kb_ref, vt_ref, kmean_ref, sel_ref, s_ref,
                        *, nb, group, hps, hd):
    qi = pl.program_id(2)
    blk = MOBA_BLOCK
    scale = hd ** -0.5
    heads = range(hps)
    lanes = lambda h: slice(h * hd, (h + 1) * hd)

    @pl.when(qi == 0)
    def _():
        k = k_ref[...]
        kb_ref[...] = k.astype(BF16)
        kmean_ref[...] = jnp.mean(k.reshape(nb, blk, hps * hd), axis=1)
        vt_ref[...] = v_ref[...].T.astype(BF16)

    q = q_ref[...]
    qb = [q[:, lanes(h)].astype(BF16) for h in heads]
    for h in heads:
        gate = _dot_nt(kmean_ref[:, lanes(h)].astype(BF16), qb[h])
        for j, row in enumerate(_top_blocks(gate, qi, nb)):
            sel_ref[h * nb + j:h * nb + j + 1, :] = row
    key_i = lax.broadcasted_iota(jnp.int32, (blk, blk), 0)
    qry_i = lax.broadcasted_iota(jnp.int32, (blk, blk), 1)
    scale2 = scale * LOG2_E

    def scores(h, j, n):
        kj = kb_ref[pl.ds(pl.multiple_of(j * blk, blk), n * blk), lanes(h)]
        return _dot_nt(kj, qb[h]) * scale2

    def pv(h, j, n, p):
        vj = vt_ref[lanes(h), pl.ds(pl.multiple_of(j * blk, blk), n * blk)]
        return _dot(vj, p.astype(BF16))

    n_trips = (qi + group - 1) // group
    own = nb * blk
    ms = []
    for h in heads:
        s = jnp.where(key_i <= qry_i, scores(h, qi, 1), NEG_BIG)
        s_ref[h, own:own + blk, :] = s
        ms.append(jnp.max(s, axis=0, keepdims=True))

    def score_pass(g, ms):
        j = g * group
        out = []
        for h, m in zip(heads, ms):
            keep = jnp.concatenate([jnp.broadcast_to(sel_ref[pl.ds(h * nb + j + r, 1), :], (blk, blk))
                                    for r in range(group)], axis=0)
            s = jnp.where(keep > 0.0, scores(h, j, group), NEG_BIG)
            s_ref[h, pl.ds(pl.multiple_of(j * blk, blk), group * blk), :] = s
            out.append(jnp.maximum(m, jnp.max(s, axis=0, keepdims=True)))
        return tuple(out)

    ms = lax.fori_loop(0, n_trips, score_pass, tuple(ms))

    carry = []
    for h in heads:
        p = jnp.exp2(s_ref[h, own:own + blk, :] - ms[h])
        carry.append((jnp.sum(p, axis=0, keepdims=True), pv(h, qi, 1, p)))

    def value_pass(g, carry):
        j = g * group
        out = []
        for h, (l, acc) in zip(heads, carry):
            p = jnp.exp2(s_ref[h, pl.ds(pl.multiple_of(j * blk, blk), group * blk), :] - ms[h])
            out.append((l + jnp.sum(p, axis=0, keepdims=True), acc + pv(h, j, group, p)))
        return tuple(out)

    carry = lax.fori_loop(0, n_trips, value_pass, tuple(carry))
    for h, (l, acc) in zip(heads, carry):
        o_ref[:, lanes(h)] = (acc / l).T.astype(o_ref.dtype)


def _moba_prompt(q, k, v, batch, seq, n_heads):
    n, d = q.shape
    hd = d // n_heads
    nb = seq // MOBA_BLOCK
    group = 4 if nb % 4 == 0 else 1
    hps = 2 if n_heads % 2 == 0 else 1
    kern = functools.partial(_moba_prompt_kernel, nb=nb, group=group, hps=hps, hd=hd)
    return pl.pallas_call(
        kern,
        grid=(batch, n_heads // hps, nb),
        in_specs=[pl.BlockSpec((MOBA_BLOCK, hps * hd), lambda b, h, i: (b * nb + i, h)),
                  pl.BlockSpec((seq, hps * hd), lambda b, h, i: (b, h)),
                  pl.BlockSpec((seq, hps * hd), lambda b, h, i: (b, h))],
        out_specs=pl.BlockSpec((MOBA_BLOCK, hps * hd), lambda b, h, i: (b * nb + i, h)),
        out_shape=jax.ShapeDtypeStruct((n, d), BF16),
        scratch_shapes=[pltpu.VMEM((seq, hps * hd), BF16), pltpu.VMEM((hps * hd, seq), BF16),
                        pltpu.VMEM((nb, hps * hd), F32), pltpu.VMEM((hps * nb, MOBA_BLOCK), F32),
                        pltpu.VMEM((hps, (nb + 1) * MOBA_BLOCK, MOBA_BLOCK), F32)],
        compiler_params=_params("parallel", "parallel", "arbitrary"),
        name="moba_prompt",
    )(q, k, v)


def _moba_sample_kernel(pt_ref, q_ref, kn_ref, vn_ref, *refs, n_t, n_pages, pps):
    del pt_ref
    kp_refs, vp_refs = refs[:pps], refs[pps:2 * pps]
    o_ref, wbd_ref, s_ref, vall_ref = refs[2 * pps:]
    p_i = pl.program_id(1)
    nh, hd = q_ref.shape[2], q_ref.shape[3]
    lanes = wbd_ref.shape[1]
    scale = hd ** -0.5
    n_past = n_pages // (MOBA_BLOCK // PAGE_SIZE)
    head_i = lax.broadcasted_iota(jnp.int32, (nh, hd), 0)

    @pl.when(p_i == 0)
    def _():
        rows = [jnp.concatenate([jnp.where(head_i == h, q_ref[t, 0], 0.0) for h in range(nh)], axis=-1)
                for t in range(n_t)]
        rows.append(jnp.zeros((lanes - n_t * nh, nh * hd), F32))
        wbd_ref[...] = jnp.concatenate(rows, axis=0).T.astype(BF16)

    for g, (kp_ref, vp_ref) in enumerate(zip(kp_refs, vp_refs)):
        off = pl.multiple_of((p_i * pps + g) * PAGE_SIZE, PAGE_SIZE)
        acc = None
        for h in range(nh):
            kh = kp_ref[pl.ds(h, PAGE_SIZE, stride=nh), :].astype(BF16)
            part = _dot(kh, wbd_ref[h * hd:(h + 1) * hd, :])
            acc = part if acc is None else acc + part
            vall_ref[pl.ds(off, PAGE_SIZE), h * hd:(h + 1) * hd] = (
                vp_ref[pl.ds(h, PAGE_SIZE, stride=nh), :].astype(BF16))
        s_ref[pl.ds(off, PAGE_SIZE), :] = acc

    @pl.when(p_i == n_pages // pps - 1)
    def _():
        s_all = s_ref[...]
        blocks = [s_all[j * MOBA_BLOCK:(j + 1) * MOBA_BLOCK] for j in range(n_past)]
        gate = [jnp.sum(b, axis=0, keepdims=True) * (1.0 / MOBA_BLOCK) for b in blocks]
        sel = []
        for j in range(n_past):
            rank = jnp.zeros_like(gate[j])
            for i in range(n_past):
                if i != j:
                    beats = (gate[i] >= gate[j]) if i < j else (gate[i] > gate[j])
                    rank = rank + beats.astype(F32)
            sel.append(rank < MOBA_TOPK)
        s_past = jnp.concatenate(
            [jnp.where(sel[j], blocks[j] * scale, NEG_BIG) for j in range(n_past)], axis=0)
        m_row = jnp.max(s_past, axis=0, keepdims=True)
        p_past = jnp.exp(s_past - m_row)
        l_row = jnp.sum(p_past, axis=0, keepdims=True)
        full = _dot(p_past.T.astype(BF16), vall_ref[...])
        stats = jnp.concatenate([m_row, l_row, jnp.zeros((lanes - 2, lanes), F32)], axis=0).T
        for t in range(n_t):
            rows = full[t * nh:(t + 1) * nh]
            out = functools.reduce(
                lambda a, b: a + b,
                [jnp.where(head_i == h, rows[:, h * hd:(h + 1) * hd], 0.0) for h in range(nh)])
            m_past = stats[t * nh:(t + 1) * nh, 0:1]
            l_past = stats[t * nh:(t + 1) * nh, 1:2]
            s_new = [jnp.sum(q_ref[t, 0] * kn_ref[t2, 0], axis=-1, keepdims=True) * scale
                     for t2 in range(t + 1)]
            m = functools.reduce(jnp.maximum, s_new, m_past)
            alpha = jnp.exp(m_past - m)
            out = alpha * out
            l = alpha * l_past
            for t2 in range(t + 1):
                p_new = jnp.exp(s_new[t2] - m)
                l = l + p_new
                out = out + p_new * vn_ref[t2, 0]
            o_ref[t, 0] = out / l


def _moba_sample(q, k, v, cache_k, cache_v, layer, page_table, n_t, n_heads):
    n, d = q.shape
    bd = n // n_t
    hd = d // n_heads
    n_pages = page_table.shape[1]
    past = n_pages * PAGE_SIZE
    lanes = 128
    assert n_pages % (MOBA_BLOCK // PAGE_SIZE) == 0 and n_t <= MOBA_BLOCK and n_t * n_heads <= lanes - 2
    q4, k4, v4 = (a.reshape(n_t, bd, n_heads, hd) for a in (q, k, v))
    n_layers, n_phys = cache_k.shape[:2]
    ck, cv = (c.reshape(n_layers, n_phys, PAGE_SIZE * n_heads, hd) for c in (cache_k, cache_v))
    new_spec = pl.BlockSpec((n_t, 1, n_heads, hd), lambda b, p, pt: (0, b, 0, 0))
    pps = 8 if n_pages % 8 == 0 else 2
    page_specs = [pl.BlockSpec((None, None, PAGE_SIZE * n_heads, hd),
                               functools.partial(lambda g, b, p, pt: (layer, pt[b, p * pps + g], 0, 0), g))
                  for g in range(pps)]
    kern = functools.partial(_moba_sample_kernel, n_t=n_t, n_pages=n_pages, pps=pps)
    out = pl.pallas_call(
        kern,
        grid_spec=pltpu.PrefetchScalarGridSpec(
            num_scalar_prefetch=1,
            grid=(bd, n_pages // pps),
            in_specs=[new_spec, new_spec, new_spec] + page_specs + page_specs,
            out_specs=new_spec,
            scratch_shapes=[pltpu.VMEM((d, lanes), BF16), pltpu.VMEM((past, lanes), F32),
                            pltpu.VMEM((past, d), BF16)],
        ),
        out_shape=jax.ShapeDtypeStruct((n_t, bd, n_heads, hd), F32),
        compiler_params=_params("parallel", "arbitrary"),
        name="moba_sample",
    )(page_table, q4, k4, v4, *([ck] * pps), *([cv] * pps))
    return out.reshape(n, d)


def _gelu(z):
    return 0.5 * z * (1.0 + jnp.tanh(0.7978845608028654 * (z + 0.044715 * z * z * z)))


def _gmlp_front(x, g_ref, win_ref, lng_ref, lnb_ref):
    h = _rms(x, g_ref[...]).astype(BF16)
    z = _gelu(_dot(h, win_ref[...]))
    dg = z.shape[1] // 2
    u, v = z[:, :dg], z[:, dg:]
    mu = jnp.mean(v, axis=-1, keepdims=True)
    var = jnp.mean(jnp.square(v - mu), axis=-1, keepdims=True)
    vn = (v - mu) * lax.rsqrt(var + LN_EPS) * lng_ref[...] + lnb_ref[...]
    return u, vn


def _gmlp_prompt_kernel(x_ref, g_ref, win_ref, lng_ref, lnb_ref, ws_ref, bias_ref, wout_ref,
                        y_ref, vn_ref, *, tm):
    x = x_ref[...]
    u, vn = _gmlp_front(x, g_ref, win_ref, lng_ref, lnb_ref)
    vn_ref[...] = vn
    ck = GMLP_CHUNK
    gc = vn.shape[1] // GMLP_GROUPS
    row_i = lax.broadcasted_iota(jnp.int32, (ck, ck), 0)
    col_i = lax.broadcasted_iota(jnp.int32, (ck, ck), 1)
    wc = [jnp.where(row_i >= col_i, ws_ref[g], 0.0).astype(BF16) for g in range(GMLP_GROUPS)]
    vb = vn.astype(BF16)
    mix = jnp.concatenate(
        [jnp.concatenate([_dot(wc[g], vb[c * ck:(c + 1) * ck, g * gc:(g + 1) * gc])
                          for g in range(GMLP_GROUPS)], axis=-1) + bias_ref[...]
         for c in range(tm // ck)], axis=0)
    y_ref[...] = x + _dot((u * mix).astype(BF16), wout_ref[...])


def _gmlp_prompt(x, g, w_in, ln_g, ln_b, w_s, bias, w_out, tm):
    n, d = x.shape
    dg = w_out.shape[0]
    kern = functools.partial(_gmlp_prompt_kernel, tm=tm)
    return pl.pallas_call(
        kern,
        grid=(n // tm,),
        in_specs=[pl.BlockSpec((tm, d), lambda i: (i, 0)), _full((1, d)), _full(w_in.shape),
                  _full((1, dg)), _full((1, dg)), _full(w_s.shape), _full(bias.shape),
                  _full(w_out.shape)],
        out_specs=[pl.BlockSpec((tm, d), lambda i: (i, 0)), pl.BlockSpec((tm, dg), lambda i: (i, 0))],
        out_shape=[jax.ShapeDtypeStruct((n, d), F32), jax.ShapeDtypeStruct((n, dg), F32)],
        compiler_params=_params("parallel"),
        name="gmlp_prompt",
    )(x, g, w_in, ln_g, ln_b, w_s, bias, w_out)


def _gmlp_sample_kernel(ws_ref, bs_ref, x_ref, g_ref, win_ref, lng_ref, lnb_ref, wout_ref,
                        y_ref, vn_ref, *, n_t):
    x = x_ref[...]
    u, vn = _gmlp_front(x, g_ref, win_ref, lng_ref, lnb_ref)
    vn_ref[...] = vn
    bd = x.shape[0] // n_t
    gc = vn.shape[1] // GMLP_GROUPS
    rows = []
    for t in range(n_t):
        parts = []
        for g in range(GMLP_GROUPS):
            sl = slice(g * gc, (g + 1) * gc)
            acc = jnp.full((bd, gc), bs_ref[g * n_t + t], F32)
            for s in range(t + 1):
                acc = acc + ws_ref[(g * n_t + t) * n_t + s] * vn[s * bd:(s + 1) * bd, sl]
            parts.append(acc)
        rows.append(jnp.concatenate(parts, axis=-1))
    mix = jnp.concatenate(rows, axis=0)
    y_ref[...] = x + _dot((u * mix).astype(BF16), wout_ref[...])


def _gmlp_sample(x, g, w_in, ln_g, ln_b, ws_flat, bs_flat, w_out, n_t):
    n, d = x.shape
    dg = w_out.shape[0]
    kern = functools.partial(_gmlp_sample_kernel, n_t=n_t)
    smem = pl.BlockSpec(memory_space=pltpu.SMEM)
    return pl.pallas_call(
        kern,
        grid=(1,),
        in_specs=[smem, smem, _full(x.shape), _full((1, d)), _full(w_in.shape), _full((1, dg)),
                  _full((1, dg)), _full(w_out.shape)],
        out_specs=[_full((n, d)), _full((n, dg))],
        out_shape=[jax.ShapeDtypeStruct((n, d), F32), jax.ShapeDtypeStruct((n, dg), F32)],
        compiler_params=_params("arbitrary"),
        name="gmlp_sample",
    )(ws_flat, bs_flat, x, g, w_in, ln_g, ln_b, w_out)


ROUTE_LANES = 128
SLOT_ALIGN = 16
MOE_SUB = 256
META_ROWS = 8


def _router_kernel(x_ref, g_ref, rw_ref, rb_ref, tri_ref, hn_ref, col_ref, row_ref, meta_ref, *, sub):
    ne = MOE_GROUPS * MOE_PER_GROUP
    hb = _rms(x_ref[...], g_ref[...]).astype(BF16)
    hn_ref[...] = hb
    logits = _dot(hb, rw_ref[...]) + rb_ref[...]
    lane = lax.broadcasted_iota(jnp.int32, logits.shape, 1)
    lg = jnp.where((lane >= ne) & (lane < ne + MOE_GROUPS), logits, -jnp.inf)
    g_max = jnp.max(lg, axis=-1, keepdims=True)
    g_idx = jnp.min(jnp.where(lg == g_max, lane, ROUTE_LANES), axis=-1, keepdims=True) - ne
    g_p = 1.0 / jnp.sum(jnp.exp(lg - g_max), axis=-1, keepdims=True)
    first = g_idx * MOE_PER_GROUP
    l1 = jnp.where((lane >= first) & (lane < first + MOE_PER_GROUP), logits, -jnp.inf)
    m1 = jnp.max(l1, axis=-1, keepdims=True)
    e1 = jnp.min(jnp.where(l1 == m1, lane, ROUTE_LANES), axis=-1, keepdims=True)
    l2 = jnp.where(lane == e1, -jnp.inf, l1)
    m2 = jnp.max(l2, axis=-1, keepdims=True)
    e2 = jnp.min(jnp.where(l2 == m2, lane, ROUTE_LANES), axis=-1, keepdims=True)
    z = jnp.sum(jnp.exp(l1 - m1), axis=-1, keepdims=True)
    p1 = 1.0 / z
    p2 = jnp.exp(m2 - m1) / z
    w1 = g_p * p1 / (p1 + p2)
    w2 = g_p * p2 / (p1 + p2)
    hit1 = lane == e1
    hit2 = lane == e2
    routed = jnp.where(hit1, 1.0, 0.0) + jnp.where(hit2, 1.0, 0.0)
    rank = _dot(tri_ref[...], routed.astype(BF16))
    n_sub = logits.shape[0] // sub
    cnt = jnp.concatenate([jnp.sum(routed[s * sub:(s + 1) * sub], axis=0, keepdims=True)
                           for s in range(n_sub)]
                          + [jnp.zeros((meta_ref.shape[0] - n_sub, ROUTE_LANES), F32)], axis=0)
    pad = jnp.floor((cnt + (SLOT_ALIGN - 1)) * (1.0 / SLOT_ALIGN)) * SLOT_ALIGN
    lane_r = lax.broadcasted_iota(jnp.int32, cnt.shape, 1)
    incl = pad
    shift = 1
    while shift < ne:
        incl = incl + jnp.where(lane_r >= shift, pltpu.roll(incl, shift, axis=1), 0.0)
        shift *= 2
    off = incl - pad
    slot = rank + jnp.concatenate([jnp.broadcast_to(off[s:s + 1], (sub, ROUTE_LANES))
                                   for s in range(n_sub)], axis=0)
    s1 = jnp.sum(jnp.where(hit1, slot, 0.0), axis=-1, keepdims=True)
    s2 = jnp.sum(jnp.where(hit2, slot, 0.0), axis=-1, keepdims=True)
    col = jnp.where(lane == 0, s1, jnp.where(lane == 1, s2, jnp.where(lane == 2, w1,
                                                                     jnp.where(lane == 3, w2, 0.0))))
    col_ref[...] = col
    row_ref[...] = col.T[:row_ref.shape[0]]
    meta = jnp.where(lane_r < ne, cnt, jnp.where(lane_r < 2 * ne, pltpu.roll(off, ne, axis=1), 0.0))
    meta_ref[...] = meta.astype(jnp.int32)


def _router(x, g, rw, rb, tri, tm, sub):
    n, d = x.shape
    nt = n // tm
    return pl.pallas_call(
        functools.partial(_router_kernel, sub=sub),
        grid=(nt,),
        in_specs=[pl.BlockSpec((tm, d), lambda i: (i, 0)), _full((1, d)), _full(rw.shape),
                  _full(rb.shape), _full(tri.shape)],
        out_specs=[pl.BlockSpec((tm, d), lambda i: (i, 0)),
                   pl.BlockSpec((tm, ROUTE_LANES), lambda i: (i, 0)),
                   pl.BlockSpec((8, tm), lambda i: (0, i)),
                   pl.BlockSpec((None, META_ROWS, ROUTE_LANES), lambda i: (i, 0, 0))],
        out_shape=[jax.ShapeDtypeStruct((n, d), BF16), jax.ShapeDtypeStruct((n, ROUTE_LANES), F32),
                   jax.ShapeDtypeStruct((8, n), F32),
                   jax.ShapeDtypeStruct((nt, META_ROWS, ROUTE_LANES), jnp.int32)],
        compiler_params=_params("arbitrary"),
        name="moe_router",
    )(x, g, rw, rb, tri)


def _moe_routed_kernel(cnt_ref, off_ref, trips_ref, hn_ref, col_ref, row_ref, x_ref, gfin_ref,
                       wg_ref, wu_ref, wd_ref, y_ref, xs_ref, ys_ref, gs_ref,
                       *, sub, mcs, n_gather, n_scatter, final_norm):
    i = pl.program_id(0)
    eg = pl.program_id(1)
    eps = wg_ref.shape[0]
    ne = pl.num_programs(1) * eps
    n_sub = xs_ref.shape[0]
    dump = n_gather
    rows_of = lambda s: slice(s * sub, (s + 1) * sub)

    @pl.when(eg == 0)
    def _():
        y_ref[...] = x_ref[...]
        ys_ref[...] = jnp.zeros_like(ys_ref)
        sid = lax.broadcasted_iota(jnp.int32, (n_gather, sub), 0).astype(F32)
        for s in range(n_sub):
            s1, s2, w1, w2 = (row_ref[r:r + 1, rows_of(s)] for r in range(4))
            h1 = s1 == sid
            h2 = s2 == sid
            onehot = (jnp.where(h1, 1.0, 0.0) + jnp.where(h2, 1.0, 0.0)).astype(BF16)
            xs_ref[s, 0:n_gather, :] = _dot(onehot, hn_ref[rows_of(s), :]).astype(BF16)
            xs_ref[s, dump:dump + mcs, :] = jnp.zeros((mcs, xs_ref.shape[2]), BF16)
            gs_ref[s, 0:n_gather, :] = jnp.sum(
                jnp.where(h1, w1, 0.0) + jnp.where(h2, w2, 0.0), axis=-1, keepdims=True)
            gs_ref[s, dump:dump + mcs, :] = jnp.zeros((mcs, 1), F32)

    for r in range(eps):
        e = eg * eps + r
        counts = [cnt_ref[(i * n_sub + s) * ne + e] for s in range(n_sub)]
        firsts = [off_ref[(i * n_sub + s) * ne + e] for s in range(n_sub)]

        def chunk(k, carry, r=r, counts=counts, firsts=firsts):
            rows = [pl.multiple_of(jnp.where(k * mcs < counts[s], firsts[s] + k * mcs, dump), SLOT_ALIGN)
                    for s in range(n_sub)]
            xg = jnp.concatenate([xs_ref[s, pl.ds(rows[s], mcs), :] for s in range(n_sub)], axis=0)
            gate = jnp.concatenate([gs_ref[s, pl.ds(rows[s], mcs), :] for s in range(n_sub)], axis=0)
            a = _dot(xg, wg_ref[r])
            b = _dot(xg, wu_ref[r])
            hid = ((a * jax.nn.sigmoid(a)) * b * gate).astype(BF16)
            out = _dot(hid, wd_ref[r]).astype(BF16)
            for s in range(n_sub):
                ys_ref[s, pl.ds(rows[s], mcs), :] = out[s * mcs:(s + 1) * mcs]
            return carry

        lax.fori_loop(0, trips_ref[i * ne + e], chunk, 0)

    @pl.when(eg == pl.num_programs(1) - 1)
    def _():
        sid = lax.broadcasted_iota(jnp.int32, (sub, n_scatter), 1).astype(F32)
        for s in range(n_sub):
            onehot = (jnp.where(col_ref[rows_of(s), 0:1] == sid, 1.0, 0.0)
                      + jnp.where(col_ref[rows_of(s), 1:2] == sid, 1.0, 0.0)).astype(BF16)
            out = y_ref[rows_of(s), :] + _dot(onehot, ys_ref[s, 0:n_scatter, :])
            y_ref[rows_of(s), :] = _rms(out, gfin_ref[...]) if final_norm else out


def _round_up(a, b):
    return -(-a // b) * b


def _moe_routed(cnt, off, hn, col, row, wg, wu, wd, layer, x, g_final, tm, sub):
    n, d = x.shape
    _, ne, _, de = wg.shape
    n_sub = tm // sub
    mcs = _round_up(2 * sub * 3 // (ne * 2), SLOT_ALIGN)
    most = 2 * sub + (ne - 1) * SLOT_ALIGN
    n_scatter = _round_up(most, 128)
    n_gather = _round_up(most + mcs, 128)
    trips = jnp.max((cnt + mcs - 1) // mcs, axis=1).reshape(-1)
    eps = 2 if ne % 2 == 0 else 1
    final_norm = g_final is not None
    if g_final is None:
        g_final = jnp.ones((1, d), F32)
    kern = functools.partial(_moe_routed_kernel, sub=sub, mcs=mcs, n_gather=n_gather, n_scatter=n_scatter,
                             final_norm=final_norm)
    tile = lambda i, e, *_: (i, 0)
    expert = lambda i, e, *_: (layer, e, 0, 0)
    return pl.pallas_call(
        kern,
        grid_spec=pltpu.PrefetchScalarGridSpec(
            num_scalar_prefetch=3,
            grid=(n // tm, ne // eps),
            in_specs=[pl.BlockSpec((tm, d), tile), pl.BlockSpec((tm, ROUTE_LANES), tile),
                      pl.BlockSpec((8, tm), lambda i, e, *_: (0, i)), pl.BlockSpec((tm, d), tile),
                      pl.BlockSpec((1, d), lambda i, e, *_: (0, 0)),
                      pl.BlockSpec((None, eps, d, de), expert), pl.BlockSpec((None, eps, d, de), expert),
                      pl.BlockSpec((None, eps, de, d), expert)],
            out_specs=pl.BlockSpec((tm, d), tile),
            scratch_shapes=[pltpu.VMEM((n_sub, n_gather + mcs, d), BF16),
                            pltpu.VMEM((n_sub, n_gather + mcs, d), BF16),
                            pltpu.VMEM((n_sub, n_gather + mcs, 1), F32)],
        ),
        out_shape=jax.ShapeDtypeStruct((n, d), F32),
        compiler_params=_params("arbitrary", "arbitrary"),
        name="moe_routed",
    )(cnt.reshape(-1), off.reshape(-1), trips, hn, col, row, x, g_final, wg, wu, wd)


def _moe(x, g, rw, rb, wg, wu, wd, layer, g_final, tm):
    ne = wg.shape[1]
    sub = min(tm, MOE_SUB)
    n_sub = tm // sub
    assert tm % sub == 0 and n_sub <= META_ROWS
    blocks = jnp.arange(tm) // sub
    tri = ((blocks[:, None] == blocks[None, :]) & (jnp.arange(tm)[:, None] > jnp.arange(tm)[None, :]))
    hn, col, row, meta = _router(x, g, rw, rb, tri.astype(BF16), tm, sub)
    return _moe_routed(meta[:, :n_sub, :ne], meta[:, :n_sub, ne:2 * ne], hn, col, row, wg, wu, wd, layer, x,
                       g_final, tm, sub)


def _tile(n, pref):
    t = min(n, pref)
    assert n % t == 0, (n, t)
    return t


def kernel(x_prompt, x_sample, state_pool, cache_k, cache_v, page_table, norm_mix, norm_ffn, norm_final, pool_w_in, pool_w_grp, pool_scale, pool_w_out, attn_w_qkv, attn_w_out, gmlp_w_in, gmlp_ln_g, gmlp_ln_b, gmlp_w_s, gmlp_b_s, gmlp_w_out, moe_rg_w, moe_rg_b, moe_re_w, moe_re_b, moe_w_gate, moe_w_up, moe_w_down):
    batch, seq, d = x_prompt.shape
    dec_b, n_t, _ = x_sample.shape
    depth = norm_mix.shape[0]
    n_heads = N_HEADS
    hd = d // n_heads
    n_pages = page_table.shape[1]
    past_len = n_pages * PAGE_SIZE
    assert seq % MOBA_BLOCK == 0 and seq % GMLP_CHUNK == 0 and n_t <= GMLP_CHUNK
    assert past_len % GMLP_CHUNK == 0 and past_len >= POOL_STATE

    n_p, n_s = batch * seq, dec_b * n_t
    xp = x_prompt.reshape(n_p, d)
    xs = x_sample.transpose(1, 0, 2).reshape(n_s, d)
    tp = _tile(seq, 512)
    row = lambda a: a.reshape(1, -1)

    def to_batch_major(a):
        return a.reshape((n_t, dec_b) + a.shape[1:]).swapaxes(0, 1)

    wg, wu, wd = (a.astype(BF16) for a in (moe_w_gate, moe_w_up, moe_w_down))
    pool_p, pool_s, k_p, v_p, k_s, v_s, gv_p, gv_s = [], [], [], [], [], [], [], []
    for i in range(depth):
        kind, j = i % N_MIXERS, i // N_MIXERS
        g_mix = row(norm_mix[i])
        if kind == 0:
            w_in, w_grp, w_out = (a.astype(BF16) for a in (pool_w_in[j], pool_w_grp[j], pool_w_out[j]))
            scale = row(pool_scale[j])
            xp, st = _pool_prompt(xp, g_mix, w_in, w_grp, scale, w_out, batch, seq, tp)
            pool_p.append(st)
            xs, u_s = _pool_sample(xs, state_pool[j].transpose(1, 0, 2), g_mix, w_in, w_grp, scale,
                                   w_out, n_t, past_len)
            pool_s.append(jnp.concatenate([state_pool[j][:, n_t:], to_batch_major(u_s)], axis=1))
        elif kind == 1:
            w_qkv, w_out = attn_w_qkv[j].astype(BF16), attn_w_out[j].astype(BF16)
            qp, kp, vp = _norm_matmul(xp, g_mix, w_qkv, 3, tp)
            qs, ks, vs = _norm_matmul(xs, g_mix, w_qkv, 3, _tile(n_s, 512))
            ap = _moba_prompt(qp, kp, vp, batch, seq, n_heads)
            a_s = _moba_sample(qs, ks, vs, cache_k, cache_v, j, page_table, n_t, n_heads)
            xp = _matmul_residual(ap, w_out, xp, tp)
            xs = _matmul_residual(a_s, w_out, xs, _tile(n_s, 512))
            k_p.append(kp.reshape(batch, seq, n_heads, hd))
            v_p.append(vp.reshape(batch, seq, n_heads, hd))
            k_s.append(to_batch_major(ks).reshape(dec_b, n_t, n_heads, hd))
            v_s.append(to_batch_major(vs).reshape(dec_b, n_t, n_heads, hd))
        else:
            w_in, w_out = gmlp_w_in[j].astype(BF16), gmlp_w_out[j].astype(BF16)
            ln_g, ln_b = row(gmlp_ln_g[j]), row(gmlp_ln_b[j])
            gc = w_out.shape[0] // GMLP_GROUPS
            bias = jnp.broadcast_to(gmlp_b_s[j].T[:, :, None], (GMLP_CHUNK, GMLP_GROUPS, gc))
            xp, vn_p = _gmlp_prompt(xp, g_mix, w_in, ln_g, ln_b, gmlp_w_s[j],
                                    bias.reshape(GMLP_CHUNK, GMLP_GROUPS * gc), w_out, tp)
            xs, vn_s = _gmlp_sample(xs, g_mix, w_in, ln_g, ln_b,
                                    gmlp_w_s[j][:, :n_t, :n_t].reshape(-1),
                                    gmlp_b_s[j][:, :n_t].reshape(-1), w_out, n_t)
            gv_p.append(vn_p.reshape(batch, seq, -1))
            gv_s.append(to_batch_major(vn_s))

        g_ffn = row(norm_ffn[i])
        ne = moe_re_w.shape[2]
        fill = jnp.zeros((d, ROUTE_LANES - ne - MOE_GROUPS), F32)
        rw = jnp.concatenate([moe_re_w[i], moe_rg_w[i], fill], axis=1).astype(BF16)
        rb = jnp.concatenate([moe_re_b[i], moe_rg_b[i], fill[0]]).reshape(1, ROUTE_LANES)
        g_fin = row(norm_final) if i == depth - 1 else None
        xp = _moe(xp, g_ffn, rw, rb, wg, wu, wd, i, g_fin, _tile(n_p, 1024))
        xs = _moe(xs, g_ffn, rw, rb, wg, wu, wd, i, g_fin, _tile(n_s, 512))

    y_p = xp.reshape(batch, seq, d)
    y_s = to_batch_major(xs)
    return (y_p, y_s, jnp.stack(pool_p), jnp.stack(pool_s), jnp.stack(k_p), jnp.stack(v_p),
            jnp.stack(k_s), jnp.stack(v_s), jnp.stack(gv_p), jnp.stack(gv_s))
```

```python
import functools

import jax
import jax.numpy as jnp
from jax import lax
from jax.experimental import pallas as pl
from jax.experimental.pallas import tpu as pltpu

F32 = jnp.float32
BF16 = jnp.bfloat16

RMS_EPS = 1e-6
LN_EPS = 1e-5
N_MIXERS = 3
POOL_WINDOWS = (2, 4, 8, 16)
POOL_STATE = max(POOL_WINDOWS) - 1
MOBA_BLOCK = 256
MOBA_TOPK = 3
GMLP_CHUNK = 128
GMLP_GROUPS = 8
MOE_GROUPS = 4
MOE_PER_GROUP = 4
PAGE_SIZE = 128
N_HEADS = 8

V7X_VMEM_LIMIT_BYTES = 56 * 1024 * 1024
NEG_BIG = -1e30
LOG2_E = 1.4426950408889634


def _params(*sem):
    return pltpu.CompilerParams(dimension_semantics=sem, vmem_limit_bytes=V7X_VMEM_LIMIT_BYTES)


def _rms(x, g):
    return x * lax.rsqrt(jnp.mean(x * x, axis=-1, keepdims=True) + RMS_EPS) * g


def _dot(a, b):
    return jnp.dot(a, b, preferred_element_type=F32)


def _dot_nt(a, b, precision=None):
    return lax.dot_general(a, b, (((1,), (1,)), ((), ())), preferred_element_type=F32,
                           precision=precision)


def _full(shape):
    return pl.BlockSpec(shape, lambda *_: (0,) * len(shape))


def _norm_matmul_kernel(x_ref, g_ref, w_ref, *out_refs):
    h = _rms(x_ref[...], g_ref[...]).astype(BF16)
    f = w_ref.shape[1] // len(out_refs)
    for i, o_ref in enumerate(out_refs):
        o_ref[...] = _dot(h, w_ref[:, i * f:(i + 1) * f])


def _norm_matmul(x, g, w, n_out, tm):
    n, d = x.shape
    f = w.shape[1] // n_out
    return pl.pallas_call(
        _norm_matmul_kernel,
        grid=(n // tm,),
        in_specs=[pl.BlockSpec((tm, d), lambda i: (i, 0)), _full((1, d)), _full(w.shape)],
        out_specs=[pl.BlockSpec((tm, f), lambda i: (i, 0))] * n_out,
        out_shape=[jax.ShapeDtypeStruct((n, f), F32)] * n_out,
        compiler_params=_params("parallel"),
        name="norm_matmul",
    )(x, g, w)


def _matmul_residual_kernel(a_ref, w_ref, x_ref, y_ref):
    y_ref[...] = x_ref[...] + _dot(a_ref[...].astype(BF16), w_ref[...])


def _matmul_residual(a, w, x, tm):
    n, d = x.shape
    k = a.shape[1]
    return pl.pallas_call(
        _matmul_residual_kernel,
        grid=(n // tm,),
        in_specs=[pl.BlockSpec((tm, k), lambda i: (i, 0)), _full(w.shape),
                  pl.BlockSpec((tm, d), lambda i: (i, 0))],
        out_specs=pl.BlockSpec((tm, d), lambda i: (i, 0)),
        out_shape=jax.ShapeDtypeStruct((n, d), F32),
        compiler_params=_params("parallel"),
        name="matmul_residual",
    )(a, w, x)


def _pool_tail(r, x, wgrp_ref, scale_ref, wout_ref):
    gc = wgrp_ref.shape[1]
    rb = r.astype(BF16)
    t = jnp.concatenate([_dot(rb[:, g * gc:(g + 1) * gc], wgrp_ref[g])
                         for g in range(len(POOL_WINDOWS))], axis=-1)
    t = (t * scale_ref[...]).astype(BF16)
    return x + _dot(t, wout_ref[...])


def _pool_prompt_kernel(x_ref, g_ref, win_ref, wgrp_ref, scale_ref, wout_ref,
                        y_ref, st_ref, carry_ref, *, ts, ns):
    s = pl.program_id(1)
    halo = carry_ref.shape[0]
    gc = wgrp_ref.shape[1]

    @pl.when(s == 0)
    def _():
        carry_ref[...] = jnp.zeros_like(carry_ref)

    x = x_ref[...]
    h = _rms(x, g_ref[...]).astype(BF16)
    u = _dot(h, win_ref[...])
    ext = jnp.concatenate([carry_ref[...], u], axis=0)
    pos = s * ts + lax.broadcasted_iota(jnp.int32, (ts, 1), 0)
    parts = []
    acc = ext
    width = 1
    for w in POOL_WINDOWS:
        while width < w:
            acc = acc + pltpu.roll(acc, width, axis=0)
            width *= 2
        assert width == w
        cnt = jnp.minimum(pos + 1, w).astype(F32)
        parts.append(acc[halo:, :gc] / cnt)
        acc = acc[:, gc:]
    r = jnp.concatenate(parts, axis=-1) - u
    y_ref[...] = _pool_tail(r, x, wgrp_ref, scale_ref, wout_ref)
    carry_ref[...] = u[ts - halo:, :]

    @pl.when(s == ns - 1)
    def _():
        st_ref[...] = carry_ref[halo - POOL_STATE:, :]


def _pool_prompt(x, g, w_in, w_grp, scale, w_out, batch, seq, ts):
    n, d = x.shape
    ns = seq // ts
    halo = 16
    kern = functools.partial(_pool_prompt_kernel, ts=ts, ns=ns)
    return pl.pallas_call(
        kern,
        grid=(batch, ns),
        in_specs=[pl.BlockSpec((ts, d), lambda b, s: (b * ns + s, 0)), _full((1, d)),
                  _full(w_in.shape), _full(w_grp.shape), _full((1, d)), _full(w_out.shape)],
        out_specs=[pl.BlockSpec((ts, d), lambda b, s: (b * ns + s, 0)),
                   pl.BlockSpec((None, POOL_STATE, d), lambda b, s: (b, 0, 0))],
        out_shape=[jax.ShapeDtypeStruct((n, d), F32),
                   jax.ShapeDtypeStruct((batch, POOL_STATE, d), F32)],
        scratch_shapes=[pltpu.VMEM((halo, d), F32)],
        compiler_params=_params("parallel", "arbitrary"),
        name="pool_prompt",
    )(x, g, w_in, w_grp, scale, w_out)


def _pool_sample_kernel(x_ref, pref_ref, g_ref, win_ref, wgrp_ref, scale_ref, wout_ref,
                        y_ref, u_ref, *, n_t, pos0):
    bd = x_ref.shape[0] // n_t
    gc = wgrp_ref.shape[1]
    x = x_ref[...]
    h = _rms(x, g_ref[...]).astype(BF16)
    u = _dot(h, win_ref[...])
    u_ref[...] = u
    us = [u[t * bd:(t + 1) * bd] for t in range(n_t)]
    tail = [jnp.zeros_like(us[0])]
    for m in range(1, POOL_STATE + 1):
        tail.append(tail[-1] + pref_ref[POOL_STATE - m])
    rows = []
    for t in range(n_t):
        parts = []
        for gi, w in enumerate(POOL_WINDOWS):
            sl = slice(gi * gc, (gi + 1) * gc)
            n_new = min(w, t + 1)
            wsum = us[t][:, sl]
            for k in range(1, n_new):
                wsum = wsum + us[t - k][:, sl]
            if w - n_new:
                wsum = wsum + tail[w - n_new][:, sl]
            parts.append(wsum / float(min(pos0 + t + 1, w)))
        rows.append(jnp.concatenate(parts, axis=-1) - us[t])
    r = jnp.concatenate(rows, axis=0)
    y_ref[...] = _pool_tail(r, x, wgrp_ref, scale_ref, wout_ref)


def _pool_sample(x, pref_t, g, w_in, w_grp, scale, w_out, n_t, pos0):
    n, d = x.shape
    kern = functools.partial(_pool_sample_kernel, n_t=n_t, pos0=pos0)
    return pl.pallas_call(
        kern,
        grid=(1,),
        in_specs=[_full(x.shape), _full(pref_t.shape), _full((1, d)), _full(w_in.shape),
                  _full(w_grp.shape), _full((1, d)), _full(w_out.shape)],
        out_specs=[_full((n, d)), _full((n, d))],
        out_shape=[jax.ShapeDtypeStruct((n, d), F32)] * 2,
        compiler_params=_params("arbitrary"),
        name="pool_sample",
    )(x, pref_t, g, w_in, w_grp, scale, w_out)


def _top_blocks(gate, n_valid, n_blocks):
    idx = lax.broadcasted_iota(jnp.int32, gate.shape, 0)
    valid = idx < n_valid
    rows = []
    for j in range(n_blocks):
        gj = gate[j:j + 1, :]
        beats = valid & ((gate > gj) | ((gate == gj) & (idx < j)))
        rank = jnp.sum(beats.astype(F32), axis=0, keepdims=True)
        rows.append(jnp.where(rank < MOBA_TOPK, jnp.where(j < n_valid, 1.0, 0.0), 0.0))
    return rows


def _moba_prompt_kernel(q_ref, k_ref, v_ref, o_ref, kb_ref, vt_ref, kmean_ref, sel_ref, s_ref,
                        *, nb, group, hps, hd):
    qi = pl.program_id(2)
    blk = MOBA_BLOCK
    scale = hd ** -0.5
    heads = range(hps)
    lanes = lambda h: slice(h * hd, (h + 1) * hd)

    @pl.when(qi == 0)
    def _():
        k = k_ref[...]
        kb_ref[...] = k.astype(BF16)
        kmean_ref[...] = jnp.mean(k.reshape(nb, blk, hps * hd), axis=1)
        vt_ref[...] = v_ref[...].T.astype(BF16)

    q = q_ref[...]
    qb = [q[:, lanes(h)].astype(BF16) for h in heads]
    for h in heads:
        gate = _dot_nt(kmean_ref[:, lanes(h)].astype(BF16), qb[h])
        for j, row in enumerate(_top_blocks(gate, qi, nb)):
            sel_ref[h * nb + j:h * nb + j + 1, :] = row
    key_i = lax.broadcasted_iota(jnp.int32, (blk, blk), 0)
    qry_i = lax.broadcasted_iota(jnp.int32, (blk, blk), 1)
    scale2 = scale * LOG2_E

    def scores(h, j, n):
        kj = kb_ref[pl.ds(pl.multiple_of(j * blk, blk), n * blk), lanes(h)]
        return _dot_nt(kj, qb[h]) * scale2

    def pv(h, j, n, p):
        vj = vt_ref[lanes(h), pl.ds(pl.multiple_of(j * blk, blk), n * blk)]
        return _dot(vj, p.astype(BF16))

    n_trips = qi // group + 1
    causal = jnp.where(key_i <= qry_i, 1.0, 0.0)

    def score_pass(g, ms):
        j = g * group
        out = []
        for h, m in zip(heads, ms):
            keep = jnp.concatenate(
                [jnp.where(j + r == qi, causal,
                           jnp.broadcast_to(sel_ref[pl.ds(h * nb + j + r, 1), :], (blk, blk)))
                 for r in range(group)], axis=0)
            s = jnp.where(keep > 0.0, scores(h, j, group), NEG_BIG)
            s_ref[h, pl.ds(pl.multiple_of(j * blk, blk), group * blk), :] = s
            out.append(jnp.maximum(m, jnp.max(s, axis=0, keepdims=True)))
        return tuple(out)

    ms = lax.fori_loop(0, n_trips, score_pass, tuple(jnp.full((1, blk), NEG_BIG, F32) for _ in heads))
    carry = [(jnp.zeros((1, blk), F32), jnp.zeros((hd, blk), F32)) for _ in heads]

    def value_pass(g, carry):
        j = g * group
        out = []
        for h, (l, acc) in zip(heads, carry):
            p = jnp.exp2(s_ref[h, pl.ds(pl.multiple_of(j * blk, blk), group * blk), :] - ms[h])
            out.append((l + jnp.sum(p, axis=0, keepdims=True), acc + pv(h, j, group, p)))
        return tuple(out)

    carry = lax.fori_loop(0, n_trips, value_pass, tuple(carry))
    for h, (l, acc) in zip(heads, carry):
        o_ref[:, lanes(h)] = (acc / l).T.astype(o_ref.dtype)


def _moba_prompt(q, k, v, batch, seq, n_heads):
    n, d = q.shape
    hd = d // n_heads
    nb = seq // MOBA_BLOCK
    group = 4 if nb % 4 == 0 else 1
    hps = 2 if n_heads % 2 == 0 else 1
    kern = functools.partial(_moba_prompt_kernel, nb=nb, group=group, hps=hps, hd=hd)
    return pl.pallas_call(
        kern,
        grid=(batch, n_heads // hps, nb),
        in_specs=[pl.BlockSpec((MOBA_BLOCK, hps * hd), lambda b, h, i: (b * nb + i, h)),
                  pl.BlockSpec((seq, hps * hd), lambda b, h, i: (b, h)),
                  pl.BlockSpec((seq, hps * hd), lambda b, h, i: (b, h))],
        out_specs=pl.BlockSpec((MOBA_BLOCK, hps * hd), lambda b, h, i: (b * nb + i, h)),
        out_shape=jax.ShapeDtypeStruct((n, d), BF16),
        scratch_shapes=[pltpu.VMEM((seq, hps * hd), BF16), pltpu.VMEM((hps * hd, seq), BF16),
                        pltpu.VMEM((nb, hps * hd), F32), pltpu.VMEM((hps * nb, MOBA_BLOCK), F32),
                        pltpu.VMEM((hps, nb * MOBA_BLOCK, MOBA_BLOCK), F32)],
        compiler_params=_params("parallel", "parallel", "arbitrary"),
        name="moba_prompt",
    )(q, k, v)


def _moba_sample_kernel(pt_ref, q_ref, kn_ref, vn_ref, *refs, n_t, n_pages, pps):
    del pt_ref
    kp_refs, vp_refs = refs[:pps], refs[pps:2 * pps]
    o_ref, wbd_ref, s_ref, vall_ref = refs[2 * pps:]
    p_i = pl.program_id(1)
    nh, hd = q_ref.shape[2], q_ref.shape[3]
    lanes = wbd_ref.shape[1]
    scale = hd ** -0.5
    n_past = n_pages // (MOBA_BLOCK // PAGE_SIZE)
    head_i = lax.broadcasted_iota(jnp.int32, (nh, hd), 0)

    @pl.when(p_i == 0)
    def _():
        rows = [jnp.concatenate([jnp.where(head_i == h, q_ref[t, 0], 0.0) for h in range(nh)], axis=-1)
                for t in range(n_t)]
        rows.append(jnp.zeros((lanes - n_t * nh, nh * hd), F32))
        wbd_ref[...] = jnp.concatenate(rows, axis=0).T.astype(BF16)

    for g, (kp_ref, vp_ref) in enumerate(zip(kp_refs, vp_refs)):
        off = pl.multiple_of((p_i * pps + g) * PAGE_SIZE, PAGE_SIZE)
        acc = None
        for h in range(nh):
            kh = kp_ref[pl.ds(h, PAGE_SIZE, stride=nh), :].astype(BF16)
            part = _dot(kh, wbd_ref[h * hd:(h + 1) * hd, :])
            acc = part if acc is None else acc + part
            vall_ref[pl.ds(off, PAGE_SIZE), h * hd:(h + 1) * hd] = (
                vp_ref[pl.ds(h, PAGE_SIZE, stride=nh), :].astype(BF16))
        s_ref[pl.ds(off, PAGE_SIZE), :] = acc

    @pl.when(p_i == n_pages // pps - 1)
    def _():
        s_all = s_ref[...]
        blocks = [s_all[j * MOBA_BLOCK:(j + 1) * MOBA_BLOCK] for j in range(n_past)]
        gate = [jnp.sum(b, axis=0, keepdims=True) * (1.0 / MOBA_BLOCK) for b in blocks]
        sel = []
        for j in range(n_past):
            rank = jnp.zeros_like(gate[j])
            for i in range(n_past):
                if i != j:
                    beats = (gate[i] >= gate[j]) if i < j else (gate[i] > gate[j])
                    rank = rank + beats.astype(F32)
            sel.append(rank < MOBA_TOPK)
        s_past = jnp.concatenate(
            [jnp.where(sel[j], blocks[j] * scale, NEG_BIG) for j in range(n_past)], axis=0)
        m_row = jnp.max(s_past, axis=0, keepdims=True)
        p_past = jnp.exp(s_past - m_row)
        l_row = jnp.sum(p_past, axis=0, keepdims=True)
        full = _dot(p_past.T.astype(BF16), vall_ref[...])
        stats = jnp.concatenate([m_row, l_row, jnp.zeros((lanes - 2, lanes), F32)], axis=0).T
        for t in range(n_t):
            rows = full[t * nh:(t + 1) * nh]
            out = functools.reduce(
                lambda a, b: a + b,
                [jnp.where(head_i == h, rows[:, h * hd:(h + 1) * hd], 0.0) for h in range(nh)])
            m_past = stats[t * nh:(t + 1) * nh, 0:1]
            l_past = stats[t * nh:(t + 1) * nh, 1:2]
            s_new = [jnp.sum(q_ref[t, 0] * kn_ref[t2, 0], axis=-1, keepdims=True) * scale
                     for t2 in range(t + 1)]
            m = functools.reduce(jnp.maximum, s_new, m_past)
            alpha = jnp.exp(m_past - m)
            out = alpha * out
            l = alpha * l_past
            for t2 in range(t + 1):
                p_new = jnp.exp(s_new[t2] - m)
                l = l + p_new
                out = out + p_new * vn_ref[t2, 0]
            o_ref[t, 0] = out / l


def _moba_sample(q, k, v, cache_k, cache_v, layer, page_table, n_t, n_heads):
    n, d = q.shape
    bd = n // n_t
    hd = d // n_heads
    n_pages = page_table.shape[1]
    past = n_pages * PAGE_SIZE
    lanes = 128
    assert n_pages % (MOBA_BLOCK // PAGE_SIZE) == 0 and n_t <= MOBA_BLOCK and n_t * n_heads <= lanes - 2
    q4, k4, v4 = (a.reshape(n_t, bd, n_heads, hd) for a in (q, k, v))
    n_layers, n_phys = cache_k.shape[:2]
    ck, cv = (c.reshape(n_layers, n_phys, PAGE_SIZE * n_heads, hd) for c in (cache_k, cache_v))
    new_spec = pl.BlockSpec((n_t, 1, n_heads, hd), lambda b, p, pt: (0, b, 0, 0))
    pps = 8 if n_pages % 8 == 0 else 2
    page_specs = [pl.BlockSpec((None, None, PAGE_SIZE * n_heads, hd),
                               functools.partial(lambda g, b, p, pt: (layer, pt[b, p * pps + g], 0, 0), g))
                  for g in range(pps)]
    kern = functools.partial(_moba_sample_kernel, n_t=n_t, n_pages=n_pages, pps=pps)
    out = pl.pallas_call(
        kern,
        grid_spec=pltpu.PrefetchScalarGridSpec(
            num_scalar_prefetch=1,
            grid=(bd, n_pages // pps),
            in_specs=[new_spec, new_spec, new_spec] + page_specs + page_specs,
            out_specs=new_spec,
            scratch_shapes=[pltpu.VMEM((d, lanes), BF16), pltpu.VMEM((past, lanes), F32),
                            pltpu.VMEM((past, d), BF16)],
        ),
        out_shape=jax.ShapeDtypeStruct((n_t, bd, n_heads, hd), F32),
        compiler_params=_params("parallel", "arbitrary"),
        name="moba_sample",
    )(page_table, q4, k4, v4, *([ck] * pps), *([cv] * pps))
    return out.reshape(n, d)


def _gelu(z):
    return 0.5 * z * (1.0 + jnp.tanh(0.7978845608028654 * (z + 0.044715 * z * z * z)))


def _gmlp_front(x, g_ref, win_ref, lng_ref, lnb_ref):
    h = _rms(x, g_ref[...]).astype(BF16)
    z = _gelu(_dot(h, win_ref[...]))
    dg = z.shape[1] // 2
    u, v = z[:, :dg], z[:, dg:]
    mu = jnp.mean(v, axis=-1, keepdims=True)
    var = jnp.mean(jnp.square(v - mu), axis=-1, keepdims=True)
    vn = (v - mu) * lax.rsqrt(var + LN_EPS) * lng_ref[...] + lnb_ref[...]
    return u, vn


def _gmlp_prompt_kernel(x_ref, g_ref, win_ref, lng_ref, lnb_ref, ws_ref, bias_ref, wout_ref,
                        y_ref, vn_ref, *, tm):
    x = x_ref[...]
    u, vn = _gmlp_front(x, g_ref, win_ref, lng_ref, lnb_ref)
    vn_ref[...] = vn
    ck = GMLP_CHUNK
    gc = vn.shape[1] // GMLP_GROUPS
    row_i = lax.broadcasted_iota(jnp.int32, (ck, ck), 0)
    col_i = lax.broadcasted_iota(jnp.int32, (ck, ck), 1)
    wc = [jnp.where(row_i >= col_i, ws_ref[g], 0.0).astype(BF16) for g in range(GMLP_GROUPS)]
    vb = vn.astype(BF16)
    mix = jnp.concatenate(
        [jnp.concatenate([_dot(wc[g], vb[c * ck:(c + 1) * ck, g * gc:(g + 1) * gc])
                          for g in range(GMLP_GROUPS)], axis=-1) + bias_ref[...]
         for c in range(tm // ck)], axis=0)
    y_ref[...] = x + _dot((u * mix).astype(BF16), wout_ref[...])


def _gmlp_prompt(x, g, w_in, ln_g, ln_b, w_s, bias, w_out, tm):
    n, d = x.shape
    dg = w_out.shape[0]
    kern = functools.partial(_gmlp_prompt_kernel, tm=tm)
    return pl.pallas_call(
        kern,
        grid=(n // tm,),
        in_specs=[pl.BlockSpec((tm, d), lambda i: (i, 0)), _full((1, d)), _full(w_in.shape),
                  _full((1, dg)), _full((1, dg)), _full(w_s.shape), _full(bias.shape),
                  _full(w_out.shape)],
        out_specs=[pl.BlockSpec((tm, d), lambda i: (i, 0)), pl.BlockSpec((tm, dg), lambda i: (i, 0))],
        out_shape=[jax.ShapeDtypeStruct((n, d), F32), jax.ShapeDtypeStruct((n, dg), F32)],
        compiler_params=_params("parallel"),
        name="gmlp_prompt",
    )(x, g, w_in, ln_g, ln_b, w_s, bias, w_out)


def _gmlp_sample_kernel(ws_ref, bs_ref, x_ref, g_ref, win_ref, lng_ref, lnb_ref, wout_ref,
                        y_ref, vn_ref, *, n_t):
    x = x_ref[...]
    u, vn = _gmlp_front(x, g_ref, win_ref, lng_ref, lnb_ref)
    vn_ref[...] = vn
    bd = x.shape[0] // n_t
    gc = vn.shape[1] // GMLP_GROUPS
    rows = []
    for t in range(n_t):
        parts = []
        for g in range(GMLP_GROUPS):
            sl = slice(g * gc, (g + 1) * gc)
            acc = jnp.full((bd, gc), bs_ref[g * n_t + t], F32)
            for s in range(t + 1):
                acc = acc + ws_ref[(g * n_t + t) * n_t + s] * vn[s * bd:(s + 1) * bd, sl]
            parts.append(acc)
        rows.append(jnp.concatenate(parts, axis=-1))
    mix = jnp.concatenate(rows, axis=0)
    y_ref[...] = x + _dot((u * mix).astype(BF16), wout_ref[...])


def _gmlp_sample(x, g, w_in, ln_g, ln_b, ws_flat, bs_flat, w_out, n_t):
    n, d = x.shape
    dg = w_out.shape[0]
    kern = functools.partial(_gmlp_sample_kernel, n_t=n_t)
    smem = pl.BlockSpec(memory_space=pltpu.SMEM)
    return pl.pallas_call(
        kern,
        grid=(1,),
        in_specs=[smem, smem, _full(x.shape), _full((1, d)), _full(w_in.shape), _full((1, dg)),
                  _full((1, dg)), _full(w_out.shape)],
        out_specs=[_full((n, d)), _full((n, dg))],
        out_shape=[jax.ShapeDtypeStruct((n, d), F32), jax.ShapeDtypeStruct((n, dg), F32)],
        compiler_params=_params("arbitrary"),
        name="gmlp_sample",
    )(ws_flat, bs_flat, x, g, w_in, ln_g, ln_b, w_out)


ROUTE_LANES = 128
SLOT_ALIGN = 16
MOE_SUB = 256
ROUTE_ROWS = 32


def _router_kernel(x_ref, g_ref, rw_ref, rb_ref, tri_ref, sel_ref, selt_ref,
                   hn_ref, col_ref, row_ref, meta_ref):
    ne = MOE_GROUPS * MOE_PER_GROUP
    hb = _rms(x_ref[...], g_ref[...]).astype(BF16)
    hn_ref[...] = hb
    lt = (_dot(hb, rw_ref[...]) + rb_ref[...]).T[:ROUTE_ROWS]
    tm = lt.shape[1]
    ridx = lax.broadcasted_iota(jnp.int32, lt.shape, 0)
    lg = jnp.where((ridx >= ne) & (ridx < ne + MOE_GROUPS), lt, -jnp.inf)
    g_max = jnp.max(lg, axis=0, keepdims=True)
    g_idx = jnp.min(jnp.where(lg == g_max, ridx, ROUTE_ROWS), axis=0, keepdims=True) - ne
    g_p = 1.0 / jnp.sum(jnp.exp(lg - g_max), axis=0, keepdims=True)
    first = g_idx * MOE_PER_GROUP
    l1 = jnp.where((ridx >= first) & (ridx < first + MOE_PER_GROUP), lt, -jnp.inf)
    m1 = jnp.max(l1, axis=0, keepdims=True)
    e1 = jnp.min(jnp.where(l1 == m1, ridx, ROUTE_ROWS), axis=0, keepdims=True)
    l2 = jnp.where(ridx == e1, -jnp.inf, l1)
    m2 = jnp.max(l2, axis=0, keepdims=True)
    e2 = jnp.min(jnp.where(l2 == m2, ridx, ROUTE_ROWS), axis=0, keepdims=True)
    z = jnp.sum(jnp.exp(l1 - m1), axis=0, keepdims=True)
    p1 = 1.0 / z
    p2 = jnp.exp(m2 - m1) / z
    w1 = g_p * p1 / (p1 + p2)
    w2 = g_p * p2 / (p1 + p2)
    hit1 = ridx == e1
    hit2 = ridx == e2
    routed = (jnp.where(hit1, 1.0, 0.0) + jnp.where(hit2, 1.0, 0.0)).astype(BF16)
    rank = _dot(routed, tri_ref[...])
    cnt = _dot(routed, sel_ref[...])
    pad = jnp.floor((cnt + (SLOT_ALIGN - 1)) * (1.0 / SLOT_ALIGN)) * SLOT_ALIGN
    ridx_c = lax.broadcasted_iota(jnp.int32, cnt.shape, 0)
    incl = pad
    shift = 1
    while shift < ne:
        incl = incl + jnp.where(ridx_c >= shift, pltpu.roll(incl, shift, axis=0), 0.0)
        shift *= 2
    off = incl - pad
    slot = rank + _dot(off.astype(BF16), selt_ref[...])
    s1 = jnp.sum(jnp.where(hit1, slot, 0.0), axis=0, keepdims=True)
    s2 = jnp.sum(jnp.where(hit2, slot, 0.0), axis=0, keepdims=True)
    rows = jnp.concatenate([s1, s2, w1, w2, jnp.zeros((row_ref.shape[0] - 4, tm), F32)], axis=0)
    row_ref[...] = rows
    col_ref[...] = jnp.concatenate([rows, jnp.zeros((ROUTE_LANES - rows.shape[0], tm), F32)], axis=0).T
    meta_ref[0] = cnt.astype(jnp.int32)
    meta_ref[1] = off.astype(jnp.int32)


def _router(x, g, rw, rb, tm, sub):
    n, d = x.shape
    nt = n // tm
    assert (2 * sub) // SLOT_ALIGN + MOE_GROUPS * MOE_PER_GROUP < 256
    tok = jnp.arange(tm)
    blocks = tok // sub
    tri = ((blocks[:, None] == blocks[None, :]) & (tok[:, None] < tok[None, :])).astype(BF16)
    sel = (blocks[:, None] == jnp.arange(ROUTE_LANES)[None, :]).astype(BF16)
    return pl.pallas_call(
        _router_kernel,
        grid=(nt,),
        in_specs=[pl.BlockSpec((tm, d), lambda i: (i, 0)), _full((1, d)), _full(rw.shape),
                  _full(rb.shape), _full(tri.shape), _full(sel.shape), _full(sel.T.shape)],
        out_specs=[pl.BlockSpec((tm, d), lambda i: (i, 0)),
                   pl.BlockSpec((tm, ROUTE_LANES), lambda i: (i, 0)),
                   pl.BlockSpec((8, tm), lambda i: (0, i)),
                   pl.BlockSpec((None, 2, ROUTE_ROWS, ROUTE_LANES), lambda i: (i, 0, 0, 0))],
        out_shape=[jax.ShapeDtypeStruct((n, d), BF16), jax.ShapeDtypeStruct((n, ROUTE_LANES), F32),
                   jax.ShapeDtypeStruct((8, n), F32),
                   jax.ShapeDtypeStruct((nt, 2, ROUTE_ROWS, ROUTE_LANES), jnp.int32)],
        compiler_params=_params("arbitrary"),
        name="moe_router",
    )(x, g, rw, rb, tri, sel, sel.T)


def _moe_routed_kernel(cnt_ref, off_ref, trips_ref, hn_ref, col_ref, row_ref, x_ref, gfin_ref,
                       wg_ref, wu_ref, wd_ref, y_ref, xs_ref, ys_ref, gs_ref,
                       *, sub, mcs, n_gather, n_scatter, final_norm):
    i = pl.program_id(0)
    eg = pl.program_id(1)
    eps = wg_ref.shape[0]
    ne = pl.num_programs(1) * eps
    n_sub = xs_ref.shape[0]
    dump = n_gather
    rows_of = lambda s: slice(s * sub, (s + 1) * sub)

    @pl.when(eg == 0)
    def _():
        y_ref[...] = x_ref[...]
        ys_ref[...] = jnp.zeros_like(ys_ref)
        sid = lax.broadcasted_iota(jnp.int32, (n_gather, sub), 0).astype(F32)
        for s in range(n_sub):
            s1, s2, w1, w2 = (row_ref[r:r + 1, rows_of(s)] for r in range(4))
            h1 = s1 == sid
            h2 = s2 == sid
            onehot = (jnp.where(h1, 1.0, 0.0) + jnp.where(h2, 1.0, 0.0)).astype(BF16)
            xs_ref[s, 0:n_gather, :] = _dot(onehot, hn_ref[rows_of(s), :]).astype(BF16)
            xs_ref[s, dump:dump + mcs, :] = jnp.zeros((mcs, xs_ref.shape[2]), BF16)
            gs_ref[s, 0:n_gather, :] = jnp.sum(
                jnp.where(h1, w1, 0.0) + jnp.where(h2, w2, 0.0), axis=-1, keepdims=True)
            gs_ref[s, dump:dump + mcs, :] = jnp.zeros((mcs, 1), F32)

    for r in range(eps):
        e = eg * eps + r
        counts = [cnt_ref[(i * n_sub + s) * ne + e] for s in range(n_sub)]
        firsts = [off_ref[(i * n_sub + s) * ne + e] for s in range(n_sub)]

        def chunk(k, carry, r=r, counts=counts, firsts=firsts):
            rows = [pl.multiple_of(jnp.where(k * mcs < counts[s], firsts[s] + k * mcs, dump), SLOT_ALIGN)
                    for s in range(n_sub)]
            xg = jnp.concatenate([xs_ref[s, pl.ds(rows[s], mcs), :] for s in range(n_sub)], axis=0)
            gate = jnp.concatenate([gs_ref[s, pl.ds(rows[s], mcs), :] for s in range(n_sub)], axis=0)
            a = _dot(xg, wg_ref[r])
            b = _dot(xg, wu_ref[r])
            hid = ((a * jax.nn.sigmoid(a)) * b * gate).astype(BF16)
            out = _dot(hid, wd_ref[r]).astype(BF16)
            for s in range(n_sub):
                ys_ref[s, pl.ds(rows[s], mcs), :] = out[s * mcs:(s + 1) * mcs]
            return carry

        lax.fori_loop(0, trips_ref[i * ne + e], chunk, 0)

    @pl.when(eg == pl.num_programs(1) - 1)
    def _():
        sid = lax.broadcasted_iota(jnp.int32, (sub, n_scatter), 1).astype(F32)
        for s in range(n_sub):
            onehot = (jnp.where(col_ref[rows_of(s), 0:1] == sid, 1.0, 0.0)
                      + jnp.where(col_ref[rows_of(s), 1:2] == sid, 1.0, 0.0)).astype(BF16)
            out = y_ref[rows_of(s), :] + _dot(onehot, ys_ref[s, 0:n_scatter, :])
            y_ref[rows_of(s), :] = _rms(out, gfin_ref[...]) if final_norm else out


def _round_up(a, b):
    return -(-a // b) * b


def _moe_routed(cnt, off, hn, col, row, wg, wu, wd, layer, x, g_final, tm, sub):
    n, d = x.shape
    _, ne, _, de = wg.shape
    n_sub = tm // sub
    mcs = _round_up(2 * sub * 3 // (ne * 2), SLOT_ALIGN)
    most = 2 * sub + (ne - 1) * SLOT_ALIGN
    n_scatter = _round_up(most, 128)
    n_gather = _round_up(most + mcs, 128)
    trips = jnp.max((cnt + mcs - 1) // mcs, axis=1).reshape(-1)
    eps = 2 if ne % 2 == 0 else 1
    final_norm = g_final is not None
    if g_final is None:
        g_final = jnp.ones((1, d), F32)
    kern = functools.partial(_moe_routed_kernel, sub=sub, mcs=mcs, n_gather=n_gather, n_scatter=n_scatter,
                             final_norm=final_norm)
    tile = lambda i, e, *_: (i, 0)
    expert = lambda i, e, *_: (layer, e, 0, 0)
    return pl.pallas_call(
        kern,
        grid_spec=pltpu.PrefetchScalarGridSpec(
            num_scalar_prefetch=3,
            grid=(n // tm, ne // eps),
            in_specs=[pl.BlockSpec((tm, d), tile), pl.BlockSpec((tm, ROUTE_LANES), tile),
                      pl.BlockSpec((8, tm), lambda i, e, *_: (0, i)), pl.BlockSpec((tm, d), tile),
                      pl.BlockSpec((1, d), lambda i, e, *_: (0, 0)),
                      pl.BlockSpec((None, eps, d, de), expert), pl.BlockSpec((None, eps, d, de), expert),
                      pl.BlockSpec((None, eps, de, d), expert)],
            out_specs=pl.BlockSpec((tm, d), tile),
            scratch_shapes=[pltpu.VMEM((n_sub, n_gather + mcs, d), BF16),
                            pltpu.VMEM((n_sub, n_gather + mcs, d), BF16),
                            pltpu.VMEM((n_sub, n_gather + mcs, 1), F32)],
        ),
        out_shape=jax.ShapeDtypeStruct((n, d), F32),
        compiler_params=_params("arbitrary", "arbitrary"),
        name="moe_routed",
    )(cnt.reshape(-1), off.reshape(-1), trips, hn, col, row, x, g_final, wg, wu, wd)


def _moe(x, g, rw, rb, wg, wu, wd, layer, g_final, tm):
    ne = wg.shape[1]
    sub = min(tm, MOE_SUB)
    n_sub = tm // sub
    assert tm % sub == 0 and n_sub <= ROUTE_LANES
    hn, col, row, meta = _router(x, g, rw, rb, tm, sub)
    cnt, off = (meta[:, r, :ne, :n_sub].transpose(0, 2, 1) for r in range(2))
    return _moe_routed(cnt, off, hn, col, row, wg, wu, wd, layer, x, g_final, tm, sub)


def _tile(n, pref):
    t = min(n, pref)
    assert n % t == 0, (n, t)
    return t


def kernel(x_prompt, x_sample, state_pool, cache_k, cache_v, page_table, norm_mix, norm_ffn, norm_final, pool_w_in, pool_w_grp, pool_scale, pool_w_out, attn_w_qkv, attn_w_out, gmlp_w_in, gmlp_ln_g, gmlp_ln_b, gmlp_w_s, gmlp_b_s, gmlp_w_out, moe_rg_w, moe_rg_b, moe_re_w, moe_re_b, moe_w_gate, moe_w_up, moe_w_down):
    batch, seq, d = x_prompt.shape
    dec_b, n_t, _ = x_sample.shape
    depth = norm_mix.shape[0]
    n_heads = N_HEADS
    hd = d // n_heads
    n_pages = page_table.shape[1]
    past_len = n_pages * PAGE_SIZE
    assert seq % MOBA_BLOCK == 0 and seq % GMLP_CHUNK == 0 and n_t <= GMLP_CHUNK
    assert past_len % GMLP_CHUNK == 0 and past_len >= POOL_STATE

    n_p, n_s = batch * seq, dec_b * n_t
    xp = x_prompt.reshape(n_p, d)
    xs = x_sample.transpose(1, 0, 2).reshape(n_s, d)
    tp = _tile(seq, 512)
    row = lambda a: a.reshape(1, -1)

    def to_batch_major(a):
        return a.reshape((n_t, dec_b) + a.shape[1:]).swapaxes(0, 1)

    wg, wu, wd = (a.astype(BF16) for a in (moe_w_gate, moe_w_up, moe_w_down))
    pool_p, pool_s, k_p, v_p, k_s, v_s, gv_p, gv_s = [], [], [], [], [], [], [], []
    for i in range(depth):
        kind, j = i % N_MIXERS, i // N_MIXERS
        g_mix = row(norm_mix[i])
        if kind == 0:
            w_in, w_grp, w_out = (a.astype(BF16) for a in (pool_w_in[j], pool_w_grp[j], pool_w_out[j]))
            scale = row(pool_scale[j])
            xp, st = _pool_prompt(xp, g_mix, w_in, w_grp, scale, w_out, batch, seq, tp)
            pool_p.append(st)
            xs, u_s = _pool_sample(xs, state_pool[j].transpose(1, 0, 2), g_mix, w_in, w_grp, scale,
                                   w_out, n_t, past_len)
            pool_s.append(jnp.concatenate([state_pool[j][:, n_t:], to_batch_major(u_s)], axis=1))
        elif kind == 1:
            w_qkv, w_out = attn_w_qkv[j].astype(BF16), attn_w_out[j].astype(BF16)
            qp, kp, vp = _norm_matmul(xp, g_mix, w_qkv, 3, tp)
            qs, ks, vs = _norm_matmul(xs, g_mix, w_qkv, 3, _tile(n_s, 512))
            ap = _moba_prompt(qp, kp, vp, batch, seq, n_heads)
            a_s = _moba_sample(qs, ks, vs, cache_k, cache_v, j, page_table, n_t, n_heads)
            xp = _matmul_residual(ap, w_out, xp, tp)
            xs = _matmul_residual(a_s, w_out, xs, _tile(n_s, 512))
            k_p.append(kp.reshape(batch, seq, n_heads, hd))
            v_p.append(vp.reshape(batch, seq, n_heads, hd))
            k_s.append(to_batch_major(ks).reshape(dec_b, n_t, n_heads, hd))
            v_s.append(to_batch_major(vs).reshape(dec_b, n_t, n_heads, hd))
        else:
            w_in, w_out = gmlp_w_in[j].astype(BF16), gmlp_w_out[j].astype(BF16)
            ln_g, ln_b = row(gmlp_ln_g[j]), row(gmlp_ln_b[j])
            gc = w_out.shape[0] // GMLP_GROUPS
            bias = jnp.broadcast_to(gmlp_b_s[j].T[:, :, None], (GMLP_CHUNK, GMLP_GROUPS, gc))
            xp, vn_p = _gmlp_prompt(xp, g_mix, w_in, ln_g, ln_b, gmlp_w_s[j],
                                    bias.reshape(GMLP_CHUNK, GMLP_GROUPS * gc), w_out, tp)
            xs, vn_s = _gmlp_sample(xs, g_mix, w_in, ln_g, ln_b,
                                    gmlp_w_s[j][:, :n_t, :n_t].reshape(-1),
                                    gmlp_b_s[j][:, :n_t].reshape(-1), w_out, n_t)
            gv_p.append(vn_p.reshape(batch, seq, -1))
            gv_s.append(to_batch_major(vn_s))

        g_ffn = row(norm_ffn[i])
        ne = moe_re_w.shape[2]
        fill = jnp.zeros((d, ROUTE_LANES - ne - MOE_GROUPS), F32)
        rw = jnp.concatenate([moe_re_w[i], moe_rg_w[i], fill], axis=1).astype(BF16)
        rb = jnp.concatenate([moe_re_b[i], moe_rg_b[i], fill[0]]).reshape(1, ROUTE_LANES)
        g_fin = row(norm_final) if i == depth - 1 else None
        xp = _moe(xp, g_ffn, rw, rb, wg, wu, wd, i, g_fin, _tile(n_p, 1024))
        xs = _moe(xs, g_ffn, rw, rb, wg, wu, wd, i, g_fin, _tile(n_s, 512))

    y_p = xp.reshape(batch, seq, d)
    y_s = to_batch_major(xs)
    return (y_p, y_s, jnp.stack(pool_p), jnp.stack(pool_s), jnp.stack(k_p), jnp.stack(v_p),
            jnp.stack(k_s), jnp.stack(v_s), jnp.stack(gv_p), jnp.stack(gv_s))
```

```python
import functools

import jax
import jax.numpy as jnp
from jax import lax
from jax.experimental import pallas as pl
from jax.experimental.pallas import tpu as pltpu

F32 = jnp.float32
BF16 = jnp.bfloat16

RMS_EPS = 1e-6
LN_EPS = 1e-5
N_MIXERS = 3
POOL_WINDOWS = (2, 4, 8, 16)
POOL_STATE = max(POOL_WINDOWS) - 1
MOBA_BLOCK = 256
MOBA_TOPK = 3
GMLP_CHUNK = 128
GMLP_GROUPS = 8
MOE_GROUPS = 4
MOE_PER_GROUP = 4
PAGE_SIZE = 128
N_HEADS = 8

V7X_VMEM_LIMIT_BYTES = 56 * 1024 * 1024
NEG_BIG = -1e30
LOG2_E = 1.4426950408889634


def _params(*sem):
    return pltpu.CompilerParams(dimension_semantics=sem, vmem_limit_bytes=V7X_VMEM_LIMIT_BYTES)


def _rms(x, g):
    return x * lax.rsqrt(jnp.mean(x * x, axis=-1, keepdims=True) + RMS_EPS) * g


def _dot(a, b):
    return jnp.dot(a, b, preferred_element_type=F32)


def _dot_nt(a, b, precision=None):
    return lax.dot_general(a, b, (((1,), (1,)), ((), ())), preferred_element_type=F32,
                           precision=precision)


def _full(shape):
    return pl.BlockSpec(shape, lambda *_: (0,) * len(shape))


def _norm_matmul_kernel(x_ref, g_ref, w_ref, *out_refs):
    h = _rms(x_ref[...], g_ref[...]).astype(BF16)
    f = w_ref.shape[1] // len(out_refs)
    for i, o_ref in enumerate(out_refs):
        o_ref[...] = _dot(h, w_ref[:, i * f:(i + 1) * f])


def _norm_matmul(x, g, w, n_out, tm):
    n, d = x.shape
    f = w.shape[1] // n_out
    return pl.pallas_call(
        _norm_matmul_kernel,
        grid=(n // tm,),
        in_specs=[pl.BlockSpec((tm, d), lambda i: (i, 0)), _full((1, d)), _full(w.shape)],
        out_specs=[pl.BlockSpec((tm, f), lambda i: (i, 0))] * n_out,
        out_shape=[jax.ShapeDtypeStruct((n, f), F32)] * n_out,
        compiler_params=_params("parallel"),
        name="norm_matmul",
    )(x, g, w)


def _matmul_residual_kernel(a_ref, w_ref, x_ref, y_ref):
    y_ref[...] = x_ref[...] + _dot(a_ref[...].astype(BF16), w_ref[...])


def _matmul_residual(a, w, x, tm):
    n, d = x.shape
    k = a.shape[1]
    return pl.pallas_call(
        _matmul_residual_kernel,
        grid=(n // tm,),
        in_specs=[pl.BlockSpec((tm, k), lambda i: (i, 0)), _full(w.shape),
                  pl.BlockSpec((tm, d), lambda i: (i, 0))],
        out_specs=pl.BlockSpec((tm, d), lambda i: (i, 0)),
        out_shape=jax.ShapeDtypeStruct((n, d), F32),
        compiler_params=_params("parallel"),
        name="matmul_residual",
    )(a, w, x)


def _pool_tail(r, x, wgrp_ref, scale_ref, wout_ref):
    gc = wgrp_ref.shape[1]
    rb = r.astype(BF16)
    t = jnp.concatenate([_dot(rb[:, g * gc:(g + 1) * gc], wgrp_ref[g])
                         for g in range(len(POOL_WINDOWS))], axis=-1)
    t = (t * scale_ref[...]).astype(BF16)
    return x + _dot(t, wout_ref[...])


def _pool_prompt_kernel(x_ref, g_ref, win_ref, wgrp_ref, scale_ref, wout_ref,
                        y_ref, st_ref, carry_ref, *, ts, ns):
    s = pl.program_id(1)
    halo = carry_ref.shape[0]
    gc = wgrp_ref.shape[1]

    @pl.when(s == 0)
    def _():
        carry_ref[...] = jnp.zeros_like(carry_ref)

    x = x_ref[...]
    h = _rms(x, g_ref[...]).astype(BF16)
    u = _dot(h, win_ref[...])
    ext = jnp.concatenate([carry_ref[...], u], axis=0)
    pos = s * ts + lax.broadcasted_iota(jnp.int32, (ts, 1), 0)
    parts = []
    acc = ext
    width = 1
    for w in POOL_WINDOWS:
        while width < w:
            acc = acc + pltpu.roll(acc, width, axis=0)
            width *= 2
        assert width == w
        cnt = jnp.minimum(pos + 1, w).astype(F32)
        parts.append(acc[halo:, :gc] / cnt)
        acc = acc[:, gc:]
    r = jnp.concatenate(parts, axis=-1) - u
    y_ref[...] = _pool_tail(r, x, wgrp_ref, scale_ref, wout_ref)
    carry_ref[...] = u[ts - halo:, :]

    @pl.when(s == ns - 1)
    def _():
        st_ref[...] = carry_ref[halo - POOL_STATE:, :]


def _pool_prompt(x, g, w_in, w_grp, scale, w_out, batch, seq, ts):
    n, d = x.shape
    ns = seq // ts
    halo = 16
    kern = functools.partial(_pool_prompt_kernel, ts=ts, ns=ns)
    return pl.pallas_call(
        kern,
        grid=(batch, ns),
        in_specs=[pl.BlockSpec((ts, d), lambda b, s: (b * ns + s, 0)), _full((1, d)),
                  _full(w_in.shape), _full(w_grp.shape), _full((1, d)), _full(w_out.shape)],
        out_specs=[pl.BlockSpec((ts, d), lambda b, s: (b * ns + s, 0)),
                   pl.BlockSpec((None, POOL_STATE, d), lambda b, s: (b, 0, 0))],
        out_shape=[jax.ShapeDtypeStruct((n, d), F32),
                   jax.ShapeDtypeStruct((batch, POOL_STATE, d), F32)],
        scratch_shapes=[pltpu.VMEM((halo, d), F32)],
        compiler_params=_params("parallel", "arbitrary"),
        name="pool_prompt",
    )(x, g, w_in, w_grp, scale, w_out)


def _pool_sample_kernel(x_ref, pref_ref, g_ref, win_ref, wgrp_ref, scale_ref, wout_ref,
                        y_ref, u_ref, *, n_t, pos0):
    bd = x_ref.shape[0] // n_t
    gc = wgrp_ref.shape[1]
    x = x_ref[...]
    h = _rms(x, g_ref[...]).astype(BF16)
    u = _dot(h, win_ref[...])
    u_ref[...] = u
    us = [u[t * bd:(t + 1) * bd] for t in range(n_t)]
    tail = [jnp.zeros_like(us[0])]
    for m in range(1, POOL_STATE + 1):
        tail.append(tail[-1] + pref_ref[POOL_STATE - m])
    rows = []
    for t in range(n_t):
        parts = []
        for gi, w in enumerate(POOL_WINDOWS):
            sl = slice(gi * gc, (gi + 1) * gc)
            n_new = min(w, t + 1)
            wsum = us[t][:, sl]
            for k in range(1, n_new):
                wsum = wsum + us[t - k][:, sl]
            if w - n_new:
                wsum = wsum + tail[w - n_new][:, sl]
            parts.append(wsum / float(min(pos0 + t + 1, w)))
        rows.append(jnp.concatenate(parts, axis=-1) - us[t])
    r = jnp.concatenate(rows, axis=0)
    y_ref[...] = _pool_tail(r, x, wgrp_ref, scale_ref, wout_ref)


def _pool_sample(x, pref_t, g, w_in, w_grp, scale, w_out, n_t, pos0):
    n, d = x.shape
    kern = functools.partial(_pool_sample_kernel, n_t=n_t, pos0=pos0)
    return pl.pallas_call(
        kern,
        grid=(1,),
        in_specs=[_full(x.shape), _full(pref_t.shape), _full((1, d)), _full(w_in.shape),
                  _full(w_grp.shape), _full((1, d)), _full(w_out.shape)],
        out_specs=[_full((n, d)), _full((n, d))],
        out_shape=[jax.ShapeDtypeStruct((n, d), F32)] * 2,
        compiler_params=_params("arbitrary"),
        name="pool_sample",
    )(x, pref_t, g, w_in, w_grp, scale, w_out)


def _top_blocks(gate, n_valid, n_blocks):
    idx = lax.broadcasted_iota(jnp.int32, gate.shape, 0)
    valid = idx < n_valid
    rows = []
    for j in range(n_blocks):
        gj = gate[j:j + 1, :]
        beats = valid & ((gate > gj) | ((gate == gj) & (idx < j)))
        rank = jnp.sum(beats.astype(F32), axis=0, keepdims=True)
        rows.append(jnp.where(rank < MOBA_TOPK, jnp.where(j < n_valid, 1.0, 0.0), 0.0))
    return rows


def _moba_prompt_kernel(q_ref, k_ref, v_ref, o_ref, kb_ref, vt_ref, kmean_ref, sel_ref, s_ref,
                        *, nb, group, hps, hd):
    qi = pl.program_id(2)
    blk = MOBA_BLOCK
    scale = hd ** -0.5
    heads = range(hps)
    lanes = lambda h: slice(h * hd, (h + 1) * hd)

    @pl.when(qi == 0)
    def _():
        k = k_ref[...]
        kb_ref[...] = k.astype(BF16)
        kmean_ref[...] = jnp.mean(k.reshape(nb, blk, hps * hd), axis=1)
        vt_ref[...] = v_ref[...].T.astype(BF16)

    q = q_ref[...]
    qb = [q[:, lanes(h)].astype(BF16) for h in heads]
    for h in heads:
        gate = _dot_nt(kmean_ref[:, lanes(h)].astype(BF16), qb[h])
        for j, row in enumerate(_top_blocks(gate, qi, nb)):
            sel_ref[h * nb + j:h * nb + j + 1, :] = row
    key_i = lax.broadcasted_iota(jnp.int32, (blk, blk), 0)
    qry_i = lax.broadcasted_iota(jnp.int32, (blk, blk), 1)
    scale2 = scale * LOG2_E

    def scores(h, j, n):
        kj = kb_ref[pl.ds(pl.multiple_of(j * blk, blk), n * blk), lanes(h)]
        return _dot_nt(kj, qb[h]) * scale2

    def pv(h, j, n, p):
        vj = vt_ref[lanes(h), pl.ds(pl.multiple_of(j * blk, blk), n * blk)]
        return _dot(vj, p.astype(BF16))

    n_trips = qi // group + 1
    causal = jnp.where(key_i <= qry_i, 1.0, 0.0)

    def score_pass(g, ms):
        j = g * group
        out = []
        for h, m in zip(heads, ms):
            keep = jnp.concatenate(
                [jnp.where(j + r == qi, causal,
                           jnp.broadcast_to(sel_ref[pl.ds(h * nb + j + r, 1), :], (blk, blk)))
                 for r in range(group)], axis=0)
            s = jnp.where(keep > 0.0, scores(h, j, group), NEG_BIG)
            s_ref[h, pl.ds(pl.multiple_of(j * blk, blk), group * blk), :] = s
            out.append(jnp.maximum(m, jnp.max(s, axis=0, keepdims=True)))
        return tuple(out)

    ms = lax.fori_loop(0, n_trips, score_pass, tuple(jnp.full((1, blk), NEG_BIG, F32) for _ in heads))
    carry = [(jnp.zeros((1, blk), F32), jnp.zeros((hd, blk), F32)) for _ in heads]

    def value_pass(g, carry):
        j = g * group
        out = []
        for h, (l, acc) in zip(heads, carry):
            p = jnp.exp2(s_ref[h, pl.ds(pl.multiple_of(j * blk, blk), group * blk), :] - ms[h])
            out.append((l + jnp.sum(p, axis=0, keepdims=True), acc + pv(h, j, group, p)))
        return tuple(out)

    carry = lax.fori_loop(0, n_trips, value_pass, tuple(carry))
    for h, (l, acc) in zip(heads, carry):
        o_ref[:, lanes(h)] = (acc / l).T.astype(o_ref.dtype)


def _moba_prompt(q, k, v, batch, seq, n_heads):
    n, d = q.shape
    hd = d // n_heads
    nb = seq // MOBA_BLOCK
    group = 4 if nb % 4 == 0 else 1
    hps = 2 if n_heads % 2 == 0 else 1
    kern = functools.partial(_moba_prompt_kernel, nb=nb, group=group, hps=hps, hd=hd)
    return pl.pallas_call(
        kern,
        grid=(batch, n_heads // hps, nb),
        in_specs=[pl.BlockSpec((MOBA_BLOCK, hps * hd), lambda b, h, i: (b * nb + i, h)),
                  pl.BlockSpec((seq, hps * hd), lambda b, h, i: (b, h)),
                  pl.BlockSpec((seq, hps * hd), lambda b, h, i: (b, h))],
        out_specs=pl.BlockSpec((MOBA_BLOCK, hps * hd), lambda b, h, i: (b * nb + i, h)),
        out_shape=jax.ShapeDtypeStruct((n, d), BF16),
        scratch_shapes=[pltpu.VMEM((seq, hps * hd), BF16), pltpu.VMEM((hps * hd, seq), BF16),
                        pltpu.VMEM((nb, hps * hd), F32), pltpu.VMEM((hps * nb, MOBA_BLOCK), F32),
                        pltpu.VMEM((hps, nb * MOBA_BLOCK, MOBA_BLOCK), F32)],
        compiler_params=_params("parallel", "parallel", "arbitrary"),
        name="moba_prompt",
    )(q, k, v)


def _moba_sample_kernel(pt_ref, q_ref, kn_ref, vn_ref, *refs, n_t, n_pages, pps):
    del pt_ref
    kp_refs, vp_refs = refs[:pps], refs[pps:2 * pps]
    o_ref, wbd_ref, s_ref, vall_ref = refs[2 * pps:]
    p_i = pl.program_id(1)
    nh, hd = q_ref.shape[2], q_ref.shape[3]
    lanes = wbd_ref.shape[1]
    scale = hd ** -0.5
    n_past = n_pages // (MOBA_BLOCK // PAGE_SIZE)
    head_i = lax.broadcasted_iota(jnp.int32, (nh, hd), 0)

    @pl.when(p_i == 0)
    def _():
        rows = [jnp.concatenate([jnp.where(head_i == h, q_ref[t, 0], 0.0) for h in range(nh)], axis=-1)
                for t in range(n_t)]
        rows.append(jnp.zeros((lanes - n_t * nh, nh * hd), F32))
        wbd_ref[...] = jnp.concatenate(rows, axis=0).T.astype(BF16)

    for g, (kp_ref, vp_ref) in enumerate(zip(kp_refs, vp_refs)):
        off = pl.multiple_of((p_i * pps + g) * PAGE_SIZE, PAGE_SIZE)
        acc = None
        for h in range(nh):
            kh = kp_ref[pl.ds(h, PAGE_SIZE, stride=nh), :].astype(BF16)
            part = _dot(kh, wbd_ref[h * hd:(h + 1) * hd, :])
            acc = part if acc is None else acc + part
            vall_ref[pl.ds(off, PAGE_SIZE), h * hd:(h + 1) * hd] = (
                vp_ref[pl.ds(h, PAGE_SIZE, stride=nh), :].astype(BF16))
        s_ref[pl.ds(off, PAGE_SIZE), :] = acc

    @pl.when(p_i == n_pages // pps - 1)
    def _():
        s_all = s_ref[...]
        blocks = [s_all[j * MOBA_BLOCK:(j + 1) * MOBA_BLOCK] for j in range(n_past)]
        gate = [jnp.sum(b, axis=0, keepdims=True) * (1.0 / MOBA_BLOCK) for b in blocks]
        sel = []
        for j in range(n_past):
            rank = jnp.zeros_like(gate[j])
            for i in range(n_past):
                if i != j:
                    beats = (gate[i] >= gate[j]) if i < j else (gate[i] > gate[j])
                    rank = rank + beats.astype(F32)
            sel.append(rank < MOBA_TOPK)
        s_past = jnp.concatenate(
            [jnp.where(sel[j], blocks[j] * scale, NEG_BIG) for j in range(n_past)], axis=0)
        m_row = jnp.max(s_past, axis=0, keepdims=True)
        p_past = jnp.exp(s_past - m_row)
        l_row = jnp.sum(p_past, axis=0, keepdims=True)
        full = _dot(p_past.T.astype(BF16), vall_ref[...])
        stats = jnp.concatenate([m_row, l_row, jnp.zeros((lanes - 2, lanes), F32)], axis=0).T
        for t in range(n_t):
            rows = full[t * nh:(t + 1) * nh]
            out = functools.reduce(
                lambda a, b: a + b,
                [jnp.where(head_i == h, rows[:, h * hd:(h + 1) * hd], 0.0) for h in range(nh)])
            m_past = stats[t * nh:(t + 1) * nh, 0:1]
            l_past = stats[t * nh:(t + 1) * nh, 1:2]
            s_new = [jnp.sum(q_ref[t, 0] * kn_ref[t2, 0], axis=-1, keepdims=True) * scale
                     for t2 in range(t + 1)]
            m = functools.reduce(jnp.maximum, s_new, m_past)
            alpha = jnp.exp(m_past - m)
            out = alpha * out
            l = alpha * l_past
            for t2 in range(t + 1):
                p_new = jnp.exp(s_new[t2] - m)
                l = l + p_new
                out = out + p_new * vn_ref[t2, 0]
            o_ref[t, 0] = out / l


def _moba_sample(q, k, v, cache_k, cache_v, layer, page_table, n_t, n_heads):
    n, d = q.shape
    bd = n // n_t
    hd = d // n_heads
    n_pages = page_table.shape[1]
    past = n_pages * PAGE_SIZE
    lanes = 128
    assert n_pages % (MOBA_BLOCK // PAGE_SIZE) == 0 and n_t <= MOBA_BLOCK and n_t * n_heads <= lanes - 2
    q4, k4, v4 = (a.reshape(n_t, bd, n_heads, hd) for a in (q, k, v))
    n_layers, n_phys = cache_k.shape[:2]
    ck, cv = (c.reshape(n_layers, n_phys, PAGE_SIZE * n_heads, hd) for c in (cache_k, cache_v))
    new_spec = pl.BlockSpec((n_t, 1, n_heads, hd), lambda b, p, pt: (0, b, 0, 0))
    pps = next(p for p in (16, 8, 2) if n_pages % p == 0)
    page_specs = [pl.BlockSpec((None, None, PAGE_SIZE * n_heads, hd),
                               functools.partial(lambda g, b, p, pt: (layer, pt[b, p * pps + g], 0, 0), g))
                  for g in range(pps)]
    kern = functools.partial(_moba_sample_kernel, n_t=n_t, n_pages=n_pages, pps=pps)
    out = pl.pallas_call(
        kern,
        grid_spec=pltpu.PrefetchScalarGridSpec(
            num_scalar_prefetch=1,
            grid=(bd, n_pages // pps),
            in_specs=[new_spec, new_spec, new_spec] + page_specs + page_specs,
            out_specs=new_spec,
            scratch_shapes=[pltpu.VMEM((d, lanes), BF16), pltpu.VMEM((past, lanes), F32),
                            pltpu.VMEM((past, d), BF16)],
        ),
        out_shape=jax.ShapeDtypeStruct((n_t, bd, n_heads, hd), F32),
        compiler_params=_params("parallel", "arbitrary"),
        name="moba_sample",
    )(page_table, q4, k4, v4, *([ck] * pps), *([cv] * pps))
    return out.reshape(n, d)


def _gelu(z):
    return 0.5 * z * (1.0 + jnp.tanh(0.7978845608028654 * (z + 0.044715 * z * z * z)))


def _gmlp_front(x, g_ref, win_ref, lng_ref, lnb_ref):
    h = _rms(x, g_ref[...]).astype(BF16)
    z = _gelu(_dot(h, win_ref[...]))
    dg = z.shape[1] // 2
    u, v = z[:, :dg], z[:, dg:]
    mu = jnp.mean(v, axis=-1, keepdims=True)
    var = jnp.mean(jnp.square(v - mu), axis=-1, keepdims=True)
    vn = (v - mu) * lax.rsqrt(var + LN_EPS) * lng_ref[...] + lnb_ref[...]
    return u, vn


def _gmlp_prompt_kernel(x_ref, g_ref, win_ref, lng_ref, lnb_ref, ws_ref, bias_ref, wout_ref,
                        y_ref, vn_ref, *, tm):
    x = x_ref[...]
    u, vn = _gmlp_front(x, g_ref, win_ref, lng_ref, lnb_ref)
    vn_ref[...] = vn
    ck = GMLP_CHUNK
    gc = vn.shape[1] // GMLP_GROUPS
    row_i = lax.broadcasted_iota(jnp.int32, (ck, ck), 0)
    col_i = lax.broadcasted_iota(jnp.int32, (ck, ck), 1)
    wc = [jnp.where(row_i >= col_i, ws_ref[g], 0.0).astype(BF16) for g in range(GMLP_GROUPS)]
    vb = vn.astype(BF16)
    mix = jnp.concatenate(
        [jnp.concatenate([_dot(wc[g], vb[c * ck:(c + 1) * ck, g * gc:(g + 1) * gc])
                          for g in range(GMLP_GROUPS)], axis=-1) + bias_ref[...]
         for c in range(tm // ck)], axis=0)
    y_ref[...] = x + _dot((u * mix).astype(BF16), wout_ref[...])


def _gmlp_prompt(x, g, w_in, ln_g, ln_b, w_s, bias, w_out, tm):
    n, d = x.shape
    dg = w_out.shape[0]
    kern = functools.partial(_gmlp_prompt_kernel, tm=tm)
    return pl.pallas_call(
        kern,
        grid=(n // tm,),
        in_specs=[pl.BlockSpec((tm, d), lambda i: (i, 0)), _full((1, d)), _full(w_in.shape),
                  _full((1, dg)), _full((1, dg)), _full(w_s.shape), _full(bias.shape),
                  _full(w_out.shape)],
        out_specs=[pl.BlockSpec((tm, d), lambda i: (i, 0)), pl.BlockSpec((tm, dg), lambda i: (i, 0))],
        out_shape=[jax.ShapeDtypeStruct((n, d), F32), jax.ShapeDtypeStruct((n, dg), F32)],
        compiler_params=_params("parallel"),
        name="gmlp_prompt",
    )(x, g, w_in, ln_g, ln_b, w_s, bias, w_out)


def _gmlp_sample_kernel(ws_ref, bs_ref, x_ref, g_ref, win_ref, lng_ref, lnb_ref, wout_ref,
                        y_ref, vn_ref, *, n_t):
    x = x_ref[...]
    u, vn = _gmlp_front(x, g_ref, win_ref, lng_ref, lnb_ref)
    vn_ref[...] = vn
    bd = x.shape[0] // n_t
    gc = vn.shape[1] // GMLP_GROUPS
    rows = []
    for t in range(n_t):
        parts = []
        for g in range(GMLP_GROUPS):
            sl = slice(g * gc, (g + 1) * gc)
            acc = jnp.full((bd, gc), bs_ref[g * n_t + t], F32)
            for s in range(t + 1):
                acc = acc + ws_ref[(g * n_t + t) * n_t + s] * vn[s * bd:(s + 1) * bd, sl]
            parts.append(acc)
        rows.append(jnp.concatenate(parts, axis=-1))
    mix = jnp.concatenate(rows, axis=0)
    y_ref[...] = x + _dot((u * mix).astype(BF16), wout_ref[...])


def _gmlp_sample(x, g, w_in, ln_g, ln_b, ws_flat, bs_flat, w_out, n_t):
    n, d = x.shape
    dg = w_out.shape[0]
    kern = functools.partial(_gmlp_sample_kernel, n_t=n_t)
    smem = pl.BlockSpec(memory_space=pltpu.SMEM)
    return pl.pallas_call(
        kern,
        grid=(1,),
        in_specs=[smem, smem, _full(x.shape), _full((1, d)), _full(w_in.shape), _full((1, dg)),
                  _full((1, dg)), _full(w_out.shape)],
        out_specs=[_full((n, d)), _full((n, dg))],
        out_shape=[jax.ShapeDtypeStruct((n, d), F32), jax.ShapeDtypeStruct((n, dg), F32)],
        compiler_params=_params("arbitrary"),
        name="gmlp_sample",
    )(ws_flat, bs_flat, x, g, w_in, ln_g, ln_b, w_out)


ROUTE_LANES = 128
SLOT_ALIGN = 16
MOE_SUB = 256
ROUTE_ROWS = 32


def _router_kernel(x_ref, g_ref, rw_ref, rb_ref, tri_ref, sel_ref, selt_ref, *refs, n_cast):
    hn_ref, col_ref, row_ref, meta_ref = refs[n_cast:n_cast + 4]
    for src_ref, dst_ref in zip(refs[:n_cast], refs[n_cast + 4:]):
        dst_ref[...] = src_ref[...].astype(BF16)
    ne = MOE_GROUPS * MOE_PER_GROUP
    hb = _rms(x_ref[...], g_ref[...]).astype(BF16)
    hn_ref[...] = hb
    lt = (_dot(hb, rw_ref[...]) + rb_ref[...]).T[:ROUTE_ROWS]
    tm = lt.shape[1]
    ridx = lax.broadcasted_iota(jnp.int32, lt.shape, 0)
    lg = jnp.where((ridx >= ne) & (ridx < ne + MOE_GROUPS), lt, -jnp.inf)
    g_max = jnp.max(lg, axis=0, keepdims=True)
    g_idx = jnp.min(jnp.where(lg == g_max, ridx, ROUTE_ROWS), axis=0, keepdims=True) - ne
    g_p = 1.0 / jnp.sum(jnp.exp(lg - g_max), axis=0, keepdims=True)
    first = g_idx * MOE_PER_GROUP
    l1 = jnp.where((ridx >= first) & (ridx < first + MOE_PER_GROUP), lt, -jnp.inf)
    m1 = jnp.max(l1, axis=0, keepdims=True)
    e1 = jnp.min(jnp.where(l1 == m1, ridx, ROUTE_ROWS), axis=0, keepdims=True)
    l2 = jnp.where(ridx == e1, -jnp.inf, l1)
    m2 = jnp.max(l2, axis=0, keepdims=True)
    e2 = jnp.min(jnp.where(l2 == m2, ridx, ROUTE_ROWS), axis=0, keepdims=True)
    z = jnp.sum(jnp.exp(l1 - m1), axis=0, keepdims=True)
    p1 = 1.0 / z
    p2 = jnp.exp(m2 - m1) / z
    w1 = g_p * p1 / (p1 + p2)
    w2 = g_p * p2 / (p1 + p2)
    hit1 = ridx == e1
    hit2 = ridx == e2
    routed = (jnp.where(hit1, 1.0, 0.0) + jnp.where(hit2, 1.0, 0.0)).astype(BF16)
    rank = _dot(routed, tri_ref[...])
    cnt = _dot(routed, sel_ref[...])
    pad = jnp.floor((cnt + (SLOT_ALIGN - 1)) * (1.0 / SLOT_ALIGN)) * SLOT_ALIGN
    ridx_c = lax.broadcasted_iota(jnp.int32, cnt.shape, 0)
    incl = pad
    shift = 1
    while shift < ne:
        incl = incl + jnp.where(ridx_c >= shift, pltpu.roll(incl, shift, axis=0), 0.0)
        shift *= 2
    off = incl - pad
    slot = rank + _dot(off.astype(BF16), selt_ref[...])
    s1 = jnp.sum(jnp.where(hit1, slot, 0.0), axis=0, keepdims=True)
    s2 = jnp.sum(jnp.where(hit2, slot, 0.0), axis=0, keepdims=True)
    rows = jnp.concatenate([s1, s2, w1, w2, jnp.zeros((row_ref.shape[0] - 4, tm), F32)], axis=0)
    row_ref[...] = rows
    col_ref[...] = jnp.concatenate([rows, jnp.zeros((ROUTE_LANES - rows.shape[0], tm), F32)], axis=0).T
    meta_ref[0] = cnt.astype(jnp.int32)
    meta_ref[1] = off.astype(jnp.int32)


def _router(x, g, rw, rb, tm, sub, layer, to_cast):
    n, d = x.shape
    nt = n // tm
    cast_in, cast_out, cast_shape = [], [], []
    for w in to_cast:
        per_step = w.shape[1] // nt
        assert per_step * nt == w.shape[1]
        cast_in.append(pl.BlockSpec((None, per_step) + w.shape[2:], lambda i: (layer, i, 0, 0)))
        cast_out.append(pl.BlockSpec((per_step,) + w.shape[2:], lambda i: (i, 0, 0)))
        cast_shape.append(jax.ShapeDtypeStruct(w.shape[1:], BF16))
    assert (2 * sub) // SLOT_ALIGN + MOE_GROUPS * MOE_PER_GROUP < 256
    tok = jnp.arange(tm)
    blocks = tok // sub
    tri = ((blocks[:, None] == blocks[None, :]) & (tok[:, None] < tok[None, :])).astype(BF16)
    sel = (blocks[:, None] == jnp.arange(ROUTE_LANES)[None, :]).astype(BF16)
    out = pl.pallas_call(
        functools.partial(_router_kernel, n_cast=len(to_cast)),
        grid=(nt,),
        in_specs=[pl.BlockSpec((tm, d), lambda i: (i, 0)), _full((1, d)), _full(rw.shape),
                  _full(rb.shape), _full(tri.shape), _full(sel.shape), _full(sel.T.shape)] + cast_in,
        out_specs=[pl.BlockSpec((tm, d), lambda i: (i, 0)),
                   pl.BlockSpec((tm, ROUTE_LANES), lambda i: (i, 0)),
                   pl.BlockSpec((8, tm), lambda i: (0, i)),
                   pl.BlockSpec((None, 2, ROUTE_ROWS, ROUTE_LANES), lambda i: (i, 0, 0, 0))] + cast_out,
        out_shape=[jax.ShapeDtypeStruct((n, d), BF16), jax.ShapeDtypeStruct((n, ROUTE_LANES), F32),
                   jax.ShapeDtypeStruct((8, n), F32),
                   jax.ShapeDtypeStruct((nt, 2, ROUTE_ROWS, ROUTE_LANES), jnp.int32)] + cast_shape,
        compiler_params=_params("arbitrary"),
        name="moe_router",
    )(x, g, rw, rb, tri, sel, sel.T, *to_cast)
    return out[:4], out[4:]


def _moe_routed_kernel(cnt_ref, off_ref, trips_ref, hn_ref, col_ref, row_ref, x_ref, gfin_ref,
                       wg_ref, wu_ref, wd_ref, y_ref, xs_ref, ys_ref, gs_ref,
                       *, sub, mcs, n_gather, n_scatter, final_norm):
    i = pl.program_id(0)
    eg = pl.program_id(1)
    eps = wg_ref.shape[0]
    ne = pl.num_programs(1) * eps
    n_sub = xs_ref.shape[0]
    dump = n_gather
    rows_of = lambda s: slice(s * sub, (s + 1) * sub)

    @pl.when(eg == 0)
    def _():
        y_ref[...] = x_ref[...]
        ys_ref[...] = jnp.zeros_like(ys_ref)
        sid = lax.broadcasted_iota(jnp.int32, (n_gather, sub), 0).astype(F32)
        for s in range(n_sub):
            s1, s2, w1, w2 = (row_ref[r:r + 1, rows_of(s)] for r in range(4))
            h1 = s1 == sid
            h2 = s2 == sid
            onehot = (jnp.where(h1, 1.0, 0.0) + jnp.where(h2, 1.0, 0.0)).astype(BF16)
            xs_ref[s, 0:n_gather, :] = _dot(onehot, hn_ref[rows_of(s), :]).astype(BF16)
            xs_ref[s, dump:dump + mcs, :] = jnp.zeros((mcs, xs_ref.shape[2]), BF16)
            gs_ref[s, 0:n_gather, :] = jnp.sum(
                jnp.where(h1, w1, 0.0) + jnp.where(h2, w2, 0.0), axis=-1, keepdims=True)
            gs_ref[s, dump:dump + mcs, :] = jnp.zeros((mcs, 1), F32)

    for r in range(eps):
        e = eg * eps + r
        counts = [cnt_ref[(i * n_sub + s) * ne + e] for s in range(n_sub)]
        firsts = [off_ref[(i * n_sub + s) * ne + e] for s in range(n_sub)]

        def chunk(k, carry, r=r, counts=counts, firsts=firsts):
            rows = [pl.multiple_of(jnp.where(k * mcs < counts[s], firsts[s] + k * mcs, dump), SLOT_ALIGN)
                    for s in range(n_sub)]
            xg = jnp.concatenate([xs_ref[s, pl.ds(rows[s], mcs), :] for s in range(n_sub)], axis=0)
            gate = jnp.concatenate([gs_ref[s, pl.ds(rows[s], mcs), :] for s in range(n_sub)], axis=0)
            a = _dot(xg, wg_ref[r])
            b = _dot(xg, wu_ref[r])
            hid = ((a * jax.nn.sigmoid(a)) * b * gate).astype(BF16)
            out = _dot(hid, wd_ref[r]).astype(BF16)
            for s in range(n_sub):
                ys_ref[s, pl.ds(rows[s], mcs), :] = out[s * mcs:(s + 1) * mcs]
            return carry

        lax.fori_loop(0, trips_ref[i * ne + e], chunk, 0)

    @pl.when(eg == pl.num_programs(1) - 1)
    def _():
        sid = lax.broadcasted_iota(jnp.int32, (sub, n_scatter), 1).astype(F32)
        for s in range(n_sub):
            onehot = (jnp.where(col_ref[rows_of(s), 0:1] == sid, 1.0, 0.0)
                      + jnp.where(col_ref[rows_of(s), 1:2] == sid, 1.0, 0.0)).astype(BF16)
            out = y_ref[rows_of(s), :] + _dot(onehot, ys_ref[s, 0:n_scatter, :])
            y_ref[rows_of(s), :] = _rms(out, gfin_ref[...]) if final_norm else out


def _round_up(a, b):
    return -(-a // b) * b


def _moe_routed(cnt, off, hn, col, row, wg, wu, wd, x, g_final, tm, sub):
    n, d = x.shape
    ne, _, de = wg.shape
    n_sub = tm // sub
    mcs = _round_up(2 * sub * 3 // (ne * 2), SLOT_ALIGN)
    most = 2 * sub + (ne - 1) * SLOT_ALIGN
    n_scatter = _round_up(most, 128)
    n_gather = _round_up(most + mcs, 128)
    trips = jnp.max((cnt + mcs - 1) // mcs, axis=1).reshape(-1)
    eps = 2 if ne % 2 == 0 else 1
    final_norm = g_final is not None
    if g_final is None:
        g_final = jnp.ones((1, d), F32)
    kern = functools.partial(_moe_routed_kernel, sub=sub, mcs=mcs, n_gather=n_gather, n_scatter=n_scatter,
                             final_norm=final_norm)
    tile = lambda i, e, *_: (i, 0)
    expert = lambda i, e, *_: (e, 0, 0)
    return pl.pallas_call(
        kern,
        grid_spec=pltpu.PrefetchScalarGridSpec(
            num_scalar_prefetch=3,
            grid=(n // tm, ne // eps),
            in_specs=[pl.BlockSpec((tm, d), tile), pl.BlockSpec((tm, ROUTE_LANES), tile),
                      pl.BlockSpec((8, tm), lambda i, e, *_: (0, i)), pl.BlockSpec((tm, d), tile),
                      pl.BlockSpec((1, d), lambda i, e, *_: (0, 0)),
                      pl.BlockSpec((eps, d, de), expert), pl.BlockSpec((eps, d, de), expert),
                      pl.BlockSpec((eps, de, d), expert)],
            out_specs=pl.BlockSpec((tm, d), tile),
            scratch_shapes=[pltpu.VMEM((n_sub, n_gather + mcs, d), BF16),
                            pltpu.VMEM((n_sub, n_gather + mcs, d), BF16),
                            pltpu.VMEM((n_sub, n_gather + mcs, 1), F32)],
        ),
        out_shape=jax.ShapeDtypeStruct((n, d), F32),
        compiler_params=_params("arbitrary", "arbitrary"),
        name="moe_routed",
    )(cnt.reshape(-1), off.reshape(-1), trips, hn, col, row, x, g_final, wg, wu, wd)


def _moe(x, g, rw, rb, weights, layer, g_final, tm):
    sub = min(tm, MOE_SUB)
    n_sub = tm // sub
    assert tm % sub == 0 and n_sub <= ROUTE_LANES
    to_cast = [w for w in weights if w.dtype != BF16]
    if to_cast and any(w.shape[1] % (x.shape[0] // tm) for w in to_cast):
        weights, to_cast = [w[layer].astype(BF16) for w in weights], []
    (hn, col, row, meta), cast = _router(x, g, rw, rb, tm, sub, layer, to_cast)
    wg, wu, wd = cast if to_cast else weights
    ne = wg.shape[0]
    cnt, off = (meta[:, r, :ne, :n_sub].transpose(0, 2, 1) for r in range(2))
    return _moe_routed(cnt, off, hn, col, row, wg, wu, wd, x, g_final, tm, sub), (wg, wu, wd)


def _tile(n, pref):
    t = min(n, pref)
    assert n % t == 0, (n, t)
    return t


def kernel(x_prompt, x_sample, state_pool, cache_k, cache_v, page_table, norm_mix, norm_ffn, norm_final, pool_w_in, pool_w_grp, pool_scale, pool_w_out, attn_w_qkv, attn_w_out, gmlp_w_in, gmlp_ln_g, gmlp_ln_b, gmlp_w_s, gmlp_b_s, gmlp_w_out, moe_rg_w, moe_rg_b, moe_re_w, moe_re_b, moe_w_gate, moe_w_up, moe_w_down):
    batch, seq, d = x_prompt.shape
    dec_b, n_t, _ = x_sample.shape
    depth = norm_mix.shape[0]
    n_heads = N_HEADS
    hd = d // n_heads
    n_pages = page_table.shape[1]
    past_len = n_pages * PAGE_SIZE
    assert seq % MOBA_BLOCK == 0 and seq % GMLP_CHUNK == 0 and n_t <= GMLP_CHUNK
    assert past_len % GMLP_CHUNK == 0 and past_len >= POOL_STATE

    n_p, n_s = batch * seq, dec_b * n_t
    xp = x_prompt.reshape(n_p, d)
    xs = x_sample.transpose(1, 0, 2).reshape(n_s, d)
    tp = _tile(seq, 512)
    row = lambda a: a.reshape(1, -1)

    def to_batch_major(a):
        return a.reshape((n_t, dec_b) + a.shape[1:]).swapaxes(0, 1)

    pool_p, pool_s, k_p, v_p, k_s, v_s, gv_p, gv_s = [], [], [], [], [], [], [], []
    for i in range(depth):
        kind, j = i % N_MIXERS, i // N_MIXERS
        g_mix = row(norm_mix[i])
        if kind == 0:
            w_in, w_grp, w_out = (a.astype(BF16) for a in (pool_w_in[j], pool_w_grp[j], pool_w_out[j]))
            scale = row(pool_scale[j])
            xp, st = _pool_prompt(xp, g_mix, w_in, w_grp, scale, w_out, batch, seq, tp)
            pool_p.append(st)
            xs, u_s = _pool_sample(xs, state_pool[j].transpose(1, 0, 2), g_mix, w_in, w_grp, scale,
                                   w_out, n_t, past_len)
            pool_s.append(jnp.concatenate([state_pool[j][:, n_t:], to_batch_major(u_s)], axis=1))
        elif kind == 1:
            w_qkv, w_out = attn_w_qkv[j].astype(BF16), attn_w_out[j].astype(BF16)
            qp, kp, vp = _norm_matmul(xp, g_mix, w_qkv, 3, tp)
            qs, ks, vs = _norm_matmul(xs, g_mix, w_qkv, 3, _tile(n_s, 512))
            ap = _moba_prompt(qp, kp, vp, batch, seq, n_heads)
            a_s = _moba_sample(qs, ks, vs, cache_k, cache_v, j, page_table, n_t, n_heads)
            xp = _matmul_residual(ap, w_out, xp, tp)
            xs = _matmul_residual(a_s, w_out, xs, _tile(n_s, 512))
            k_p.append(kp.reshape(batch, seq, n_heads, hd))
            v_p.append(vp.reshape(batch, seq, n_heads, hd))
            k_s.append(to_batch_major(ks).reshape(dec_b, n_t, n_heads, hd))
            v_s.append(to_batch_major(vs).reshape(dec_b, n_t, n_heads, hd))
        else:
            w_in, w_out = gmlp_w_in[j].astype(BF16), gmlp_w_out[j].astype(BF16)
            ln_g, ln_b = row(gmlp_ln_g[j]), row(gmlp_ln_b[j])
            gc = w_out.shape[0] // GMLP_GROUPS
            bias = jnp.broadcast_to(gmlp_b_s[j].T[:, :, None], (GMLP_CHUNK, GMLP_GROUPS, gc))
            xp, vn_p = _gmlp_prompt(xp, g_mix, w_in, ln_g, ln_b, gmlp_w_s[j],
                                    bias.reshape(GMLP_CHUNK, GMLP_GROUPS * gc), w_out, tp)
            xs, vn_s = _gmlp_sample(xs, g_mix, w_in, ln_g, ln_b,
                                    gmlp_w_s[j][:, :n_t, :n_t].reshape(-1),
                                    gmlp_b_s[j][:, :n_t].reshape(-1), w_out, n_t)
            gv_p.append(vn_p.reshape(batch, seq, -1))
            gv_s.append(to_batch_major(vn_s))

        g_ffn = row(norm_ffn[i])
        ne = moe_re_w.shape[2]
        fill = jnp.zeros((d, ROUTE_LANES - ne - MOE_GROUPS), F32)
        rw = jnp.concatenate([moe_re_w[i], moe_rg_w[i], fill], axis=1).astype(BF16)
        rb = jnp.concatenate([moe_re_b[i], moe_rg_b[i], fill[0]]).reshape(1, ROUTE_LANES)
        g_fin = row(norm_final) if i == depth - 1 else None
        xp, experts = _moe(xp, g_ffn, rw, rb, (moe_w_gate, moe_w_up, moe_w_down), i, g_fin, _tile(n_p, 1024))
        xs, _ = _moe(xs, g_ffn, rw, rb, experts, i, g_fin, _tile(n_s, 512))

    y_p = xp.reshape(batch, seq, d)
    y_s = to_batch_major(xs)
    return (y_p, y_s, jnp.stack(pool_p), jnp.stack(pool_s), jnp.stack(k_p), jnp.stack(v_p),
            jnp.stack(k_s), jnp.stack(v_s), jnp.stack(gv_p), jnp.stack(gv_s))
```

```python
import functools

import jax
import jax.numpy as jnp
from jax import lax
from jax.experimental import pallas as pl
from jax.experimental.pallas import tpu as pltpu

F32 = jnp.float32
BF16 = jnp.bfloat16

RMS_EPS = 1e-6
LN_EPS = 1e-5
N_MIXERS = 3
POOL_WINDOWS = (2, 4, 8, 16)
POOL_STATE = max(POOL_WINDOWS) - 1
MOBA_BLOCK = 256
MOBA_TOPK = 3
GMLP_CHUNK = 128
GMLP_GROUPS = 8
MOE_GROUPS = 4
MOE_PER_GROUP = 4
PAGE_SIZE = 128
N_HEADS = 8

V7X_VMEM_LIMIT_BYTES = 56 * 1024 * 1024
NEG_BIG = -1e30
LOG2_E = 1.4426950408889634


def _params(*sem):
    return pltpu.CompilerParams(dimension_semantics=sem, vmem_limit_bytes=V7X_VMEM_LIMIT_BYTES)


def _rms(x, g):
    return x * lax.rsqrt(jnp.mean(x * x, axis=-1, keepdims=True) + RMS_EPS) * g


def _dot(a, b):
    return jnp.dot(a, b, preferred_element_type=F32)


def _dot_nt(a, b, precision=None):
    return lax.dot_general(a, b, (((1,), (1,)), ((), ())), preferred_element_type=F32,
                           precision=precision)


def _full(shape):
    return pl.BlockSpec(shape, lambda *_: (0,) * len(shape))


def _norm_matmul_kernel(x_ref, g_ref, w_ref, *out_refs):
    h = _rms(x_ref[...], g_ref[...]).astype(BF16)
    f = w_ref.shape[1] // len(out_refs)
    for i, o_ref in enumerate(out_refs):
        o_ref[...] = _dot(h, w_ref[:, i * f:(i + 1) * f]).astype(o_ref.dtype)


def _norm_matmul(x, g, w, out_dtypes, tm):
    n, d = x.shape
    n_out = len(out_dtypes)
    f = w.shape[1] // n_out
    return pl.pallas_call(
        _norm_matmul_kernel,
        grid=(n // tm,),
        in_specs=[pl.BlockSpec((tm, d), lambda i: (i, 0)), _full((1, d)), _full(w.shape)],
        out_specs=[pl.BlockSpec((tm, f), lambda i: (i, 0))] * n_out,
        out_shape=[jax.ShapeDtypeStruct((n, f), dt) for dt in out_dtypes],
        compiler_params=_params("parallel"),
        name="norm_matmul",
    )(x, g, w)


def _matmul_residual_kernel(a_ref, w_ref, x_ref, y_ref):
    y_ref[...] = x_ref[...] + _dot(a_ref[...].astype(BF16), w_ref[...])


def _matmul_residual(a, w, x, tm):
    n, d = x.shape
    k = a.shape[1]
    return pl.pallas_call(
        _matmul_residual_kernel,
        grid=(n // tm,),
        in_specs=[pl.BlockSpec((tm, k), lambda i: (i, 0)), _full(w.shape),
                  pl.BlockSpec((tm, d), lambda i: (i, 0))],
        out_specs=pl.BlockSpec((tm, d), lambda i: (i, 0)),
        out_shape=jax.ShapeDtypeStruct((n, d), F32),
        compiler_params=_params("parallel"),
        name="matmul_residual",
    )(a, w, x)


def _pool_tail(r, x, wgrp_ref, scale_ref, wout_ref):
    gc = wgrp_ref.shape[1]
    rb = r.astype(BF16)
    t = jnp.concatenate([_dot(rb[:, g * gc:(g + 1) * gc], wgrp_ref[g])
                         for g in range(len(POOL_WINDOWS))], axis=-1)
    t = (t * scale_ref[...]).astype(BF16)
    return x + _dot(t, wout_ref[...])


def _pool_prompt_kernel(x_ref, g_ref, win_ref, wgrp_ref, scale_ref, wout_ref,
                        y_ref, st_ref, carry_ref, *, ts, ns):
    s = pl.program_id(1)
    halo = carry_ref.shape[0]
    gc = wgrp_ref.shape[1]

    @pl.when(s == 0)
    def _():
        carry_ref[...] = jnp.zeros_like(carry_ref)

    x = x_ref[...]
    h = _rms(x, g_ref[...]).astype(BF16)
    u = _dot(h, win_ref[...])
    ext = jnp.concatenate([carry_ref[...], u], axis=0)
    pos = s * ts + lax.broadcasted_iota(jnp.int32, (ts, 1), 0)
    parts = []
    acc = ext
    width = 1
    for w in POOL_WINDOWS:
        while width < w:
            acc = acc + pltpu.roll(acc, width, axis=0)
            width *= 2
        assert width == w
        cnt = jnp.minimum(pos + 1, w).astype(F32)
        parts.append(acc[halo:, :gc] / cnt)
        acc = acc[:, gc:]
    r = jnp.concatenate(parts, axis=-1) - u
    y_ref[...] = _pool_tail(r, x, wgrp_ref, scale_ref, wout_ref)
    carry_ref[...] = u[ts - halo:, :]

    @pl.when(s == ns - 1)
    def _():
        st_ref[...] = carry_ref[halo - POOL_STATE:, :]


def _pool_prompt(x, g, w_in, w_grp, scale, w_out, batch, seq, ts):
    n, d = x.shape
    ns = seq // ts
    halo = 16
    kern = functools.partial(_pool_prompt_kernel, ts=ts, ns=ns)
    return pl.pallas_call(
        kern,
        grid=(batch, ns),
        in_specs=[pl.BlockSpec((ts, d), lambda b, s: (b * ns + s, 0)), _full((1, d)),
                  _full(w_in.shape), _full(w_grp.shape), _full((1, d)), _full(w_out.shape)],
        out_specs=[pl.BlockSpec((ts, d), lambda b, s: (b * ns + s, 0)),
                   pl.BlockSpec((None, POOL_STATE, d), lambda b, s: (b, 0, 0))],
        out_shape=[jax.ShapeDtypeStruct((n, d), F32),
                   jax.ShapeDtypeStruct((batch, POOL_STATE, d), F32)],
        scratch_shapes=[pltpu.VMEM((halo, d), F32)],
        compiler_params=_params("parallel", "arbitrary"),
        name="pool_prompt",
    )(x, g, w_in, w_grp, scale, w_out)


def _pool_sample_kernel(x_ref, pref_ref, g_ref, win_ref, wgrp_ref, scale_ref, wout_ref,
                        y_ref, st_ref, *, n_t, pos0):
    bd = x_ref.shape[0] // n_t
    gc = wgrp_ref.shape[1]
    x = x_ref[...]
    h = _rms(x, g_ref[...]).astype(BF16)
    u = _dot(h, win_ref[...])
    us = [u[t * bd:(t + 1) * bd] for t in range(n_t)]
    for r in range(POOL_STATE):
        st_ref[:, r, :] = pref_ref[:, r + n_t, :] if r + n_t < POOL_STATE else us[r + n_t - POOL_STATE]
    tail = [jnp.zeros_like(us[0])]
    for m in range(1, POOL_STATE + 1):
        tail.append(tail[-1] + pref_ref[:, POOL_STATE - m, :])
    rows = []
    for t in range(n_t):
        parts = []
        for gi, w in enumerate(POOL_WINDOWS):
            sl = slice(gi * gc, (gi + 1) * gc)
            n_new = min(w, t + 1)
            wsum = us[t][:, sl]
            for k in range(1, n_new):
                wsum = wsum + us[t - k][:, sl]
            if w - n_new:
                wsum = wsum + tail[w - n_new][:, sl]
            parts.append(wsum / float(min(pos0 + t + 1, w)))
        rows.append(jnp.concatenate(parts, axis=-1) - us[t])
    r = jnp.concatenate(rows, axis=0)
    y_ref[...] = _pool_tail(r, x, wgrp_ref, scale_ref, wout_ref)


def _pool_sample(x, state, layer, g, w_in, w_grp, scale, w_out, n_t, pos0):
    n, d = x.shape
    assert n_t <= POOL_STATE
    st_shape = state.shape[1:]
    kern = functools.partial(_pool_sample_kernel, n_t=n_t, pos0=pos0)
    return pl.pallas_call(
        kern,
        grid=(1,),
        in_specs=[_full(x.shape), pl.BlockSpec((None,) + st_shape, lambda i: (layer, 0, 0, 0)),
                  _full((1, d)), _full(w_in.shape), _full(w_grp.shape), _full((1, d)), _full(w_out.shape)],
        out_specs=[_full((n, d)), _full(st_shape)],
        out_shape=[jax.ShapeDtypeStruct((n, d), F32), jax.ShapeDtypeStruct(st_shape, F32)],
        compiler_params=_params("arbitrary"),
        name="pool_sample",
    )(x, state, g, w_in, w_grp, scale, w_out)


def _top_blocks(gate, n_valid, n_blocks):
    idx = lax.broadcasted_iota(jnp.int32, gate.shape, 0)
    valid = idx < n_valid
    rows = []
    for j in range(n_blocks):
        gj = gate[j:j + 1, :]
        beats = valid & ((gate > gj) | ((gate == gj) & (idx < j)))
        rank = jnp.sum(beats.astype(F32), axis=0, keepdims=True)
        rows.append(jnp.where(rank < MOBA_TOPK, jnp.where(j < n_valid, 1.0, 0.0), 0.0))
    return rows


def _moba_prompt_kernel(q_ref, k_ref, v_ref, o_ref, kb_ref, vt_ref, kmean_ref, sel_ref, s_ref,
                        *, nb, group, hps, hd):
    qi = pl.program_id(2)
    blk = MOBA_BLOCK
    scale = hd ** -0.5
    heads = range(hps)
    lanes = lambda h: slice(h * hd, (h + 1) * hd)

    @pl.when(qi == 0)
    def _():
        k = k_ref[...]
        kb_ref[...] = k.astype(BF16)
        kmean_ref[...] = jnp.mean(k.reshape(nb, blk, hps * hd), axis=1)
        vt_ref[...] = v_ref[...].T.astype(BF16)

    q = q_ref[...]
    qb = [q[:, lanes(h)].astype(BF16) for h in heads]
    for h in heads:
        gate = _dot_nt(kmean_ref[:, lanes(h)].astype(BF16), qb[h])
        for j, row in enumerate(_top_blocks(gate, qi, nb)):
            sel_ref[h * nb + j:h * nb + j + 1, :] = row
    key_i = lax.broadcasted_iota(jnp.int32, (blk, blk), 0)
    qry_i = lax.broadcasted_iota(jnp.int32, (blk, blk), 1)
    scale2 = scale * LOG2_E

    def scores(h, j, n):
        kj = kb_ref[pl.ds(pl.multiple_of(j * blk, blk), n * blk), lanes(h)]
        return _dot_nt(kj, qb[h]) * scale2

    def pv(h, j, n, p):
        vj = vt_ref[lanes(h), pl.ds(pl.multiple_of(j * blk, blk), n * blk)]
        return _dot(vj, p.astype(BF16))

    n_trips = qi // group + 1
    causal = jnp.where(key_i <= qry_i, 1.0, 0.0)

    def score_pass(g, ms):
        j = g * group
        out = []
        for h, m in zip(heads, ms):
            keep = jnp.concatenate(
                [jnp.where(j + r == qi, causal,
                           jnp.broadcast_to(sel_ref[pl.ds(h * nb + j + r, 1), :], (blk, blk)))
                 for r in range(group)], axis=0)
            s = jnp.where(keep > 0.0, scores(h, j, group), NEG_BIG)
            s_ref[h, pl.ds(pl.multiple_of(j * blk, blk), group * blk), :] = s
            out.append(jnp.maximum(m, jnp.max(s, axis=0, keepdims=True)))
        return tuple(out)

    ms = lax.fori_loop(0, n_trips, score_pass, tuple(jnp.full((1, blk), NEG_BIG, F32) for _ in heads))
    carry = [(jnp.zeros((1, blk), F32), jnp.zeros((hd, blk), F32)) for _ in heads]

    def value_pass(g, carry):
        j = g * group
        out = []
        for h, (l, acc) in zip(heads, carry):
            p = jnp.exp2(s_ref[h, pl.ds(pl.multiple_of(j * blk, blk), group * blk), :] - ms[h])
            out.append((l + jnp.sum(p, axis=0, keepdims=True), acc + pv(h, j, group, p)))
        return tuple(out)

    carry = lax.fori_loop(0, n_trips, value_pass, tuple(carry))
    for h, (l, acc) in zip(heads, carry):
        o_ref[:, lanes(h)] = (acc / l).T.astype(o_ref.dtype)


def _moba_prompt(q, k, v, batch, seq, n_heads):
    n, d = q.shape
    hd = d // n_heads
    nb = seq // MOBA_BLOCK
    group = 4 if nb % 4 == 0 else 1
    hps = 2 if n_heads % 2 == 0 else 1
    kern = functools.partial(_moba_prompt_kernel, nb=nb, group=group, hps=hps, hd=hd)
    return pl.pallas_call(
        kern,
        grid=(batch, n_heads // hps, nb),
        in_specs=[pl.BlockSpec((MOBA_BLOCK, hps * hd), lambda b, h, i: (b * nb + i, h)),
                  pl.BlockSpec((seq, hps * hd), lambda b, h, i: (b, h)),
                  pl.BlockSpec((seq, hps * hd), lambda b, h, i: (b, h))],
        out_specs=pl.BlockSpec((MOBA_BLOCK, hps * hd), lambda b, h, i: (b * nb + i, h)),
        out_shape=jax.ShapeDtypeStruct((n, d), BF16),
        scratch_shapes=[pltpu.VMEM((seq, hps * hd), BF16), pltpu.VMEM((hps * hd, seq), BF16),
                        pltpu.VMEM((nb, hps * hd), F32), pltpu.VMEM((hps * nb, MOBA_BLOCK), F32),
                        pltpu.VMEM((hps, nb * MOBA_BLOCK, MOBA_BLOCK), F32)],
        compiler_params=_params("parallel", "parallel", "arbitrary"),
        name="moba_prompt",
    )(q, k, v)


def _moba_sample_kernel(pt_ref, q_ref, kn_ref, vn_ref, *refs, n_t, n_pages, pps):
    del pt_ref
    kp_refs, vp_refs = refs[:pps], refs[pps:2 * pps]
    o_ref, wbd_ref, s_ref, vall_ref = refs[2 * pps:]
    p_i = pl.program_id(1)
    nh, hd = q_ref.shape[2], q_ref.shape[3]
    lanes = wbd_ref.shape[1]
    scale = hd ** -0.5
    n_past = n_pages // (MOBA_BLOCK // PAGE_SIZE)
    head_i = lax.broadcasted_iota(jnp.int32, (nh, hd), 0)

    @pl.when(p_i == 0)
    def _():
        rows = [jnp.concatenate([jnp.where(head_i == h, q_ref[t, 0], 0.0) for h in range(nh)], axis=-1)
                for t in range(n_t)]
        rows.append(jnp.zeros((lanes - n_t * nh, nh * hd), F32))
        wbd_ref[...] = jnp.concatenate(rows, axis=0).T.astype(BF16)

    for g, (kp_ref, vp_ref) in enumerate(zip(kp_refs, vp_refs)):
        off = pl.multiple_of((p_i * pps + g) * PAGE_SIZE, PAGE_SIZE)
        acc = None
        for h in range(nh):
            kh = kp_ref[pl.ds(h, PAGE_SIZE, stride=nh), :].astype(BF16)
            part = _dot(kh, wbd_ref[h * hd:(h + 1) * hd, :])
            acc = part if acc is None else acc + part
            vall_ref[pl.ds(off, PAGE_SIZE), h * hd:(h + 1) * hd] = (
                vp_ref[pl.ds(h, PAGE_SIZE, stride=nh), :].astype(BF16))
        s_ref[pl.ds(off, PAGE_SIZE), :] = acc

    @pl.when(p_i == n_pages // pps - 1)
    def _():
        s_all = s_ref[...]
        blocks = [s_all[j * MOBA_BLOCK:(j + 1) * MOBA_BLOCK] for j in range(n_past)]
        gate = [jnp.sum(b, axis=0, keepdims=True) * (1.0 / MOBA_BLOCK) for b in blocks]
        sel = []
        for j in range(n_past):
            rank = jnp.zeros_like(gate[j])
            for i in range(n_past):
                if i != j:
                    beats = (gate[i] >= gate[j]) if i < j else (gate[i] > gate[j])
                    rank = rank + beats.astype(F32)
            sel.append(rank < MOBA_TOPK)
        s_past = jnp.concatenate(
            [jnp.where(sel[j], blocks[j] * scale, NEG_BIG) for j in range(n_past)], axis=0)
        m_row = jnp.max(s_past, axis=0, keepdims=True)
        p_past = jnp.exp(s_past - m_row)
        l_row = jnp.sum(p_past, axis=0, keepdims=True)
        full = _dot(p_past.T.astype(BF16), vall_ref[...])
        stats = jnp.concatenate([m_row, l_row, jnp.zeros((lanes - 2, lanes), F32)], axis=0).T
        for t in range(n_t):
            rows = full[t * nh:(t + 1) * nh]
            out = functools.reduce(
                lambda a, b: a + b,
                [jnp.where(head_i == h, rows[:, h * hd:(h + 1) * hd], 0.0) for h in range(nh)])
            m_past = stats[t * nh:(t + 1) * nh, 0:1]
            l_past = stats[t * nh:(t + 1) * nh, 1:2]
            s_new = [jnp.sum(q_ref[t, 0] * kn_ref[t2, 0], axis=-1, keepdims=True) * scale
                     for t2 in range(t + 1)]
            m = functools.reduce(jnp.maximum, s_new, m_past)
            alpha = jnp.exp(m_past - m)
            out = alpha * out
            l = alpha * l_past
            for t2 in range(t + 1):
                p_new = jnp.exp(s_new[t2] - m)
                l = l + p_new
                out = out + p_new * vn_ref[t2, 0]
            o_ref[t, 0] = out / l


def _moba_sample(q, k, v, cache_k, cache_v, layer, page_table, n_t, n_heads):
    n, d = q.shape
    bd = n // n_t
    hd = d // n_heads
    n_pages = page_table.shape[1]
    past = n_pages * PAGE_SIZE
    lanes = 128
    assert n_pages % (MOBA_BLOCK // PAGE_SIZE) == 0 and n_t <= MOBA_BLOCK and n_t * n_heads <= lanes - 2
    q4, k4, v4 = (a.reshape(n_t, bd, n_heads, hd) for a in (q, k, v))
    n_layers, n_phys = cache_k.shape[:2]
    ck, cv = (c.reshape(n_layers, n_phys, PAGE_SIZE * n_heads, hd) for c in (cache_k, cache_v))
    new_spec = pl.BlockSpec((n_t, 1, n_heads, hd), lambda b, p, pt: (0, b, 0, 0))
    pps = next(p for p in (16, 8, 2) if n_pages % p == 0)
    page_specs = [pl.BlockSpec((None, None, PAGE_SIZE * n_heads, hd),
                               functools.partial(lambda g, b, p, pt: (layer, pt[b, p * pps + g], 0, 0), g))
                  for g in range(pps)]
    kern = functools.partial(_moba_sample_kernel, n_t=n_t, n_pages=n_pages, pps=pps)
    out = pl.pallas_call(
        kern,
        grid_spec=pltpu.PrefetchScalarGridSpec(
            num_scalar_prefetch=1,
            grid=(bd, n_pages // pps),
            in_specs=[new_spec, new_spec, new_spec] + page_specs + page_specs,
            out_specs=new_spec,
            scratch_shapes=[pltpu.VMEM((d, lanes), BF16), pltpu.VMEM((past, lanes), F32),
                            pltpu.VMEM((past, d), BF16)],
        ),
        out_shape=jax.ShapeDtypeStruct((n_t, bd, n_heads, hd), F32),
        compiler_params=_params("parallel", "arbitrary"),
        name="moba_sample",
    )(page_table, q4, k4, v4, *([ck] * pps), *([cv] * pps))
    return out.reshape(n, d)


def _gelu(z):
    return 0.5 * z * (1.0 + jnp.tanh(0.7978845608028654 * (z + 0.044715 * z * z * z)))


def _gmlp_front(x, g_ref, win_ref, lng_ref, lnb_ref):
    h = _rms(x, g_ref[...]).astype(BF16)
    z = _gelu(_dot(h, win_ref[...]))
    dg = z.shape[1] // 2
    u, v = z[:, :dg], z[:, dg:]
    mu = jnp.mean(v, axis=-1, keepdims=True)
    var = jnp.mean(jnp.square(v - mu), axis=-1, keepdims=True)
    vn = (v - mu) * lax.rsqrt(var + LN_EPS) * lng_ref[...] + lnb_ref[...]
    return u, vn


def _gmlp_prompt_kernel(x_ref, g_ref, win_ref, lng_ref, lnb_ref, ws_ref, bias_ref, wout_ref,
                        y_ref, vn_ref, *, tm):
    x = x_ref[...]
    u, vn = _gmlp_front(x, g_ref, win_ref, lng_ref, lnb_ref)
    vn_ref[...] = vn
    ck = GMLP_CHUNK
    gc = vn.shape[1] // GMLP_GROUPS
    row_i = lax.broadcasted_iota(jnp.int32, (ck, ck), 0)
    col_i = lax.broadcasted_iota(jnp.int32, (ck, ck), 1)
    wc = [jnp.where(row_i >= col_i, ws_ref[g], 0.0).astype(BF16) for g in range(GMLP_GROUPS)]
    vb = vn.astype(BF16)
    mix = jnp.concatenate(
        [jnp.concatenate([_dot(wc[g], vb[c * ck:(c + 1) * ck, g * gc:(g + 1) * gc])
                          for g in range(GMLP_GROUPS)], axis=-1) + bias_ref[...]
         for c in range(tm // ck)], axis=0)
    y_ref[...] = x + _dot((u * mix).astype(BF16), wout_ref[...])


def _gmlp_prompt(x, g, w_in, ln_g, ln_b, w_s, bias, w_out, tm):
    n, d = x.shape
    dg = w_out.shape[0]
    kern = functools.partial(_gmlp_prompt_kernel, tm=tm)
    return pl.pallas_call(
        kern,
        grid=(n // tm,),
        in_specs=[pl.BlockSpec((tm, d), lambda i: (i, 0)), _full((1, d)), _full(w_in.shape),
                  _full((1, dg)), _full((1, dg)), _full(w_s.shape), _full(bias.shape),
                  _full(w_out.shape)],
        out_specs=[pl.BlockSpec((tm, d), lambda i: (i, 0)), pl.BlockSpec((tm, dg), lambda i: (i, 0))],
        out_shape=[jax.ShapeDtypeStruct((n, d), F32), jax.ShapeDtypeStruct((n, dg), F32)],
        compiler_params=_params("parallel"),
        name="gmlp_prompt",
    )(x, g, w_in, ln_g, ln_b, w_s, bias, w_out)


def _gmlp_sample_kernel(ws_ref, bs_ref, x_ref, g_ref, win_ref, lng_ref, lnb_ref, wout_ref,
                        y_ref, vn_ref, *, n_t):
    x = x_ref[...]
    u, vn = _gmlp_front(x, g_ref, win_ref, lng_ref, lnb_ref)
    vn_ref[...] = vn
    bd = x.shape[0] // n_t
    gc = vn.shape[1] // GMLP_GROUPS
    rows = []
    for t in range(n_t):
        parts = []
        for g in range(GMLP_GROUPS):
            sl = slice(g * gc, (g + 1) * gc)
            acc = jnp.full((bd, gc), bs_ref[g * n_t + t], F32)
            for s in range(t + 1):
                acc = acc + ws_ref[(g * n_t + t) * n_t + s] * vn[s * bd:(s + 1) * bd, sl]
            parts.append(acc)
        rows.append(jnp.concatenate(parts, axis=-1))
    mix = jnp.concatenate(rows, axis=0)
    y_ref[...] = x + _dot((u * mix).astype(BF16), wout_ref[...])


def _gmlp_sample(x, g, w_in, ln_g, ln_b, ws_flat, bs_flat, w_out, n_t):
    n, d = x.shape
    dg = w_out.shape[0]
    kern = functools.partial(_gmlp_sample_kernel, n_t=n_t)
    smem = pl.BlockSpec(memory_space=pltpu.SMEM)
    return pl.pallas_call(
        kern,
        grid=(1,),
        in_specs=[smem, smem, _full(x.shape), _full((1, d)), _full(w_in.shape), _full((1, dg)),
                  _full((1, dg)), _full(w_out.shape)],
        out_specs=[_full((n, d)), _full((n, dg))],
        out_shape=[jax.ShapeDtypeStruct((n, d), F32), jax.ShapeDtypeStruct((n, dg), F32)],
        compiler_params=_params("arbitrary"),
        name="gmlp_sample",
    )(ws_flat, bs_flat, x, g, w_in, ln_g, ln_b, w_out)


ROUTE_LANES = 128
SLOT_ALIGN = 16
MOE_SUB = 256
ROUTE_ROWS = 32


def _router_kernel(x_ref, g_ref, rw_ref, rb_ref, tri_ref, sel_ref, selt_ref, *refs, n_cast):
    hn_ref, col_ref, row_ref, meta_ref = refs[n_cast:n_cast + 4]
    for src_ref, dst_ref in zip(refs[:n_cast], refs[n_cast + 4:]):
        dst_ref[...] = src_ref[...].astype(BF16)
    ne = MOE_GROUPS * MOE_PER_GROUP
    hb = _rms(x_ref[...], g_ref[...]).astype(BF16)
    hn_ref[...] = hb
    lt = (_dot(hb, rw_ref[...]) + rb_ref[...]).T[:ROUTE_ROWS]
    tm = lt.shape[1]
    ridx = lax.broadcasted_iota(jnp.int32, lt.shape, 0)
    lg = jnp.where((ridx >= ne) & (ridx < ne + MOE_GROUPS), lt, -jnp.inf)
    g_max = jnp.max(lg, axis=0, keepdims=True)
    g_idx = jnp.min(jnp.where(lg == g_max, ridx, ROUTE_ROWS), axis=0, keepdims=True) - ne
    g_p = 1.0 / jnp.sum(jnp.exp(lg - g_max), axis=0, keepdims=True)
    first = g_idx * MOE_PER_GROUP
    l1 = jnp.where((ridx >= first) & (ridx < first + MOE_PER_GROUP), lt, -jnp.inf)
    m1 = jnp.max(l1, axis=0, keepdims=True)
    e1 = jnp.min(jnp.where(l1 == m1, ridx, ROUTE_ROWS), axis=0, keepdims=True)
    l2 = jnp.where(ridx == e1, -jnp.inf, l1)
    m2 = jnp.max(l2, axis=0, keepdims=True)
    e2 = jnp.min(jnp.where(l2 == m2, ridx, ROUTE_ROWS), axis=0, keepdims=True)
    z = jnp.sum(jnp.exp(l1 - m1), axis=0, keepdims=True)
    p1 = 1.0 / z
    p2 = jnp.exp(m2 - m1) / z
    w1 = g_p * p1 / (p1 + p2)
    w2 = g_p * p2 / (p1 + p2)
    hit1 = ridx == e1
    hit2 = ridx == e2
    routed = (jnp.where(hit1, 1.0, 0.0) + jnp.where(hit2, 1.0, 0.0)).astype(BF16)
    rank = _dot(routed, tri_ref[...])
    cnt = _dot(routed, sel_ref[...])
    pad = jnp.floor((cnt + (SLOT_ALIGN - 1)) * (1.0 / SLOT_ALIGN)) * SLOT_ALIGN
    ridx_c = lax.broadcasted_iota(jnp.int32, cnt.shape, 0)
    incl = pad
    shift = 1
    while shift < ne:
        incl = incl + jnp.where(ridx_c >= shift, pltpu.roll(incl, shift, axis=0), 0.0)
        shift *= 2
    off = incl - pad
    slot = rank + _dot(off.astype(BF16), selt_ref[...])
    s1 = jnp.sum(jnp.where(hit1, slot, 0.0), axis=0, keepdims=True)
    s2 = jnp.sum(jnp.where(hit2, slot, 0.0), axis=0, keepdims=True)
    rows = jnp.concatenate([s1, s2, w1, w2, jnp.zeros((row_ref.shape[0] - 4, tm), F32)], axis=0)
    row_ref[...] = rows
    col_ref[...] = jnp.concatenate([rows, jnp.zeros((ROUTE_LANES - rows.shape[0], tm), F32)], axis=0).T
    meta_ref[0] = cnt.astype(jnp.int32)
    meta_ref[1] = off.astype(jnp.int32)


def _router(x, g, rw, rb, tm, sub, layer, to_cast):
    n, d = x.shape
    nt = n // tm
    cast_in, cast_out, cast_shape = [], [], []
    for w in to_cast:
        per_step = w.shape[1] // nt
        assert per_step * nt == w.shape[1]
        cast_in.append(pl.BlockSpec((None, per_step) + w.shape[2:], lambda i: (layer, i, 0, 0)))
        cast_out.append(pl.BlockSpec((per_step,) + w.shape[2:], lambda i: (i, 0, 0)))
        cast_shape.append(jax.ShapeDtypeStruct(w.shape[1:], BF16))
    assert (2 * sub) // SLOT_ALIGN + MOE_GROUPS * MOE_PER_GROUP < 256
    tok = jnp.arange(tm)
    blocks = tok // sub
    tri = ((blocks[:, None] == blocks[None, :]) & (tok[:, None] < tok[None, :])).astype(BF16)
    sel = (blocks[:, None] == jnp.arange(ROUTE_LANES)[None, :]).astype(BF16)
    out = pl.pallas_call(
        functools.partial(_router_kernel, n_cast=len(to_cast)),
        grid=(nt,),
        in_specs=[pl.BlockSpec((tm, d), lambda i: (i, 0)), _full((1, d)), _full(rw.shape),
                  _full(rb.shape), _full(tri.shape), _full(sel.shape), _full(sel.T.shape)] + cast_in,
        out_specs=[pl.BlockSpec((tm, d), lambda i: (i, 0)),
                   pl.BlockSpec((tm, ROUTE_LANES), lambda i: (i, 0)),
                   pl.BlockSpec((8, tm), lambda i: (0, i)),
                   pl.BlockSpec((None, 2, ROUTE_ROWS, ROUTE_LANES), lambda i: (i, 0, 0, 0))] + cast_out,
        out_shape=[jax.ShapeDtypeStruct((n, d), BF16), jax.ShapeDtypeStruct((n, ROUTE_LANES), F32),
                   jax.ShapeDtypeStruct((8, n), F32),
                   jax.ShapeDtypeStruct((nt, 2, ROUTE_ROWS, ROUTE_LANES), jnp.int32)] + cast_shape,
        compiler_params=_params("arbitrary"),
        name="moe_router",
    )(x, g, rw, rb, tri, sel, sel.T, *to_cast)
    return out[:4], out[4:]


def _moe_routed_kernel(cnt_ref, off_ref, trips_ref, hn_ref, col_ref, row_ref, x_ref, gfin_ref,
                       wg_ref, wu_ref, wd_ref, y_ref, xs_ref, ys_ref, gs_ref,
                       *, sub, mcs, n_gather, n_scatter, final_norm):
    i = pl.program_id(0)
    eg = pl.program_id(1)
    eps = wg_ref.shape[0]
    ne = pl.num_programs(1) * eps
    n_sub = xs_ref.shape[0]
    dump = n_gather
    rows_of = lambda s: slice(s * sub, (s + 1) * sub)

    @pl.when(eg == 0)
    def _():
        y_ref[...] = x_ref[...]
        ys_ref[...] = jnp.zeros_like(ys_ref)
        sid = lax.broadcasted_iota(jnp.int32, (n_gather, sub), 0).astype(F32)
        for s in range(n_sub):
            s1, s2, w1, w2 = (row_ref[r:r + 1, rows_of(s)] for r in range(4))
            h1 = s1 == sid
            h2 = s2 == sid
            onehot = (jnp.where(h1, 1.0, 0.0) + jnp.where(h2, 1.0, 0.0)).astype(BF16)
            xs_ref[s, 0:n_gather, :] = _dot(onehot, hn_ref[rows_of(s), :]).astype(BF16)
            xs_ref[s, dump:dump + mcs, :] = jnp.zeros((mcs, xs_ref.shape[2]), BF16)
            gs_ref[s, 0:n_gather, :] = jnp.sum(
                jnp.where(h1, w1, 0.0) + jnp.where(h2, w2, 0.0), axis=-1, keepdims=True)
            gs_ref[s, dump:dump + mcs, :] = jnp.zeros((mcs, 1), F32)

    for r in range(eps):
        e = eg * eps + r
        counts = [cnt_ref[(i * n_sub + s) * ne + e] for s in range(n_sub)]
        firsts = [off_ref[(i * n_sub + s) * ne + e] for s in range(n_sub)]

        def chunk(k, carry, r=r, counts=counts, firsts=firsts):
            rows = [pl.multiple_of(jnp.where(k * mcs < counts[s], firsts[s] + k * mcs, dump), SLOT_ALIGN)
                    for s in range(n_sub)]
            xg = jnp.concatenate([xs_ref[s, pl.ds(rows[s], mcs), :] for s in range(n_sub)], axis=0)
            gate = jnp.concatenate([gs_ref[s, pl.ds(rows[s], mcs), :] for s in range(n_sub)], axis=0)
            a = _dot(xg, wg_ref[r])
            b = _dot(xg, wu_ref[r])
            hid = ((a * jax.nn.sigmoid(a)) * b * gate).astype(BF16)
            out = _dot(hid, wd_ref[r]).astype(BF16)
            for s in range(n_sub):
                ys_ref[s, pl.ds(rows[s], mcs), :] = out[s * mcs:(s + 1) * mcs]
            return carry

        lax.fori_loop(0, trips_ref[i * ne + e], chunk, 0)

    @pl.when(eg == pl.num_programs(1) - 1)
    def _():
        sid = lax.broadcasted_iota(jnp.int32, (sub, n_scatter), 1).astype(F32)
        for s in range(n_sub):
            onehot = (jnp.where(col_ref[rows_of(s), 0:1] == sid, 1.0, 0.0)
                      + jnp.where(col_ref[rows_of(s), 1:2] == sid, 1.0, 0.0)).astype(BF16)
            out = y_ref[rows_of(s), :] + _dot(onehot, ys_ref[s, 0:n_scatter, :])
            y_ref[rows_of(s), :] = _rms(out, gfin_ref[...]) if final_norm else out


def _round_up(a, b):
    return -(-a // b) * b


def _moe_routed(cnt, off, hn, col, row, wg, wu, wd, x, g_final, tm, sub):
    n, d = x.shape
    ne, _, de = wg.shape
    n_sub = tm // sub
    mcs = _round_up(2 * sub * 3 // (ne * 2), SLOT_ALIGN)
    most = 2 * sub + (ne - 1) * SLOT_ALIGN
    n_scatter = _round_up(most, 128)
    n_gather = _round_up(most + mcs, 128)
    trips = jnp.max((cnt + mcs - 1) // mcs, axis=1).reshape(-1)
    eps = 2 if ne % 2 == 0 else 1
    final_norm = g_final is not None
    if g_final is None:
        g_final = jnp.ones((1, d), F32)
    kern = functools.partial(_moe_routed_kernel, sub=sub, mcs=mcs, n_gather=n_gather, n_scatter=n_scatter,
                             final_norm=final_norm)
    tile = lambda i, e, *_: (i, 0)
    expert = lambda i, e, *_: (e, 0, 0)
    return pl.pallas_call(
        kern,
        grid_spec=pltpu.PrefetchScalarGridSpec(
            num_scalar_prefetch=3,
            grid=(n // tm, ne // eps),
            in_specs=[pl.BlockSpec((tm, d), tile), pl.BlockSpec((tm, ROUTE_LANES), tile),
                      pl.BlockSpec((8, tm), lambda i, e, *_: (0, i)), pl.BlockSpec((tm, d), tile),
                      pl.BlockSpec((1, d), lambda i, e, *_: (0, 0)),
                      pl.BlockSpec((eps, d, de), expert), pl.BlockSpec((eps, d, de), expert),
                      pl.BlockSpec((eps, de, d), expert)],
            out_specs=pl.BlockSpec((tm, d), tile),
            scratch_shapes=[pltpu.VMEM((n_sub, n_gather + mcs, d), BF16),
                            pltpu.VMEM((n_sub, n_gather + mcs, d), BF16),
                            pltpu.VMEM((n_sub, n_gather + mcs, 1), F32)],
        ),
        out_shape=jax.ShapeDtypeStruct((n, d), F32),
        compiler_params=_params("arbitrary", "arbitrary"),
        name="moe_routed",
    )(cnt.reshape(-1), off.reshape(-1), trips, hn, col, row, x, g_final, wg, wu, wd)


def _moe(x, g, rw, rb, weights, layer, g_final, tm):
    sub = min(tm, MOE_SUB)
    n_sub = tm // sub
    assert tm % sub == 0 and n_sub <= ROUTE_LANES
    to_cast = [w for w in weights if w.dtype != BF16]
    if to_cast and any(w.shape[1] % (x.shape[0] // tm) for w in to_cast):
        weights, to_cast = [w[layer].astype(BF16) for w in weights], []
    (hn, col, row, meta), cast = _router(x, g, rw, rb, tm, sub, layer, to_cast)
    wg, wu, wd = cast if to_cast else weights
    ne = wg.shape[0]
    cnt, off = (meta[:, r, :ne, :n_sub].transpose(0, 2, 1) for r in range(2))
    return _moe_routed(cnt, off, hn, col, row, wg, wu, wd, x, g_final, tm, sub), (wg, wu, wd)


def _tile(n, pref):
    t = min(n, pref)
    assert n % t == 0, (n, t)
    return t


def kernel(x_prompt, x_sample, state_pool, cache_k, cache_v, page_table, norm_mix, norm_ffn, norm_final, pool_w_in, pool_w_grp, pool_scale, pool_w_out, attn_w_qkv, attn_w_out, gmlp_w_in, gmlp_ln_g, gmlp_ln_b, gmlp_w_s, gmlp_b_s, gmlp_w_out, moe_rg_w, moe_rg_b, moe_re_w, moe_re_b, moe_w_gate, moe_w_up, moe_w_down):
    batch, seq, d = x_prompt.shape
    dec_b, n_t, _ = x_sample.shape
    depth = norm_mix.shape[0]
    n_heads = N_HEADS
    hd = d // n_heads
    n_pages = page_table.shape[1]
    past_len = n_pages * PAGE_SIZE
    assert seq % MOBA_BLOCK == 0 and seq % GMLP_CHUNK == 0 and n_t <= GMLP_CHUNK
    assert past_len % GMLP_CHUNK == 0 and past_len >= POOL_STATE

    n_p, n_s = batch * seq, dec_b * n_t
    xp = x_prompt.reshape(n_p, d)
    xs = x_sample.transpose(1, 0, 2).reshape(n_s, d)
    tp = _tile(seq, 512)
    row = lambda a: a.reshape(1, -1)

    def to_batch_major(a):
        return a.reshape((n_t, dec_b) + a.shape[1:]).swapaxes(0, 1)

    pool_p, pool_s, k_p, v_p, k_s, v_s, gv_p, gv_s = [], [], [], [], [], [], [], []
    for i in range(depth):
        kind, j = i % N_MIXERS, i // N_MIXERS
        g_mix = row(norm_mix[i])
        if kind == 0:
            w_in, w_grp, w_out = (a.astype(BF16) for a in (pool_w_in[j], pool_w_grp[j], pool_w_out[j]))
            scale = row(pool_scale[j])
            xp, st = _pool_prompt(xp, g_mix, w_in, w_grp, scale, w_out, batch, seq, tp)
            pool_p.append(st)
            xs, st = _pool_sample(xs, state_pool, j, g_mix, w_in, w_grp, scale, w_out, n_t, past_len)
            pool_s.append(st)
        elif kind == 1:
            w_qkv, w_out = attn_w_qkv[j].astype(BF16), attn_w_out[j].astype(BF16)
            qp, kp, vp = _norm_matmul(xp, g_mix, w_qkv, (BF16, F32, F32), _tile(n_p, 1024))
            qs, ks, vs = _norm_matmul(xs, g_mix, w_qkv, (F32, F32, F32), _tile(n_s, 512))
            ap = _moba_prompt(qp, kp, vp, batch, seq, n_heads)
            a_s = _moba_sample(qs, ks, vs, cache_k, cache_v, j, page_table, n_t, n_heads)
            xp = _matmul_residual(ap, w_out, xp, _tile(n_p, 1024))
            xs = _matmul_residual(a_s, w_out, xs, _tile(n_s, 512))
            k_p.append(kp.reshape(batch, seq, n_heads, hd))
            v_p.append(vp.reshape(batch, seq, n_heads, hd))
            k_s.append(to_batch_major(ks).reshape(dec_b, n_t, n_heads, hd))
            v_s.append(to_batch_major(vs).reshape(dec_b, n_t, n_heads, hd))
        else:
            w_in, w_out = gmlp_w_in[j].astype(BF16), gmlp_w_out[j].astype(BF16)
            ln_g, ln_b = row(gmlp_ln_g[j]), row(gmlp_ln_b[j])
            gc = w_out.shape[0] // GMLP_GROUPS
            bias = jnp.broadcast_to(gmlp_b_s[j].T[:, :, None], (GMLP_CHUNK, GMLP_GROUPS, gc))
            xp, vn_p = _gmlp_prompt(xp, g_mix, w_in, ln_g, ln_b, gmlp_w_s[j],
                                    bias.reshape(GMLP_CHUNK, GMLP_GROUPS * gc), w_out, tp)
            xs, vn_s = _gmlp_sample(xs, g_mix, w_in, ln_g, ln_b,
                                    gmlp_w_s[j][:, :n_t, :n_t].reshape(-1),
                                    gmlp_b_s[j][:, :n_t].reshape(-1), w_out, n_t)
            gv_p.append(vn_p.reshape(batch, seq, -1))
            gv_s.append(to_batch_major(vn_s))

        g_ffn = row(norm_ffn[i])
        ne = moe_re_w.shape[2]
        fill = jnp.zeros((d, ROUTE_LANES - ne - MOE_GROUPS), F32)
        rw = jnp.concatenate([moe_re_w[i], moe_rg_w[i], fill], axis=1).astype(BF16)
        rb = jnp.concatenate([moe_re_b[i], moe_rg_b[i], fill[0]]).reshape(1, ROUTE_LANES)
        g_fin = row(norm_final) if i == depth - 1 else None
        xp, experts = _moe(xp, g_ffn, rw, rb, (moe_w_gate, moe_w_up, moe_w_down), i, g_fin, _tile(n_p, 1024))
        xs, _ = _moe(xs, g_ffn, rw, rb, experts, i, g_fin, _tile(n_s, 512))

    y_p = xp.reshape(batch, seq, d)
    y_s = to_batch_major(xs)
    return (y_p, y_s, jnp.stack(pool_p), jnp.stack(pool_s), jnp.stack(k_p), jnp.stack(v_p),
            jnp.stack(k_s), jnp.stack(v_s), jnp.stack(gv_p), jnp.stack(gv_s))
```

```python
import functools

import jax
import jax.numpy as jnp
from jax import lax
from jax.experimental import pallas as pl
from jax.experimental.pallas import tpu as pltpu

F32 = jnp.float32
BF16 = jnp.bfloat16

RMS_EPS = 1e-6
LN_EPS = 1e-5
N_MIXERS = 3
POOL_WINDOWS = (2, 4, 8, 16)
POOL_STATE = max(POOL_WINDOWS) - 1
MOBA_BLOCK = 256
MOBA_TOPK = 3
GMLP_CHUNK = 128
GMLP_GROUPS = 8
MOE_GROUPS = 4
MOE_PER_GROUP = 4
PAGE_SIZE = 128
N_HEADS = 8

V7X_VMEM_LIMIT_BYTES = 56 * 1024 * 1024
NEG_BIG = -1e30
LOG2_E = 1.4426950408889634


def _params(*sem):
    return pltpu.CompilerParams(dimension_semantics=sem, vmem_limit_bytes=V7X_VMEM_LIMIT_BYTES)


def _rms(x, g):
    return x * lax.rsqrt(jnp.mean(x * x, axis=-1, keepdims=True) + RMS_EPS) * g


def _dot(a, b):
    return jnp.dot(a, b, preferred_element_type=F32)


def _dot_nt(a, b, precision=None):
    return lax.dot_general(a, b, (((1,), (1,)), ((), ())), preferred_element_type=F32,
                           precision=precision)


def _full(shape):
    return pl.BlockSpec(shape, lambda *_: (0,) * len(shape))


def _norm_matmul_kernel(x_ref, g_ref, w_ref, *out_refs):
    h = _rms(x_ref[...], g_ref[...]).astype(BF16)
    f = w_ref.shape[1] // len(out_refs)
    for i, o_ref in enumerate(out_refs):
        o_ref[...] = _dot(h, w_ref[:, i * f:(i + 1) * f]).astype(o_ref.dtype)


def _norm_matmul(x, g, w, out_dtypes, tm):
    n, d = x.shape
    n_out = len(out_dtypes)
    f = w.shape[1] // n_out
    return pl.pallas_call(
        _norm_matmul_kernel,
        grid=(n // tm,),
        in_specs=[pl.BlockSpec((tm, d), lambda i: (i, 0)), _full((1, d)), _full(w.shape)],
        out_specs=[pl.BlockSpec((tm, f), lambda i: (i, 0))] * n_out,
        out_shape=[jax.ShapeDtypeStruct((n, f), dt) for dt in out_dtypes],
        compiler_params=_params("parallel"),
        name="norm_matmul",
    )(x, g, w)


def _matmul_residual_kernel(a_ref, w_ref, x_ref, y_ref):
    y_ref[...] = x_ref[...] + _dot(a_ref[...].astype(BF16), w_ref[...])


def _matmul_residual(a, w, x, tm):
    n, d = x.shape
    k = a.shape[1]
    return pl.pallas_call(
        _matmul_residual_kernel,
        grid=(n // tm,),
        in_specs=[pl.BlockSpec((tm, k), lambda i: (i, 0)), _full(w.shape),
                  pl.BlockSpec((tm, d), lambda i: (i, 0))],
        out_specs=pl.BlockSpec((tm, d), lambda i: (i, 0)),
        out_shape=jax.ShapeDtypeStruct((n, d), F32),
        compiler_params=_params("parallel"),
        name="matmul_residual",
    )(a, w, x)


def _pool_tail(r, x, wgrp_ref, scale_ref, wout_ref):
    gc = wgrp_ref.shape[1]
    rb = r.astype(BF16)
    t = jnp.concatenate([_dot(rb[:, g * gc:(g + 1) * gc], wgrp_ref[g])
                         for g in range(len(POOL_WINDOWS))], axis=-1)
    t = (t * scale_ref[...]).astype(BF16)
    return x + _dot(t, wout_ref[...])


def _pool_prompt_kernel(x_ref, g_ref, win_ref, wgrp_ref, scale_ref, wout_ref,
                        y_ref, st_ref, carry_ref, *, ts, ns):
    s = pl.program_id(1)
    halo = carry_ref.shape[0]
    gc = wgrp_ref.shape[1]

    @pl.when(s == 0)
    def _():
        carry_ref[...] = jnp.zeros_like(carry_ref)

    x = x_ref[...]
    h = _rms(x, g_ref[...]).astype(BF16)
    u = _dot(h, win_ref[...])
    ext = jnp.concatenate([carry_ref[...], u], axis=0)
    pos = s * ts + lax.broadcasted_iota(jnp.int32, (ts, 1), 0)
    parts = []
    acc = ext
    width = 1
    for w in POOL_WINDOWS:
        while width < w:
            acc = acc + pltpu.roll(acc, width, axis=0)
            width *= 2
        assert width == w
        cnt = jnp.minimum(pos + 1, w).astype(F32)
        parts.append(acc[halo:, :gc] / cnt)
        acc = acc[:, gc:]
    r = jnp.concatenate(parts, axis=-1) - u
    y_ref[...] = _pool_tail(r, x, wgrp_ref, scale_ref, wout_ref)
    carry_ref[...] = u[ts - halo:, :]

    @pl.when(s == ns - 1)
    def _():
        st_ref[...] = carry_ref[halo - POOL_STATE:, :]


def _pool_prompt(x, g, w_in, w_grp, scale, w_out, batch, seq, ts):
    n, d = x.shape
    ns = seq // ts
    halo = 16
    kern = functools.partial(_pool_prompt_kernel, ts=ts, ns=ns)
    return pl.pallas_call(
        kern,
        grid=(batch, ns),
        in_specs=[pl.BlockSpec((ts, d), lambda b, s: (b * ns + s, 0)), _full((1, d)),
                  _full(w_in.shape), _full(w_grp.shape), _full((1, d)), _full(w_out.shape)],
        out_specs=[pl.BlockSpec((ts, d), lambda b, s: (b * ns + s, 0)),
                   pl.BlockSpec((None, POOL_STATE, d), lambda b, s: (b, 0, 0))],
        out_shape=[jax.ShapeDtypeStruct((n, d), F32),
                   jax.ShapeDtypeStruct((batch, POOL_STATE, d), F32)],
        scratch_shapes=[pltpu.VMEM((halo, d), F32)],
        compiler_params=_params("parallel", "arbitrary"),
        name="pool_prompt",
    )(x, g, w_in, w_grp, scale, w_out)


def _pool_sample_kernel(x_ref, pref_ref, g_ref, win_ref, wgrp_ref, scale_ref, wout_ref,
                        y_ref, st_ref, *, n_t, pos0):
    bd = x_ref.shape[0] // n_t
    gc = wgrp_ref.shape[1]
    x = x_ref[...]
    h = _rms(x, g_ref[...]).astype(BF16)
    u = _dot(h, win_ref[...])
    us = [u[t * bd:(t + 1) * bd] for t in range(n_t)]
    for r in range(POOL_STATE):
        st_ref[:, r, :] = pref_ref[:, r + n_t, :] if r + n_t < POOL_STATE else us[r + n_t - POOL_STATE]
    tail = [jnp.zeros_like(us[0])]
    for m in range(1, POOL_STATE + 1):
        tail.append(tail[-1] + pref_ref[:, POOL_STATE - m, :])
    rows = []
    for t in range(n_t):
        parts = []
        for gi, w in enumerate(POOL_WINDOWS):
            sl = slice(gi * gc, (gi + 1) * gc)
            n_new = min(w, t + 1)
            wsum = us[t][:, sl]
            for k in range(1, n_new):
                wsum = wsum + us[t - k][:, sl]
            if w - n_new:
                wsum = wsum + tail[w - n_new][:, sl]
            parts.append(wsum / float(min(pos0 + t + 1, w)))
        rows.append(jnp.concatenate(parts, axis=-1) - us[t])
    r = jnp.concatenate(rows, axis=0)
    y_ref[...] = _pool_tail(r, x, wgrp_ref, scale_ref, wout_ref)


def _pool_sample(x, state, layer, g, w_in, w_grp, scale, w_out, n_t, pos0):
    n, d = x.shape
    assert n_t <= POOL_STATE
    st_shape = state.shape[1:]
    kern = functools.partial(_pool_sample_kernel, n_t=n_t, pos0=pos0)
    return pl.pallas_call(
        kern,
        grid=(1,),
        in_specs=[_full(x.shape), pl.BlockSpec((None,) + st_shape, lambda i: (layer, 0, 0, 0)),
                  _full((1, d)), _full(w_in.shape), _full(w_grp.shape), _full((1, d)), _full(w_out.shape)],
        out_specs=[_full((n, d)), _full(st_shape)],
        out_shape=[jax.ShapeDtypeStruct((n, d), F32), jax.ShapeDtypeStruct(st_shape, F32)],
        compiler_params=_params("arbitrary"),
        name="pool_sample",
    )(x, state, g, w_in, w_grp, scale, w_out)


def _top_blocks(gate, n_valid, n_blocks):
    idx = lax.broadcasted_iota(jnp.int32, gate.shape, 0)
    valid = idx < n_valid
    rows = []
    for j in range(n_blocks):
        gj = gate[j:j + 1, :]
        beats = valid & ((gate > gj) | ((gate == gj) & (idx < j)))
        rank = jnp.sum(beats.astype(F32), axis=0, keepdims=True)
        rows.append(jnp.where(rank < MOBA_TOPK, jnp.where(j < n_valid, 1.0, 0.0), 0.0))
    return rows


def _moba_prompt_kernel(q_ref, k_ref, v_ref, o_ref, kb_ref, vt_ref, kmean_ref, sel_ref, s_ref,
                        *, nb, group, hps, hd):
    qi = pl.program_id(2)
    blk = MOBA_BLOCK
    scale = hd ** -0.5
    heads = range(hps)
    lanes = lambda h: slice(h * hd, (h + 1) * hd)

    @pl.when(qi == 0)
    def _():
        k = k_ref[...]
        kb_ref[...] = k.astype(BF16)
        kmean_ref[...] = jnp.mean(k.reshape(nb, blk, hps * hd), axis=1)
        vt_ref[...] = v_ref[...].T.astype(BF16)

    q = q_ref[...]
    qb = [q[:, lanes(h)].astype(BF16) for h in heads]
    for h in heads:
        gate = _dot_nt(kmean_ref[:, lanes(h)].astype(BF16), qb[h])
        for j, row in enumerate(_top_blocks(gate, qi, nb)):
            sel_ref[h * nb + j:h * nb + j + 1, :] = row
    key_i = lax.broadcasted_iota(jnp.int32, (blk, blk), 0)
    qry_i = lax.broadcasted_iota(jnp.int32, (blk, blk), 1)
    scale2 = scale * LOG2_E

    def scores(h, j, n):
        kj = kb_ref[pl.ds(pl.multiple_of(j * blk, blk), n * blk), lanes(h)]
        return _dot_nt(kj, qb[h]) * scale2

    def pv(h, j, n, p):
        vj = vt_ref[lanes(h), pl.ds(pl.multiple_of(j * blk, blk), n * blk)]
        return _dot(vj, p.astype(BF16))

    n_trips = qi // group + 1
    causal = jnp.where(key_i <= qry_i, 1.0, 0.0)

    def score_pass(g, ms):
        j = g * group
        out = []
        for h, m in zip(heads, ms):
            keep = jnp.concatenate(
                [jnp.where(j + r == qi, causal,
                           jnp.broadcast_to(sel_ref[pl.ds(h * nb + j + r, 1), :], (blk, blk)))
                 for r in range(group)], axis=0)
            s = jnp.where(keep > 0.0, scores(h, j, group), NEG_BIG)
            s_ref[h, pl.ds(pl.multiple_of(j * blk, blk), group * blk), :] = s
            out.append(jnp.maximum(m, jnp.max(s, axis=0, keepdims=True)))
        return tuple(out)

    ms = lax.fori_loop(0, n_trips, score_pass, tuple(jnp.full((1, blk), NEG_BIG, F32) for _ in heads))
    carry = [(jnp.zeros((1, blk), F32), jnp.zeros((hd, blk), F32)) for _ in heads]

    def value_pass(g, carry):
        j = g * group
        out = []
        for h, (l, acc) in zip(heads, carry):
            p = jnp.exp2(s_ref[h, pl.ds(pl.multiple_of(j * blk, blk), group * blk), :] - ms[h])
            out.append((l + jnp.sum(p, axis=0, keepdims=True), acc + pv(h, j, group, p)))
        return tuple(out)

    carry = lax.fori_loop(0, n_trips, value_pass, tuple(carry))
    for h, (l, acc) in zip(heads, carry):
        o_ref[:, lanes(h)] = (acc / l).T.astype(o_ref.dtype)


def _moba_prompt(q, k, v, batch, seq, n_heads):
    n, d = q.shape
    hd = d // n_heads
    nb = seq // MOBA_BLOCK
    group = 4 if nb % 4 == 0 else 1
    hps = 2 if n_heads % 2 == 0 else 1
    kern = functools.partial(_moba_prompt_kernel, nb=nb, group=group, hps=hps, hd=hd)
    return pl.pallas_call(
        kern,
        grid=(batch, n_heads // hps, nb),
        in_specs=[pl.BlockSpec((MOBA_BLOCK, hps * hd), lambda b, h, i: (b * nb + i, h)),
                  pl.BlockSpec((seq, hps * hd), lambda b, h, i: (b, h)),
                  pl.BlockSpec((seq, hps * hd), lambda b, h, i: (b, h))],
        out_specs=pl.BlockSpec((MOBA_BLOCK, hps * hd), lambda b, h, i: (b * nb + i, h)),
        out_shape=jax.ShapeDtypeStruct((n, d), BF16),
        scratch_shapes=[pltpu.VMEM((seq, hps * hd), BF16), pltpu.VMEM((hps * hd, seq), BF16),
                        pltpu.VMEM((nb, hps * hd), F32), pltpu.VMEM((hps * nb, MOBA_BLOCK), F32),
                        pltpu.VMEM((hps, nb * MOBA_BLOCK, MOBA_BLOCK), F32)],
        compiler_params=_params("parallel", "parallel", "arbitrary"),
        name="moba_prompt",
    )(q, k, v)


def _moba_sample_kernel(pt_ref, q_ref, kn_ref, vn_ref, *refs, n_t, n_pages, pps):
    del pt_ref
    kp_refs, vp_refs = refs[:pps], refs[pps:2 * pps]
    o_ref, wbd_ref, s_ref, vall_ref = refs[2 * pps:]
    p_i = pl.program_id(1)
    nh, hd = q_ref.shape[2], q_ref.shape[3]
    lanes = wbd_ref.shape[1]
    scale = hd ** -0.5
    n_past = n_pages // (MOBA_BLOCK // PAGE_SIZE)
    head_i = lax.broadcasted_iota(jnp.int32, (nh, hd), 0)

    @pl.when(p_i == 0)
    def _():
        rows = [jnp.concatenate([jnp.where(head_i == h, q_ref[t, 0], 0.0) for h in range(nh)], axis=-1)
                for t in range(n_t)]
        rows.append(jnp.zeros((lanes - n_t * nh, nh * hd), F32))
        wbd_ref[...] = jnp.concatenate(rows, axis=0).T.astype(BF16)

    for g, (kp_ref, vp_ref) in enumerate(zip(kp_refs, vp_refs)):
        off = pl.multiple_of((p_i * pps + g) * PAGE_SIZE, PAGE_SIZE)
        acc = None
        for h in range(nh):
            kh = kp_ref[pl.ds(h, PAGE_SIZE, stride=nh), :].astype(BF16)
            part = _dot(kh, wbd_ref[h * hd:(h + 1) * hd, :])
            acc = part if acc is None else acc + part
            vall_ref[pl.ds(off, PAGE_SIZE), h * hd:(h + 1) * hd] = (
                vp_ref[pl.ds(h, PAGE_SIZE, stride=nh), :].astype(BF16))
        s_ref[pl.ds(off, PAGE_SIZE), :] = acc

    @pl.when(p_i == n_pages // pps - 1)
    def _():
        s_all = s_ref[...]
        blocks = [s_all[j * MOBA_BLOCK:(j + 1) * MOBA_BLOCK] for j in range(n_past)]
        gate = [jnp.sum(b, axis=0, keepdims=True) * (1.0 / MOBA_BLOCK) for b in blocks]
        sel = []
        for j in range(n_past):
            rank = jnp.zeros_like(gate[j])
            for i in range(n_past):
                if i != j:
                    beats = (gate[i] >= gate[j]) if i < j else (gate[i] > gate[j])
                    rank = rank + beats.astype(F32)
            sel.append(rank < MOBA_TOPK)
        s_past = jnp.concatenate(
            [jnp.where(sel[j], blocks[j] * scale, NEG_BIG) for j in range(n_past)], axis=0)
        m_row = jnp.max(s_past, axis=0, keepdims=True)
        p_past = jnp.exp(s_past - m_row)
        l_row = jnp.sum(p_past, axis=0, keepdims=True)
        full = _dot(p_past.T.astype(BF16), vall_ref[...])
        stats = jnp.concatenate([m_row, l_row, jnp.zeros((lanes - 2, lanes), F32)], axis=0).T
        for t in range(n_t):
            rows = full[t * nh:(t + 1) * nh]
            out = functools.reduce(
                lambda a, b: a + b,
                [jnp.where(head_i == h, rows[:, h * hd:(h + 1) * hd], 0.0) for h in range(nh)])
            m_past = stats[t * nh:(t + 1) * nh, 0:1]
            l_past = stats[t * nh:(t + 1) * nh, 1:2]
            s_new = [jnp.sum(q_ref[t, 0] * kn_ref[t2, 0], axis=-1, keepdims=True) * scale
                     for t2 in range(t + 1)]
            m = functools.reduce(jnp.maximum, s_new, m_past)
            alpha = jnp.exp(m_past - m)
            out = alpha * out
            l = alpha * l_past
            for t2 in range(t + 1):
                p_new = jnp.exp(s_new[t2] - m)
                l = l + p_new
                out = out + p_new * vn_ref[t2, 0]
            o_ref[t, 0] = out / l


def _moba_sample(q, k, v, cache_k, cache_v, layer, page_table, n_t, n_heads):
    n, d = q.shape
    bd = n // n_t
    hd = d // n_heads
    n_pages = page_table.shape[1]
    past = n_pages * PAGE_SIZE
    lanes = 128
    assert n_pages % (MOBA_BLOCK // PAGE_SIZE) == 0 and n_t <= MOBA_BLOCK and n_t * n_heads <= lanes - 2
    q4, k4, v4 = (a.reshape(n_t, bd, n_heads, hd) for a in (q, k, v))
    n_layers, n_phys = cache_k.shape[:2]
    ck, cv = (c.reshape(n_layers, n_phys, PAGE_SIZE * n_heads, hd) for c in (cache_k, cache_v))
    new_spec = pl.BlockSpec((n_t, 1, n_heads, hd), lambda b, p, pt: (0, b, 0, 0))
    pps = next(p for p in (16, 8, 2) if n_pages % p == 0)
    page_specs = [pl.BlockSpec((None, None, PAGE_SIZE * n_heads, hd),
                               functools.partial(lambda g, b, p, pt: (layer, pt[b, p * pps + g], 0, 0), g))
                  for g in range(pps)]
    kern = functools.partial(_moba_sample_kernel, n_t=n_t, n_pages=n_pages, pps=pps)
    out = pl.pallas_call(
        kern,
        grid_spec=pltpu.PrefetchScalarGridSpec(
            num_scalar_prefetch=1,
            grid=(bd, n_pages // pps),
            in_specs=[new_spec, new_spec, new_spec] + page_specs + page_specs,
            out_specs=new_spec,
            scratch_shapes=[pltpu.VMEM((d, lanes), BF16), pltpu.VMEM((past, lanes), F32),
                            pltpu.VMEM((past, d), BF16)],
        ),
        out_shape=jax.ShapeDtypeStruct((n_t, bd, n_heads, hd), F32),
        compiler_params=_params("parallel", "arbitrary"),
        name="moba_sample",
    )(page_table, q4, k4, v4, *([ck] * pps), *([cv] * pps))
    return out.reshape(n, d)


def _gelu(z):
    return 0.5 * z * (1.0 + jnp.tanh(0.7978845608028654 * (z + 0.044715 * z * z * z)))


def _gmlp_front(x, g_ref, win_ref, lng_ref, lnb_ref):
    h = _rms(x, g_ref[...]).astype(BF16)
    z = _gelu(_dot(h, win_ref[...]))
    dg = z.shape[1] // 2
    u, v = z[:, :dg], z[:, dg:]
    mu = jnp.mean(v, axis=-1, keepdims=True)
    var = jnp.mean(jnp.square(v - mu), axis=-1, keepdims=True)
    vn = (v - mu) * lax.rsqrt(var + LN_EPS) * lng_ref[...] + lnb_ref[...]
    return u, vn


def _gmlp_prompt_kernel(x_ref, g_ref, win_ref, lng_ref, lnb_ref, ws_ref, bias_ref, wout_ref,
                        y_ref, vn_ref, *, tm):
    x = x_ref[...]
    u, vn = _gmlp_front(x, g_ref, win_ref, lng_ref, lnb_ref)
    vn_ref[...] = vn
    ck = GMLP_CHUNK
    gc = vn.shape[1] // GMLP_GROUPS
    row_i = lax.broadcasted_iota(jnp.int32, (ck, ck), 0)
    col_i = lax.broadcasted_iota(jnp.int32, (ck, ck), 1)
    wc = [jnp.where(row_i >= col_i, ws_ref[g], 0.0).astype(BF16) for g in range(GMLP_GROUPS)]
    vb = vn.astype(BF16)
    mix = jnp.concatenate(
        [jnp.concatenate([_dot(wc[g], vb[c * ck:(c + 1) * ck, g * gc:(g + 1) * gc])
                          for g in range(GMLP_GROUPS)], axis=-1) + bias_ref[...]
         for c in range(tm // ck)], axis=0)
    y_ref[...] = x + _dot((u * mix).astype(BF16), wout_ref[...])


def _gmlp_prompt(x, g, w_in, ln_g, ln_b, w_s, bias, w_out, tm):
    n, d = x.shape
    dg = w_out.shape[0]
    kern = functools.partial(_gmlp_prompt_kernel, tm=tm)
    return pl.pallas_call(
        kern,
        grid=(n // tm,),
        in_specs=[pl.BlockSpec((tm, d), lambda i: (i, 0)), _full((1, d)), _full(w_in.shape),
                  _full((1, dg)), _full((1, dg)), _full(w_s.shape), _full(bias.shape),
                  _full(w_out.shape)],
        out_specs=[pl.BlockSpec((tm, d), lambda i: (i, 0)), pl.BlockSpec((tm, dg), lambda i: (i, 0))],
        out_shape=[jax.ShapeDtypeStruct((n, d), F32), jax.ShapeDtypeStruct((n, dg), F32)],
        compiler_params=_params("parallel"),
        name="gmlp_prompt",
    )(x, g, w_in, ln_g, ln_b, w_s, bias, w_out)


def _gmlp_sample_kernel(ws_ref, bs_ref, x_ref, g_ref, win_ref, lng_ref, lnb_ref, wout_ref,
                        y_ref, vn_ref, *, n_t):
    x = x_ref[...]
    u, vn = _gmlp_front(x, g_ref, win_ref, lng_ref, lnb_ref)
    vn_ref[...] = vn
    bd = x.shape[0] // n_t
    gc = vn.shape[1] // GMLP_GROUPS
    rows = []
    for t in range(n_t):
        parts = []
        for g in range(GMLP_GROUPS):
            sl = slice(g * gc, (g + 1) * gc)
            acc = jnp.full((bd, gc), bs_ref[g * n_t + t], F32)
            for s in range(t + 1):
                acc = acc + ws_ref[(g * n_t + t) * n_t + s] * vn[s * bd:(s + 1) * bd, sl]
            parts.append(acc)
        rows.append(jnp.concatenate(parts, axis=-1))
    mix = jnp.concatenate(rows, axis=0)
    y_ref[...] = x + _dot((u * mix).astype(BF16), wout_ref[...])


def _gmlp_sample(x, g, w_in, ln_g, ln_b, ws_flat, bs_flat, w_out, n_t):
    n, d = x.shape
    dg = w_out.shape[0]
    kern = functools.partial(_gmlp_sample_kernel, n_t=n_t)
    smem = pl.BlockSpec(memory_space=pltpu.SMEM)
    return pl.pallas_call(
        kern,
        grid=(1,),
        in_specs=[smem, smem, _full(x.shape), _full((1, d)), _full(w_in.shape), _full((1, dg)),
                  _full((1, dg)), _full(w_out.shape)],
        out_specs=[_full((n, d)), _full((n, dg))],
        out_shape=[jax.ShapeDtypeStruct((n, d), F32), jax.ShapeDtypeStruct((n, dg), F32)],
        compiler_params=_params("arbitrary"),
        name="gmlp_sample",
    )(ws_flat, bs_flat, x, g, w_in, ln_g, ln_b, w_out)


ROUTE_LANES = 128
SLOT_ALIGN = 16
MOE_SUB = 256
ROUTE_ROWS = 32


def _router_kernel(x_ref, g_ref, rw_ref, rb_ref, tri_ref, sel_ref, selt_ref, *refs, n_cast):
    hn_ref, col_ref, row_ref, meta_ref = refs[n_cast:n_cast + 4]
    for src_ref, dst_ref in zip(refs[:n_cast], refs[n_cast + 4:]):
        dst_ref[...] = src_ref[...].astype(BF16)
    ne = MOE_GROUPS * MOE_PER_GROUP
    hb = _rms(x_ref[...], g_ref[...]).astype(BF16)
    hn_ref[...] = hb
    lt = (_dot(hb, rw_ref[...]) + rb_ref[...]).T[:ROUTE_ROWS]
    tm = lt.shape[1]
    ridx = lax.broadcasted_iota(jnp.int32, lt.shape, 0)
    lg = jnp.where((ridx >= ne) & (ridx < ne + MOE_GROUPS), lt, -jnp.inf)
    g_max = jnp.max(lg, axis=0, keepdims=True)
    g_idx = jnp.min(jnp.where(lg == g_max, ridx, ROUTE_ROWS), axis=0, keepdims=True) - ne
    g_p = 1.0 / jnp.sum(jnp.exp(lg - g_max), axis=0, keepdims=True)
    first = g_idx * MOE_PER_GROUP
    l1 = jnp.where((ridx >= first) & (ridx < first + MOE_PER_GROUP), lt, -jnp.inf)
    m1 = jnp.max(l1, axis=0, keepdims=True)
    e1 = jnp.min(jnp.where(l1 == m1, ridx, ROUTE_ROWS), axis=0, keepdims=True)
    l2 = jnp.where(ridx == e1, -jnp.inf, l1)
    m2 = jnp.max(l2, axis=0, keepdims=True)
    e2 = jnp.min(jnp.where(l2 == m2, ridx, ROUTE_ROWS), axis=0, keepdims=True)
    z = jnp.sum(jnp.exp(l1 - m1), axis=0, keepdims=True)
    p1 = 1.0 / z
    p2 = jnp.exp(m2 - m1) / z
    w1 = g_p * p1 / (p1 + p2)
    w2 = g_p * p2 / (p1 + p2)
    hit1 = ridx == e1
    hit2 = ridx == e2
    routed = (jnp.where(hit1, 1.0, 0.0) + jnp.where(hit2, 1.0, 0.0)).astype(BF16)
    rank = _dot(routed, tri_ref[...])
    cnt = _dot(routed, sel_ref[...])
    pad = jnp.floor((cnt + (SLOT_ALIGN - 1)) * (1.0 / SLOT_ALIGN)) * SLOT_ALIGN
    ridx_c = lax.broadcasted_iota(jnp.int32, cnt.shape, 0)
    incl = pad
    shift = 1
    while shift < ne:
        incl = incl + jnp.where(ridx_c >= shift, pltpu.roll(incl, shift, axis=0), 0.0)
        shift *= 2
    off = incl - pad
    slot = rank + _dot(off.astype(BF16), selt_ref[...])
    s1 = jnp.sum(jnp.where(hit1, slot, 0.0), axis=0, keepdims=True)
    s2 = jnp.sum(jnp.where(hit2, slot, 0.0), axis=0, keepdims=True)
    rows = jnp.concatenate([s1, s2, w1, w2, jnp.zeros((row_ref.shape[0] - 4, tm), F32)], axis=0)
    row_ref[...] = rows
    col_ref[...] = jnp.concatenate([rows, jnp.zeros((ROUTE_LANES - rows.shape[0], tm), F32)], axis=0).T
    meta_ref[0] = cnt.astype(jnp.int32)
    meta_ref[1] = off.astype(jnp.int32)


def _router(x, g, rw, rb, tm, sub, layer, to_cast):
    n, d = x.shape
    nt = n // tm
    cast_in, cast_out, cast_shape = [], [], []
    for w in to_cast:
        per_step = w.shape[1] // nt
        assert per_step * nt == w.shape[1]
        cast_in.append(pl.BlockSpec((None, per_step) + w.shape[2:], lambda i: (layer, i, 0, 0)))
        cast_out.append(pl.BlockSpec((per_step,) + w.shape[2:], lambda i: (i, 0, 0)))
        cast_shape.append(jax.ShapeDtypeStruct(w.shape[1:], BF16))
    assert (2 * sub) // SLOT_ALIGN + MOE_GROUPS * MOE_PER_GROUP < 256
    tok = jnp.arange(tm)
    blocks = tok // sub
    tri = ((blocks[:, None] == blocks[None, :]) & (tok[:, None] < tok[None, :])).astype(BF16)
    sel = (blocks[:, None] == jnp.arange(ROUTE_LANES)[None, :]).astype(BF16)
    out = pl.pallas_call(
        functools.partial(_router_kernel, n_cast=len(to_cast)),
        grid=(nt,),
        in_specs=[pl.BlockSpec((tm, d), lambda i: (i, 0)), _full((1, d)), _full(rw.shape),
                  _full(rb.shape), _full(tri.shape), _full(sel.shape), _full(sel.T.shape)] + cast_in,
        out_specs=[pl.BlockSpec((tm, d), lambda i: (i, 0)),
                   pl.BlockSpec((tm, ROUTE_LANES), lambda i: (i, 0)),
                   pl.BlockSpec((8, tm), lambda i: (0, i)),
                   pl.BlockSpec((None, 2, ROUTE_ROWS, ROUTE_LANES), lambda i: (i, 0, 0, 0))] + cast_out,
        out_shape=[jax.ShapeDtypeStruct((n, d), BF16), jax.ShapeDtypeStruct((n, ROUTE_LANES), F32),
                   jax.ShapeDtypeStruct((8, n), F32),
                   jax.ShapeDtypeStruct((nt, 2, ROUTE_ROWS, ROUTE_LANES), jnp.int32)] + cast_shape,
        compiler_params=_params("arbitrary"),
        name="moe_router",
    )(x, g, rw, rb, tri, sel, sel.T, *to_cast)
    return out[:4], out[4:]


def _moe_routed_kernel(cnt_ref, off_ref, trips_ref, hn_ref, col_ref, row_ref, x_ref, gfin_ref, *refs,
                       sub, mcs, n_gather, n_scatter, final_norm, ws):
    nw = ws * ws
    wg_refs, wu_refs, wd_refs = refs[:nw], refs[nw:2 * nw], refs[2 * nw:3 * nw]
    y_ref, xs_ref, ys_ref, gs_ref = refs[3 * nw:]
    i = pl.program_id(0)
    eg = pl.program_id(1)
    eps = wg_refs[0].shape[0]
    ne = pl.num_programs(1) * eps
    n_sub = xs_ref.shape[0]
    dump = n_gather
    rows_of = lambda s: slice(s * sub, (s + 1) * sub)
    dk = hn_ref.shape[1] // ws

    def up_proj(xg, w_refs, r):
        cols = []
        for c in range(ws):
            acc = _dot(xg[:, :dk], w_refs[c][r])
            for k in range(1, ws):
                acc = acc + _dot(xg[:, k * dk:(k + 1) * dk], w_refs[k * ws + c][r])
            cols.append(acc)
        return jnp.concatenate(cols, axis=-1)

    @pl.when(eg == 0)
    def _():
        y_ref[...] = x_ref[...]
        ys_ref[...] = jnp.zeros_like(ys_ref)
        sid = lax.broadcasted_iota(jnp.int32, (n_gather, sub), 0).astype(F32)
        for s in range(n_sub):
            s1, s2, w1, w2 = (row_ref[r:r + 1, rows_of(s)] for r in range(4))
            h1 = s1 == sid
            h2 = s2 == sid
            onehot = (jnp.where(h1, 1.0, 0.0) + jnp.where(h2, 1.0, 0.0)).astype(BF16)
            xs_ref[s, 0:n_gather, :] = _dot(onehot, hn_ref[rows_of(s), :]).astype(BF16)
            xs_ref[s, dump:dump + mcs, :] = jnp.zeros((mcs, xs_ref.shape[2]), BF16)
            gs_ref[s, 0:n_gather, :] = jnp.sum(
                jnp.where(h1, w1, 0.0) + jnp.where(h2, w2, 0.0), axis=-1, keepdims=True)
            gs_ref[s, dump:dump + mcs, :] = jnp.zeros((mcs, 1), F32)

    for r in range(eps):
        e = eg * eps + r
        counts = [cnt_ref[(i * n_sub + s) * ne + e] for s in range(n_sub)]
        firsts = [off_ref[(i * n_sub + s) * ne + e] for s in range(n_sub)]

        def chunk(k, carry, r=r, counts=counts, firsts=firsts):
            rows = [pl.multiple_of(jnp.where(k * mcs < counts[s], firsts[s] + k * mcs, dump), SLOT_ALIGN)
                    for s in range(n_sub)]
            xg = jnp.concatenate([xs_ref[s, pl.ds(rows[s], mcs), :] for s in range(n_sub)], axis=0)
            gate = jnp.concatenate([gs_ref[s, pl.ds(rows[s], mcs), :] for s in range(n_sub)], axis=0)
            a = up_proj(xg, wg_refs, r)
            b = up_proj(xg, wu_refs, r)
            hid = ((a * jax.nn.sigmoid(a)) * b * gate).astype(BF16)
            out = jnp.concatenate([_dot(hid, w_ref[r]) for w_ref in wd_refs], axis=-1).astype(BF16)
            for s in range(n_sub):
                ys_ref[s, pl.ds(rows[s], mcs), :] = out[s * mcs:(s + 1) * mcs]
            return carry

        lax.fori_loop(0, trips_ref[i * ne + e], chunk, 0)

    @pl.when(eg == pl.num_programs(1) - 1)
    def _():
        sid = lax.broadcasted_iota(jnp.int32, (sub, n_scatter), 1).astype(F32)
        for s in range(n_sub):
            onehot = (jnp.where(col_ref[rows_of(s), 0:1] == sid, 1.0, 0.0)
                      + jnp.where(col_ref[rows_of(s), 1:2] == sid, 1.0, 0.0)).astype(BF16)
            out = y_ref[rows_of(s), :] + _dot(onehot, ys_ref[s, 0:n_scatter, :])
            y_ref[rows_of(s), :] = _rms(out, gfin_ref[...]) if final_norm else out


def _round_up(a, b):
    return -(-a // b) * b


def _moe_routed(cnt, off, hn, col, row, wg, wu, wd, x, g_final, tm, sub):
    n, d = x.shape
    ne, _, de = wg.shape
    n_sub = tm // sub
    mcs = _round_up(2 * sub * 3 // (ne * 2), SLOT_ALIGN)
    most = 2 * sub + (ne - 1) * SLOT_ALIGN
    n_scatter = _round_up(most, 128)
    n_gather = _round_up(most + mcs, 128)
    trips = jnp.max((cnt + mcs - 1) // mcs, axis=1).reshape(-1)
    eps = 2 if ne % 2 == 0 else 1
    final_norm = g_final is not None
    if g_final is None:
        g_final = jnp.ones((1, d), F32)
    ws = 2
    kern = functools.partial(_moe_routed_kernel, sub=sub, mcs=mcs, n_gather=n_gather, n_scatter=n_scatter,
                             final_norm=final_norm, ws=ws)
    tile = lambda i, e, *_: (i, 0)

    def piece(shape, k, c):
        return pl.BlockSpec((eps,) + shape, lambda i, e, *_: (e, k, c))

    up_specs = [piece((d // ws, de // ws), k, c) for k in range(ws) for c in range(ws)]
    down_specs = [piece((de, d // (ws * ws)), 0, c) for c in range(ws * ws)]
    return pl.pallas_call(
        kern,
        grid_spec=pltpu.PrefetchScalarGridSpec(
            num_scalar_prefetch=3,
            grid=(n // tm, ne // eps),
            in_specs=[pl.BlockSpec((tm, d), tile), pl.BlockSpec((tm, ROUTE_LANES), tile),
                      pl.BlockSpec((8, tm), lambda i, e, *_: (0, i)), pl.BlockSpec((tm, d), tile),
                      pl.BlockSpec((1, d), lambda i, e, *_: (0, 0))] + up_specs + up_specs + down_specs,
            out_specs=pl.BlockSpec((tm, d), tile),
            scratch_shapes=[pltpu.VMEM((n_sub, n_gather + mcs, d), BF16),
                            pltpu.VMEM((n_sub, n_gather + mcs, d), BF16),
                            pltpu.VMEM((n_sub, n_gather + mcs, 1), F32)],
        ),
        out_shape=jax.ShapeDtypeStruct((n, d), F32),
        compiler_params=_params("arbitrary", "arbitrary"),
        name="moe_routed",
    )(cnt.reshape(-1), off.reshape(-1), trips, hn, col, row, x, g_final, *([wg] * (ws * ws)),
      *([wu] * (ws * ws)), *([wd] * (ws * ws)))


def _moe(x, g, rw, rb, weights, layer, g_final, tm):
    sub = min(tm, MOE_SUB)
    n_sub = tm // sub
    assert tm % sub == 0 and n_sub <= ROUTE_LANES
    to_cast = [w for w in weights if w.dtype != BF16]
    if to_cast and any(w.shape[1] % (x.shape[0] // tm) for w in to_cast):
        weights, to_cast = [w[layer].astype(BF16) for w in weights], []
    (hn, col, row, meta), cast = _router(x, g, rw, rb, tm, sub, layer, to_cast)
    wg, wu, wd = cast if to_cast else weights
    ne = wg.shape[0]
    cnt, off = (meta[:, r, :ne, :n_sub].transpose(0, 2, 1) for r in range(2))
    return _moe_routed(cnt, off, hn, col, row, wg, wu, wd, x, g_final, tm, sub), (wg, wu, wd)


def _tile(n, pref):
    t = min(n, pref)
    assert n % t == 0, (n, t)
    return t


def kernel(x_prompt, x_sample, state_pool, cache_k, cache_v, page_table, norm_mix, norm_ffn, norm_final, pool_w_in, pool_w_grp, pool_scale, pool_w_out, attn_w_qkv, attn_w_out, gmlp_w_in, gmlp_ln_g, gmlp_ln_b, gmlp_w_s, gmlp_b_s, gmlp_w_out, moe_rg_w, moe_rg_b, moe_re_w, moe_re_b, moe_w_gate, moe_w_up, moe_w_down):
    batch, seq, d = x_prompt.shape
    dec_b, n_t, _ = x_sample.shape
    depth = norm_mix.shape[0]
    n_heads = N_HEADS
    hd = d // n_heads
    n_pages = page_table.shape[1]
    past_len = n_pages * PAGE_SIZE
    assert seq % MOBA_BLOCK == 0 and seq % GMLP_CHUNK == 0 and n_t <= GMLP_CHUNK
    assert past_len % GMLP_CHUNK == 0 and past_len >= POOL_STATE

    n_p, n_s = batch * seq, dec_b * n_t
    xp = x_prompt.reshape(n_p, d)
    xs = x_sample.transpose(1, 0, 2).reshape(n_s, d)
    tp = _tile(seq, 512)
    row = lambda a: a.reshape(1, -1)

    def to_batch_major(a):
        return a.reshape((n_t, dec_b) + a.shape[1:]).swapaxes(0, 1)

    pool_p, pool_s, k_p, v_p, k_s, v_s, gv_p, gv_s = [], [], [], [], [], [], [], []
    for i in range(depth):
        kind, j = i % N_MIXERS, i // N_MIXERS
        g_mix = row(norm_mix[i])
        if kind == 0:
            w_in, w_grp, w_out = (a.astype(BF16) for a in (pool_w_in[j], pool_w_grp[j], pool_w_out[j]))
            scale = row(pool_scale[j])
            xp, st = _pool_prompt(xp, g_mix, w_in, w_grp, scale, w_out, batch, seq, tp)
            pool_p.append(st)
            xs, st = _pool_sample(xs, state_pool, j, g_mix, w_in, w_grp, scale, w_out, n_t, past_len)
            pool_s.append(st)
        elif kind == 1:
            w_qkv, w_out = attn_w_qkv[j].astype(BF16), attn_w_out[j].astype(BF16)
            qp, kp, vp = _norm_matmul(xp, g_mix, w_qkv, (BF16, F32, F32), _tile(n_p, 1024))
            qs, ks, vs = _norm_matmul(xs, g_mix, w_qkv, (F32, F32, F32), _tile(n_s, 512))
            ap = _moba_prompt(qp, kp, vp, batch, seq, n_heads)
            a_s = _moba_sample(qs, ks, vs, cache_k, cache_v, j, page_table, n_t, n_heads)
            xp = _matmul_residual(ap, w_out, xp, _tile(n_p, 1024))
            xs = _matmul_residual(a_s, w_out, xs, _tile(n_s, 512))
            k_p.append(kp.reshape(batch, seq, n_heads, hd))
            v_p.append(vp.reshape(batch, seq, n_heads, hd))
            k_s.append(to_batch_major(ks).reshape(dec_b, n_t, n_heads, hd))
            v_s.append(to_batch_major(vs).reshape(dec_b, n_t, n_heads, hd))
        else:
            w_in, w_out = gmlp_w_in[j].astype(BF16), gmlp_w_out[j].astype(BF16)
            ln_g, ln_b = row(gmlp_ln_g[j]), row(gmlp_ln_b[j])
            gc = w_out.shape[0] // GMLP_GROUPS
            bias = jnp.broadcast_to(gmlp_b_s[j].T[:, :, None], (GMLP_CHUNK, GMLP_GROUPS, gc))
            xp, vn_p = _gmlp_prompt(xp, g_mix, w_in, ln_g, ln_b, gmlp_w_s[j],
                                    bias.reshape(GMLP_CHUNK, GMLP_GROUPS * gc), w_out, tp)
            xs, vn_s = _gmlp_sample(xs, g_mix, w_in, ln_g, ln_b,
                                    gmlp_w_s[j][:, :n_t, :n_t].reshape(-1),
                                    gmlp_b_s[j][:, :n_t].reshape(-1), w_out, n_t)
            gv_p.append(vn_p.reshape(batch, seq, -1))
            gv_s.append(to_batch_major(vn_s))

        g_ffn = row(norm_ffn[i])
        ne = moe_re_w.shape[2]
        fill = jnp.zeros((d, ROUTE_LANES - ne - MOE_GROUPS), F32)
        rw = jnp.concatenate([moe_re_w[i], moe_rg_w[i], fill], axis=1).astype(BF16)
        rb = jnp.concatenate([moe_re_b[i], moe_rg_b[i], fill[0]]).reshape(1, ROUTE_LANES)
        g_fin = row(norm_final) if i == depth - 1 else None
        xp, experts = _moe(xp, g_ffn, rw, rb, (moe_w_gate, moe_w_up, moe_w_down), i, g_fin, _tile(n_p, 1024))
        xs, _ = _moe(xs, g_ffn, rw, rb, experts, i, g_fin, _tile(n_s, 512))

    y_p = xp.reshape(batch, seq, d)
    y_s = to_batch_major(xs)
    return (y_p, y_s, jnp.stack(pool_p), jnp.stack(pool_s), jnp.stack(k_p), jnp.stack(v_p),
            jnp.stack(k_s), jnp.stack(v_s), jnp.stack(gv_p), jnp.stack(gv_s))
```

```python
import functools

import jax
import jax.numpy as jnp
from jax import lax
from jax.experimental import pallas as pl
from jax.experimental.pallas import tpu as pltpu

F32 = jnp.float32
BF16 = jnp.bfloat16

RMS_EPS = 1e-6
LN_EPS = 1e-5
N_MIXERS = 3
POOL_WINDOWS = (2, 4, 8, 16)
POOL_STATE = max(POOL_WINDOWS) - 1
MOBA_BLOCK = 256
MOBA_TOPK = 3
GMLP_CHUNK = 128
GMLP_GROUPS = 8
MOE_GROUPS = 4
MOE_PER_GROUP = 4
PAGE_SIZE = 128
N_HEADS = 8

V7X_VMEM_LIMIT_BYTES = 56 * 1024 * 1024
NEG_BIG = -1e30
LOG2_E = 1.4426950408889634


def _params(*sem):
    return pltpu.CompilerParams(dimension_semantics=sem, vmem_limit_bytes=V7X_VMEM_LIMIT_BYTES)


def _rms(x, g):
    return x * lax.rsqrt(jnp.mean(x * x, axis=-1, keepdims=True) + RMS_EPS) * g


def _dot(a, b):
    return jnp.dot(a, b, preferred_element_type=F32)


def _dot_nt(a, b, precision=None):
    return lax.dot_general(a, b, (((1,), (1,)), ((), ())), preferred_element_type=F32,
                           precision=precision)


def _full(shape):
    return pl.BlockSpec(shape, lambda *_: (0,) * len(shape))


def _norm_matmul_kernel(x_ref, g_ref, w_ref, *out_refs):
    h = _rms(x_ref[...], g_ref[...]).astype(BF16)
    f = w_ref.shape[1] // len(out_refs)
    for i, o_ref in enumerate(out_refs):
        o_ref[...] = _dot(h, w_ref[:, i * f:(i + 1) * f]).astype(o_ref.dtype)


def _norm_matmul(x, g, w, out_dtypes, tm):
    n, d = x.shape
    n_out = len(out_dtypes)
    f = w.shape[1] // n_out
    return pl.pallas_call(
        _norm_matmul_kernel,
        grid=(n // tm,),
        in_specs=[pl.BlockSpec((tm, d), lambda i: (i, 0)), _full((1, d)), _full(w.shape)],
        out_specs=[pl.BlockSpec((tm, f), lambda i: (i, 0))] * n_out,
        out_shape=[jax.ShapeDtypeStruct((n, f), dt) for dt in out_dtypes],
        compiler_params=_params("parallel"),
        name="norm_matmul",
    )(x, g, w)


def _matmul_residual_kernel(a_ref, w_ref, x_ref, y_ref):
    y_ref[...] = x_ref[...] + _dot(a_ref[...].astype(BF16), w_ref[...])


def _matmul_residual(a, w, x, tm):
    n, d = x.shape
    k = a.shape[1]
    return pl.pallas_call(
        _matmul_residual_kernel,
        grid=(n // tm,),
        in_specs=[pl.BlockSpec((tm, k), lambda i: (i, 0)), _full(w.shape),
                  pl.BlockSpec((tm, d), lambda i: (i, 0))],
        out_specs=pl.BlockSpec((tm, d), lambda i: (i, 0)),
        out_shape=jax.ShapeDtypeStruct((n, d), F32),
        compiler_params=_params("parallel"),
        name="matmul_residual",
    )(a, w, x)


def _pool_tail(r, x, wgrp_ref, scale_ref, wout_ref):
    gc = wgrp_ref.shape[1]
    rb = r.astype(BF16)
    t = jnp.concatenate([_dot(rb[:, g * gc:(g + 1) * gc], wgrp_ref[g])
                         for g in range(len(POOL_WINDOWS))], axis=-1)
    t = (t * scale_ref[...]).astype(BF16)
    return x + _dot(t, wout_ref[...])


def _pool_prompt_kernel(x_ref, g_ref, win_ref, wgrp_ref, scale_ref, wout_ref,
                        y_ref, st_ref, carry_ref, *, ts, ns):
    s = pl.program_id(1)
    halo = carry_ref.shape[0]
    gc = wgrp_ref.shape[1]

    @pl.when(s == 0)
    def _():
        carry_ref[...] = jnp.zeros_like(carry_ref)

    x = x_ref[...]
    h = _rms(x, g_ref[...]).astype(BF16)
    u = _dot(h, win_ref[...])
    ext = jnp.concatenate([carry_ref[...], u], axis=0)
    pos = s * ts + lax.broadcasted_iota(jnp.int32, (ts, 1), 0)
    parts = []
    acc = ext
    width = 1
    for w in POOL_WINDOWS:
        while width < w:
            acc = acc + pltpu.roll(acc, width, axis=0)
            width *= 2
        assert width == w
        cnt = jnp.minimum(pos + 1, w).astype(F32)
        parts.append(acc[halo:, :gc] / cnt)
        acc = acc[:, gc:]
    r = jnp.concatenate(parts, axis=-1) - u
    y_ref[...] = _pool_tail(r, x, wgrp_ref, scale_ref, wout_ref)
    carry_ref[...] = u[ts - halo:, :]

    @pl.when(s == ns - 1)
    def _():
        st_ref[...] = carry_ref[halo - POOL_STATE:, :]


def _pool_prompt(x, g, w_in, w_grp, scale, w_out, batch, seq, ts):
    n, d = x.shape
    ns = seq // ts
    halo = 16
    kern = functools.partial(_pool_prompt_kernel, ts=ts, ns=ns)
    return pl.pallas_call(
        kern,
        grid=(batch, ns),
        in_specs=[pl.BlockSpec((ts, d), lambda b, s: (b * ns + s, 0)), _full((1, d)),
                  _full(w_in.shape), _full(w_grp.shape), _full((1, d)), _full(w_out.shape)],
        out_specs=[pl.BlockSpec((ts, d), lambda b, s: (b * ns + s, 0)),
                   pl.BlockSpec((None, POOL_STATE, d), lambda b, s: (b, 0, 0))],
        out_shape=[jax.ShapeDtypeStruct((n, d), F32),
                   jax.ShapeDtypeStruct((batch, POOL_STATE, d), F32)],
        scratch_shapes=[pltpu.VMEM((halo, d), F32)],
        compiler_params=_params("parallel", "arbitrary"),
        name="pool_prompt",
    )(x, g, w_in, w_grp, scale, w_out)


def _pool_sample_kernel(x_ref, pref_ref, g_ref, win_ref, wgrp_ref, scale_ref, wout_ref,
                        y_ref, st_ref, *, n_t, pos0):
    bd = x_ref.shape[0] // n_t
    gc = wgrp_ref.shape[1]
    x = x_ref[...]
    h = _rms(x, g_ref[...]).astype(BF16)
    u = _dot(h, win_ref[...])
    us = [u[t * bd:(t + 1) * bd] for t in range(n_t)]
    for r in range(POOL_STATE):
        st_ref[:, r, :] = pref_ref[:, r + n_t, :] if r + n_t < POOL_STATE else us[r + n_t - POOL_STATE]
    tail = [jnp.zeros_like(us[0])]
    for m in range(1, POOL_STATE + 1):
        tail.append(tail[-1] + pref_ref[:, POOL_STATE - m, :])
    rows = []
    for t in range(n_t):
        parts = []
        for gi, w in enumerate(POOL_WINDOWS):
            sl = slice(gi * gc, (gi + 1) * gc)
            n_new = min(w, t + 1)
            wsum = us[t][:, sl]
            for k in range(1, n_new):
                wsum = wsum + us[t - k][:, sl]
            if w - n_new:
                wsum = wsum + tail[w - n_new][:, sl]
            parts.append(wsum / float(min(pos0 + t + 1, w)))
        rows.append(jnp.concatenate(parts, axis=-1) - us[t])
    r = jnp.concatenate(rows, axis=0)
    y_ref[...] = _pool_tail(r, x, wgrp_ref, scale_ref, wout_ref)


def _pool_sample(x, state, layer, g, w_in, w_grp, scale, w_out, n_t, pos0):
    n, d = x.shape
    assert n_t <= POOL_STATE
    st_shape = state.shape[1:]
    kern = functools.partial(_pool_sample_kernel, n_t=n_t, pos0=pos0)
    return pl.pallas_call(
        kern,
        grid=(1,),
        in_specs=[_full(x.shape), pl.BlockSpec((None,) + st_shape, lambda i: (layer, 0, 0, 0)),
                  _full((1, d)), _full(w_in.shape), _full(w_grp.shape), _full((1, d)), _full(w_out.shape)],
        out_specs=[_full((n, d)), _full(st_shape)],
        out_shape=[jax.ShapeDtypeStruct((n, d), F32), jax.ShapeDtypeStruct(st_shape, F32)],
        compiler_params=_params("arbitrary"),
        name="pool_sample",
    )(x, state, g, w_in, w_grp, scale, w_out)


def _top_blocks(gate, n_valid, n_blocks):
    idx = lax.broadcasted_iota(jnp.int32, gate.shape, 0)
    valid = idx < n_valid
    rows = []
    for j in range(n_blocks):
        gj = gate[j:j + 1, :]
        beats = valid & ((gate > gj) | ((gate == gj) & (idx < j)))
        rank = jnp.sum(beats.astype(F32), axis=0, keepdims=True)
        rows.append(jnp.where(rank < MOBA_TOPK, jnp.where(j < n_valid, 1.0, 0.0), 0.0))
    return rows


def _moba_prompt_kernel(q_ref, k_ref, v_ref, o_ref, kb_ref, vt_ref, kmean_ref, sel_ref, s_ref,
                        *, nb, group, hps, hd):
    qi = pl.program_id(2)
    blk = MOBA_BLOCK
    scale = hd ** -0.5
    heads = range(hps)
    lanes = lambda h: slice(h * hd, (h + 1) * hd)

    @pl.when(qi == 0)
    def _():
        k = k_ref[...]
        kb_ref[...] = k.astype(BF16)
        kmean_ref[...] = jnp.mean(k.reshape(nb, blk, hps * hd), axis=1)
        vt_ref[...] = v_ref[...].T.astype(BF16)

    q = q_ref[...]
    qb = [q[:, lanes(h)].astype(BF16) for h in heads]
    for h in heads:
        gate = _dot_nt(kmean_ref[:, lanes(h)].astype(BF16), qb[h])
        for j, row in enumerate(_top_blocks(gate, qi, nb)):
            sel_ref[h * nb + j:h * nb + j + 1, :] = row
    key_i = lax.broadcasted_iota(jnp.int32, (blk, blk), 0)
    qry_i = lax.broadcasted_iota(jnp.int32, (blk, blk), 1)
    scale2 = scale * LOG2_E

    def scores(h, j, n):
        kj = kb_ref[pl.ds(pl.multiple_of(j * blk, blk), n * blk), lanes(h)]
        return _dot_nt(kj, qb[h]) * scale2

    def pv(h, j, n, p):
        vj = vt_ref[lanes(h), pl.ds(pl.multiple_of(j * blk, blk), n * blk)]
        return _dot(vj, p.astype(BF16))

    n_trips = qi // group + 1
    causal = jnp.where(key_i <= qry_i, 1.0, 0.0)

    def score_pass(g, ms):
        j = g * group
        out = []
        for h, m in zip(heads, ms):
            keep = jnp.concatenate(
                [jnp.where(j + r == qi, causal,
                           jnp.broadcast_to(sel_ref[pl.ds(h * nb + j + r, 1), :], (blk, blk)))
                 for r in range(group)], axis=0)
            s = jnp.where(keep > 0.0, scores(h, j, group), NEG_BIG)
            s_ref[h, pl.ds(pl.multiple_of(j * blk, blk), group * blk), :] = s
            out.append(jnp.maximum(m, jnp.max(s, axis=0, keepdims=True)))
        return tuple(out)

    ms = lax.fori_loop(0, n_trips, score_pass, tuple(jnp.full((1, blk), NEG_BIG, F32) for _ in heads))
    carry = [(jnp.zeros((1, blk), F32), jnp.zeros((hd, blk), F32)) for _ in heads]

    def value_pass(g, carry):
        j = g * group
        out = []
        for h, (l, acc) in zip(heads, carry):
            p = jnp.exp2(s_ref[h, pl.ds(pl.multiple_of(j * blk, blk), group * blk), :] - ms[h])
            out.append((l + jnp.sum(p, axis=0, keepdims=True), acc + pv(h, j, group, p)))
        return tuple(out)

    carry = lax.fori_loop(0, n_trips, value_pass, tuple(carry))
    for h, (l, acc) in zip(heads, carry):
        o_ref[:, lanes(h)] = (acc / l).T.astype(o_ref.dtype)


def _moba_prompt(q, k, v, batch, seq, n_heads):
    n, d = q.shape
    hd = d // n_heads
    nb = seq // MOBA_BLOCK
    group = 4 if nb % 4 == 0 else 1
    hps = 2 if n_heads % 2 == 0 else 1
    kern = functools.partial(_moba_prompt_kernel, nb=nb, group=group, hps=hps, hd=hd)
    return pl.pallas_call(
        kern,
        grid=(batch, n_heads // hps, nb),
        in_specs=[pl.BlockSpec((MOBA_BLOCK, hps * hd), lambda b, h, i: (b * nb + i, h)),
                  pl.BlockSpec((seq, hps * hd), lambda b, h, i: (b, h)),
                  pl.BlockSpec((seq, hps * hd), lambda b, h, i: (b, h))],
        out_specs=pl.BlockSpec((MOBA_BLOCK, hps * hd), lambda b, h, i: (b * nb + i, h)),
        out_shape=jax.ShapeDtypeStruct((n, d), BF16),
        scratch_shapes=[pltpu.VMEM((seq, hps * hd), BF16), pltpu.VMEM((hps * hd, seq), BF16),
                        pltpu.VMEM((nb, hps * hd), F32), pltpu.VMEM((hps * nb, MOBA_BLOCK), F32),
                        pltpu.VMEM((hps, nb * MOBA_BLOCK, MOBA_BLOCK), F32)],
        compiler_params=_params("parallel", "parallel", "arbitrary"),
        name="moba_prompt",
    )(q, k, v)


def _moba_sample_kernel(pt_ref, q_ref, kn_ref, vn_ref, *refs, n_t, n_pages, pps):
    del pt_ref
    kp_refs, vp_refs = refs[:pps], refs[pps:2 * pps]
    o_ref, wbd_ref, s_ref, vall_ref = refs[2 * pps:]
    p_i = pl.program_id(1)
    nh, hd = q_ref.shape[2], q_ref.shape[3]
    lanes = wbd_ref.shape[1]
    scale = hd ** -0.5
    n_past = n_pages // (MOBA_BLOCK // PAGE_SIZE)
    head_i = lax.broadcasted_iota(jnp.int32, (nh, hd), 0)

    @pl.when(p_i == 0)
    def _():
        rows = [jnp.concatenate([jnp.where(head_i == h, q_ref[t, 0], 0.0) for h in range(nh)], axis=-1)
                for t in range(n_t)]
        rows.append(jnp.zeros((lanes - n_t * nh, nh * hd), F32))
        wbd_ref[...] = jnp.concatenate(rows, axis=0).T.astype(BF16)

    for g, (kp_ref, vp_ref) in enumerate(zip(kp_refs, vp_refs)):
        off = pl.multiple_of((p_i * pps + g) * PAGE_SIZE, PAGE_SIZE)
        acc = None
        for h in range(nh):
            kh = kp_ref[pl.ds(h, PAGE_SIZE, stride=nh), :].astype(BF16)
            part = _dot(kh, wbd_ref[h * hd:(h + 1) * hd, :])
            acc = part if acc is None else acc + part
            vall_ref[pl.ds(off, PAGE_SIZE), h * hd:(h + 1) * hd] = (
                vp_ref[pl.ds(h, PAGE_SIZE, stride=nh), :].astype(BF16))
        s_ref[pl.ds(off, PAGE_SIZE), :] = acc

    @pl.when(p_i == n_pages // pps - 1)
    def _():
        s_all = s_ref[...]
        blocks = [s_all[j * MOBA_BLOCK:(j + 1) * MOBA_BLOCK] for j in range(n_past)]
        gate = [jnp.sum(b, axis=0, keepdims=True) * (1.0 / MOBA_BLOCK) for b in blocks]
        sel = []
        for j in range(n_past):
            rank = jnp.zeros_like(gate[j])
            for i in range(n_past):
                if i != j:
                    beats = (gate[i] >= gate[j]) if i < j else (gate[i] > gate[j])
                    rank = rank + beats.astype(F32)
            sel.append(rank < MOBA_TOPK)
        s_past = jnp.concatenate(
            [jnp.where(sel[j], blocks[j] * scale, NEG_BIG) for j in range(n_past)], axis=0)
        m_row = jnp.max(s_past, axis=0, keepdims=True)
        p_past = jnp.exp(s_past - m_row)
        l_row = jnp.sum(p_past, axis=0, keepdims=True)
        full = _dot(p_past.T.astype(BF16), vall_ref[...])
        stats = jnp.concatenate([m_row, l_row, jnp.zeros((lanes - 2, lanes), F32)], axis=0).T
        for t in range(n_t):
            rows = full[t * nh:(t + 1) * nh]
            out = functools.reduce(
                lambda a, b: a + b,
                [jnp.where(head_i == h, rows[:, h * hd:(h + 1) * hd], 0.0) for h in range(nh)])
            m_past = stats[t * nh:(t + 1) * nh, 0:1]
            l_past = stats[t * nh:(t + 1) * nh, 1:2]
            s_new = [jnp.sum(q_ref[t, 0] * kn_ref[t2, 0], axis=-1, keepdims=True) * scale
                     for t2 in range(t + 1)]
            m = functools.reduce(jnp.maximum, s_new, m_past)
            alpha = jnp.exp(m_past - m)
            out = alpha * out
            l = alpha * l_past
            for t2 in range(t + 1):
                p_new = jnp.exp(s_new[t2] - m)
                l = l + p_new
                out = out + p_new * vn_ref[t2, 0]
            o_ref[t, 0] = out / l


def _moba_sample(q, k, v, cache_k, cache_v, layer, page_table, n_t, n_heads):
    n, d = q.shape
    bd = n // n_t
    hd = d // n_heads
    n_pages = page_table.shape[1]
    past = n_pages * PAGE_SIZE
    lanes = 128
    assert n_pages % (MOBA_BLOCK // PAGE_SIZE) == 0 and n_t <= MOBA_BLOCK and n_t * n_heads <= lanes - 2
    q4, k4, v4 = (a.reshape(n_t, bd, n_heads, hd) for a in (q, k, v))
    n_layers, n_phys = cache_k.shape[:2]
    ck, cv = (c.reshape(n_layers, n_phys, PAGE_SIZE * n_heads, hd) for c in (cache_k, cache_v))
    new_spec = pl.BlockSpec((n_t, 1, n_heads, hd), lambda b, p, pt: (0, b, 0, 0))
    pps = next(p for p in (16, 8, 2) if n_pages % p == 0)
    page_specs = [pl.BlockSpec((None, None, PAGE_SIZE * n_heads, hd),
                               functools.partial(lambda g, b, p, pt: (layer, pt[b, p * pps + g], 0, 0), g))
                  for g in range(pps)]
    kern = functools.partial(_moba_sample_kernel, n_t=n_t, n_pages=n_pages, pps=pps)
    out = pl.pallas_call(
        kern,
        grid_spec=pltpu.PrefetchScalarGridSpec(
            num_scalar_prefetch=1,
            grid=(bd, n_pages // pps),
            in_specs=[new_spec, new_spec, new_spec] + page_specs + page_specs,
            out_specs=new_spec,
            scratch_shapes=[pltpu.VMEM((d, lanes), BF16), pltpu.VMEM((past, lanes), F32),
                            pltpu.VMEM((past, d), BF16)],
        ),
        out_shape=jax.ShapeDtypeStruct((n_t, bd, n_heads, hd), F32),
        compiler_params=_params("parallel", "arbitrary"),
        name="moba_sample",
    )(page_table, q4, k4, v4, *([ck] * pps), *([cv] * pps))
    return out.reshape(n, d)


def _gelu(z):
    return 0.5 * z * (1.0 + jnp.tanh(0.7978845608028654 * (z + 0.044715 * z * z * z)))


def _gmlp_front(x, g_ref, win_ref, lng_ref, lnb_ref):
    h = _rms(x, g_ref[...]).astype(BF16)
    z = _gelu(_dot(h, win_ref[...]))
    dg = z.shape[1] // 2
    u, v = z[:, :dg], z[:, dg:]
    mu = jnp.mean(v, axis=-1, keepdims=True)
    var = jnp.mean(jnp.square(v - mu), axis=-1, keepdims=True)
    vn = (v - mu) * lax.rsqrt(var + LN_EPS) * lng_ref[...] + lnb_ref[...]
    return u, vn


def _gmlp_prompt_kernel(x_ref, g_ref, win_ref, lng_ref, lnb_ref, ws_ref, bias_ref, wout_ref,
                        y_ref, vn_ref, *, tm):
    x = x_ref[...]
    u, vn = _gmlp_front(x, g_ref, win_ref, lng_ref, lnb_ref)
    vn_ref[...] = vn
    ck = GMLP_CHUNK
    gc = vn.shape[1] // GMLP_GROUPS
    row_i = lax.broadcasted_iota(jnp.int32, (ck, ck), 0)
    col_i = lax.broadcasted_iota(jnp.int32, (ck, ck), 1)
    wc = [jnp.where(row_i >= col_i, ws_ref[g], 0.0).astype(BF16) for g in range(GMLP_GROUPS)]
    vb = vn.astype(BF16)
    mix = jnp.concatenate(
        [jnp.concatenate([_dot(wc[g], vb[c * ck:(c + 1) * ck, g * gc:(g + 1) * gc])
                          for g in range(GMLP_GROUPS)], axis=-1) + bias_ref[...]
         for c in range(tm // ck)], axis=0)
    y_ref[...] = x + _dot((u * mix).astype(BF16), wout_ref[...])


def _gmlp_prompt(x, g, w_in, ln_g, ln_b, w_s, bias, w_out, tm):
    n, d = x.shape
    dg = w_out.shape[0]
    kern = functools.partial(_gmlp_prompt_kernel, tm=tm)
    return pl.pallas_call(
        kern,
        grid=(n // tm,),
        in_specs=[pl.BlockSpec((tm, d), lambda i: (i, 0)), _full((1, d)), _full(w_in.shape),
                  _full((1, dg)), _full((1, dg)), _full(w_s.shape), _full(bias.shape),
                  _full(w_out.shape)],
        out_specs=[pl.BlockSpec((tm, d), lambda i: (i, 0)), pl.BlockSpec((tm, dg), lambda i: (i, 0))],
        out_shape=[jax.ShapeDtypeStruct((n, d), F32), jax.ShapeDtypeStruct((n, dg), F32)],
        compiler_params=_params("parallel"),
        name="gmlp_prompt",
    )(x, g, w_in, ln_g, ln_b, w_s, bias, w_out)


def _gmlp_sample_kernel(ws_ref, bs_ref, x_ref, g_ref, win_ref, lng_ref, lnb_ref, wout_ref,
                        y_ref, vn_ref, *, n_t):
    x = x_ref[...]
    u, vn = _gmlp_front(x, g_ref, win_ref, lng_ref, lnb_ref)
    vn_ref[...] = vn
    bd = x.shape[0] // n_t
    gc = vn.shape[1] // GMLP_GROUPS
    rows = []
    for t in range(n_t):
        parts = []
        for g in range(GMLP_GROUPS):
            sl = slice(g * gc, (g + 1) * gc)
            acc = jnp.full((bd, gc), bs_ref[g * n_t + t], F32)
            for s in range(t + 1):
                acc = acc + ws_ref[(g * n_t + t) * n_t + s] * vn[s * bd:(s + 1) * bd, sl]
            parts.append(acc)
        rows.append(jnp.concatenate(parts, axis=-1))
    mix = jnp.concatenate(rows, axis=0)
    y_ref[...] = x + _dot((u * mix).astype(BF16), wout_ref[...])


def _gmlp_sample(x, g, w_in, ln_g, ln_b, ws_flat, bs_flat, w_out, n_t):
    n, d = x.shape
    dg = w_out.shape[0]
    kern = functools.partial(_gmlp_sample_kernel, n_t=n_t)
    smem = pl.BlockSpec(memory_space=pltpu.SMEM)
    return pl.pallas_call(
        kern,
        grid=(1,),
        in_specs=[smem, smem, _full(x.shape), _full((1, d)), _full(w_in.shape), _full((1, dg)),
                  _full((1, dg)), _full(w_out.shape)],
        out_specs=[_full((n, d)), _full((n, dg))],
        out_shape=[jax.ShapeDtypeStruct((n, d), F32), jax.ShapeDtypeStruct((n, dg), F32)],
        compiler_params=_params("arbitrary"),
        name="gmlp_sample",
    )(ws_flat, bs_flat, x, g, w_in, ln_g, ln_b, w_out)


ROUTE_LANES = 128
SLOT_ALIGN = 16
MOE_SUB = 256
ROUTE_ROWS = 32


def _router_kernel(x_ref, g_ref, rw_ref, rb_ref, tri_ref, sel_ref, selt_ref, *refs, n_cast):
    hn_ref, col_ref, row_ref, meta_ref = refs[n_cast:n_cast + 4]
    for src_ref, dst_ref in zip(refs[:n_cast], refs[n_cast + 4:]):
        dst_ref[...] = src_ref[...].astype(BF16)
    ne = MOE_GROUPS * MOE_PER_GROUP
    hb = _rms(x_ref[...], g_ref[...]).astype(BF16)
    hn_ref[...] = hb
    lt = (_dot(hb, rw_ref[...]) + rb_ref[...]).T[:ROUTE_ROWS]
    tm = lt.shape[1]
    ridx = lax.broadcasted_iota(jnp.int32, lt.shape, 0)
    lg = jnp.where((ridx >= ne) & (ridx < ne + MOE_GROUPS), lt, -jnp.inf)
    g_max = jnp.max(lg, axis=0, keepdims=True)
    g_idx = jnp.min(jnp.where(lg == g_max, ridx, ROUTE_ROWS), axis=0, keepdims=True) - ne
    g_p = 1.0 / jnp.sum(jnp.exp(lg - g_max), axis=0, keepdims=True)
    first = g_idx * MOE_PER_GROUP
    l1 = jnp.where((ridx >= first) & (ridx < first + MOE_PER_GROUP), lt, -jnp.inf)
    m1 = jnp.max(l1, axis=0, keepdims=True)
    e1 = jnp.min(jnp.where(l1 == m1, ridx, ROUTE_ROWS), axis=0, keepdims=True)
    l2 = jnp.where(ridx == e1, -jnp.inf, l1)
    m2 = jnp.max(l2, axis=0, keepdims=True)
    e2 = jnp.min(jnp.where(l2 == m2, ridx, ROUTE_ROWS), axis=0, keepdims=True)
    z = jnp.sum(jnp.exp(l1 - m1), axis=0, keepdims=True)
    p1 = 1.0 / z
    p2 = jnp.exp(m2 - m1) / z
    w1 = g_p * p1 / (p1 + p2)
    w2 = g_p * p2 / (p1 + p2)
    hit1 = ridx == e1
    hit2 = ridx == e2
    routed = (jnp.where(hit1, 1.0, 0.0) + jnp.where(hit2, 1.0, 0.0)).astype(BF16)
    rank = _dot(routed, tri_ref[...])
    cnt = _dot(routed, sel_ref[...])
    pad = jnp.floor((cnt + (SLOT_ALIGN - 1)) * (1.0 / SLOT_ALIGN)) * SLOT_ALIGN
    ridx_c = lax.broadcasted_iota(jnp.int32, cnt.shape, 0)
    incl = pad
    shift = 1
    while shift < ne:
        incl = incl + jnp.where(ridx_c >= shift, pltpu.roll(incl, shift, axis=0), 0.0)
        shift *= 2
    off = incl - pad
    slot = rank + _dot(off.astype(BF16), selt_ref[...])
    s1 = jnp.sum(jnp.where(hit1, slot, 0.0), axis=0, keepdims=True)
    s2 = jnp.sum(jnp.where(hit2, slot, 0.0), axis=0, keepdims=True)
    rows = jnp.concatenate([s1, s2, w1, w2, jnp.zeros((row_ref.shape[0] - 4, tm), F32)], axis=0)
    row_ref[...] = rows
    col_ref[...] = jnp.concatenate([rows, jnp.zeros((ROUTE_LANES - rows.shape[0], tm), F32)], axis=0).T
    meta_ref[0] = cnt.astype(jnp.int32)
    meta_ref[1] = off.astype(jnp.int32)


def _router(x, g, rw, rb, tm, sub, layer, to_cast):
    n, d = x.shape
    nt = n // tm
    cast_in, cast_out, cast_shape = [], [], []
    for w in to_cast:
        per_step = w.shape[1] // nt
        assert per_step * nt == w.shape[1]
        cast_in.append(pl.BlockSpec((None, per_step) + w.shape[2:], lambda i: (layer, i, 0, 0)))
        cast_out.append(pl.BlockSpec((per_step,) + w.shape[2:], lambda i: (i, 0, 0)))
        cast_shape.append(jax.ShapeDtypeStruct(w.shape[1:], BF16))
    assert (2 * sub) // SLOT_ALIGN + MOE_GROUPS * MOE_PER_GROUP < 256
    tok = jnp.arange(tm)
    blocks = tok // sub
    tri = ((blocks[:, None] == blocks[None, :]) & (tok[:, None] < tok[None, :])).astype(BF16)
    sel = (blocks[:, None] == jnp.arange(ROUTE_LANES)[None, :]).astype(BF16)
    out = pl.pallas_call(
        functools.partial(_router_kernel, n_cast=len(to_cast)),
        grid=(nt,),
        in_specs=[pl.BlockSpec((tm, d), lambda i: (i, 0)), _full((1, d)), _full(rw.shape),
                  _full(rb.shape), _full(tri.shape), _full(sel.shape), _full(sel.T.shape)] + cast_in,
        out_specs=[pl.BlockSpec((tm, d), lambda i: (i, 0)),
                   pl.BlockSpec((tm, ROUTE_LANES), lambda i: (i, 0)),
                   pl.BlockSpec((8, tm), lambda i: (0, i)),
                   pl.BlockSpec((None, 2, ROUTE_ROWS, ROUTE_LANES), lambda i: (i, 0, 0, 0))] + cast_out,
        out_shape=[jax.ShapeDtypeStruct((n, d), BF16), jax.ShapeDtypeStruct((n, ROUTE_LANES), F32),
                   jax.ShapeDtypeStruct((8, n), F32),
                   jax.ShapeDtypeStruct((nt, 2, ROUTE_ROWS, ROUTE_LANES), jnp.int32)] + cast_shape,
        compiler_params=_params("arbitrary"),
        name="moe_router",
    )(x, g, rw, rb, tri, sel, sel.T, *to_cast)
    return out[:4], out[4:]


def _moe_routed_kernel(cnt_ref, off_ref, trips_ref, hn_ref, col_ref, row_ref, x_ref, gfin_ref,
                       wg_ref, wu_ref, wd_ref, y_ref, xs_ref, ys_ref, gs_ref,
                       *, sub, mcs, n_gather, n_scatter, final_norm):
    i = pl.program_id(0)
    eg = pl.program_id(1)
    eps = wg_ref.shape[0]
    ne = pl.num_programs(1) * eps
    n_sub = xs_ref.shape[0]
    dump = n_gather
    rows_of = lambda s: slice(s * sub, (s + 1) * sub)

    @pl.when(eg == 0)
    def _():
        y_ref[...] = x_ref[...]
        ys_ref[...] = jnp.zeros_like(ys_ref)
        sid = lax.broadcasted_iota(jnp.int32, (n_gather, sub), 0).astype(F32)
        for s in range(n_sub):
            s1, s2, w1, w2 = (row_ref[r:r + 1, rows_of(s)] for r in range(4))
            h1 = s1 == sid
            h2 = s2 == sid
            onehot = (jnp.where(h1, 1.0, 0.0) + jnp.where(h2, 1.0, 0.0)).astype(BF16)
            xs_ref[s, 0:n_gather, :] = _dot(onehot, hn_ref[rows_of(s), :]).astype(BF16)
            xs_ref[s, dump:dump + mcs, :] = jnp.zeros((mcs, xs_ref.shape[2]), BF16)
            gs_ref[s, 0:n_gather, :] = jnp.sum(
                jnp.where(h1, w1, 0.0) + jnp.where(h2, w2, 0.0), axis=-1, keepdims=True)
            gs_ref[s, dump:dump + mcs, :] = jnp.zeros((mcs, 1), F32)

    for r in range(eps):
        e = eg * eps + r
        counts = [cnt_ref[(i * n_sub + s) * ne + e] for s in range(n_sub)]
        firsts = [off_ref[(i * n_sub + s) * ne + e] for s in range(n_sub)]

        def chunk(k, carry, r=r, counts=counts, firsts=firsts):
            rows = [pl.multiple_of(jnp.where(k * mcs < counts[s], firsts[s] + k * mcs, dump), SLOT_ALIGN)
                    for s in range(n_sub)]
            xg = jnp.concatenate([xs_ref[s, pl.ds(rows[s], mcs), :] for s in range(n_sub)], axis=0)
            gate = jnp.concatenate([gs_ref[s, pl.ds(rows[s], mcs), :] for s in range(n_sub)], axis=0)
            a = _dot(xg, wg_ref[r])
            b = _dot(xg, wu_ref[r])
            hid = ((a * jax.nn.sigmoid(a)) * b * gate).astype(BF16)
            out = _dot(hid, wd_ref[r]).astype(BF16)
            for s in range(n_sub):
                ys_ref[s, pl.ds(rows[s], mcs), :] = out[s * mcs:(s + 1) * mcs]
            return carry

        lax.fori_loop(0, trips_ref[i * ne + e], chunk, 0)

    @pl.when(eg == pl.num_programs(1) - 1)
    def _():
        sid = lax.broadcasted_iota(jnp.int32, (sub, n_scatter), 1).astype(F32)
        for s in range(n_sub):
            onehot = (jnp.where(col_ref[rows_of(s), 0:1] == sid, 1.0, 0.0)
                      + jnp.where(col_ref[rows_of(s), 1:2] == sid, 1.0, 0.0)).astype(BF16)
            out = y_ref[rows_of(s), :] + _dot(onehot, ys_ref[s, 0:n_scatter, :])
            y_ref[rows_of(s), :] = _rms(out, gfin_ref[...]) if final_norm else out


def _round_up(a, b):
    return -(-a // b) * b


def _moe_routed(cnt, off, hn, col, row, wg, wu, wd, x, g_final, tm, sub):
    n, d = x.shape
    ne, _, de = wg.shape
    n_sub = tm // sub
    mcs = _round_up(2 * sub * 2 // ne, SLOT_ALIGN)
    most = 2 * sub + (ne - 1) * SLOT_ALIGN
    n_scatter = _round_up(most, 128)
    n_gather = _round_up(most + mcs, 128)
    trips = jnp.max((cnt + mcs - 1) // mcs, axis=1).reshape(-1)
    eps = 2 if ne % 2 == 0 else 1
    final_norm = g_final is not None
    if g_final is None:
        g_final = jnp.ones((1, d), F32)
    kern = functools.partial(_moe_routed_kernel, sub=sub, mcs=mcs, n_gather=n_gather, n_scatter=n_scatter,
                             final_norm=final_norm)
    tile = lambda i, e, *_: (i, 0)
    expert = lambda i, e, *_: (e, 0, 0)
    return pl.pallas_call(
        kern,
        grid_spec=pltpu.PrefetchScalarGridSpec(
            num_scalar_prefetch=3,
            grid=(n // tm, ne // eps),
            in_specs=[pl.BlockSpec((tm, d), tile), pl.BlockSpec((tm, ROUTE_LANES), tile),
                      pl.BlockSpec((8, tm), lambda i, e, *_: (0, i)), pl.BlockSpec((tm, d), tile),
                      pl.BlockSpec((1, d), lambda i, e, *_: (0, 0)),
                      pl.BlockSpec((eps, d, de), expert), pl.BlockSpec((eps, d, de), expert),
                      pl.BlockSpec((eps, de, d), expert)],
            out_specs=pl.BlockSpec((tm, d), tile),
            scratch_shapes=[pltpu.VMEM((n_sub, n_gather + mcs, d), BF16),
                            pltpu.VMEM((n_sub, n_gather + mcs, d), BF16),
                            pltpu.VMEM((n_sub, n_gather + mcs, 1), F32)],
        ),
        out_shape=jax.ShapeDtypeStruct((n, d), F32),
        compiler_params=_params("arbitrary", "arbitrary"),
        name="moe_routed",
    )(cnt.reshape(-1), off.reshape(-1), trips, hn, col, row, x, g_final, wg, wu, wd)


def _moe(x, g, rw, rb, weights, layer, g_final, tm):
    sub = min(tm, MOE_SUB)
    n_sub = tm // sub
    assert tm % sub == 0 and n_sub <= ROUTE_LANES
    to_cast = [w for w in weights if w.dtype != BF16]
    if to_cast and any(w.shape[1] % (x.shape[0] // tm) for w in to_cast):
        weights, to_cast = [w[layer].astype(BF16) for w in weights], []
    (hn, col, row, meta), cast = _router(x, g, rw, rb, tm, sub, layer, to_cast)
    wg, wu, wd = cast if to_cast else weights
    ne = wg.shape[0]
    cnt, off = (meta[:, r, :ne, :n_sub].transpose(0, 2, 1) for r in range(2))
    return _moe_routed(cnt, off, hn, col, row, wg, wu, wd, x, g_final, tm, sub), (wg, wu, wd)


def _tile(n, pref):
    t = min(n, pref)
    assert n % t == 0, (n, t)
    return t


def kernel(x_prompt, x_sample, state_pool, cache_k, cache_v, page_table, norm_mix, norm_ffn, norm_final, pool_w_in, pool_w_grp, pool_scale, pool_w_out, attn_w_qkv, attn_w_out, gmlp_w_in, gmlp_ln_g, gmlp_ln_b, gmlp_w_s, gmlp_b_s, gmlp_w_out, moe_rg_w, moe_rg_b, moe_re_w, moe_re_b, moe_w_gate, moe_w_up, moe_w_down):
    batch, seq, d = x_prompt.shape
    dec_b, n_t, _ = x_sample.shape
    depth = norm_mix.shape[0]
    n_heads = N_HEADS
    hd = d // n_heads
    n_pages = page_table.shape[1]
    past_len = n_pages * PAGE_SIZE
    assert seq % MOBA_BLOCK == 0 and seq % GMLP_CHUNK == 0 and n_t <= GMLP_CHUNK
    assert past_len % GMLP_CHUNK == 0 and past_len >= POOL_STATE

    n_p, n_s = batch * seq, dec_b * n_t
    xp = x_prompt.reshape(n_p, d)
    xs = x_sample.transpose(1, 0, 2).reshape(n_s, d)
    tp = _tile(seq, 512)
    row = lambda a: a.reshape(1, -1)

    def to_batch_major(a):
        return a.reshape((n_t, dec_b) + a.shape[1:]).swapaxes(0, 1)

    pool_p, pool_s, k_p, v_p, k_s, v_s, gv_p, gv_s = [], [], [], [], [], [], [], []
    for i in range(depth):
        kind, j = i % N_MIXERS, i // N_MIXERS
        g_mix = row(norm_mix[i])
        if kind == 0:
            w_in, w_grp, w_out = (a.astype(BF16) for a in (pool_w_in[j], pool_w_grp[j], pool_w_out[j]))
            scale = row(pool_scale[j])
            xp, st = _pool_prompt(xp, g_mix, w_in, w_grp, scale, w_out, batch, seq, tp)
            pool_p.append(st)
            xs, st = _pool_sample(xs, state_pool, j, g_mix, w_in, w_grp, scale, w_out, n_t, past_len)
            pool_s.append(st)
        elif kind == 1:
            w_qkv, w_out = attn_w_qkv[j].astype(BF16), attn_w_out[j].astype(BF16)
            qp, kp, vp = _norm_matmul(xp, g_mix, w_qkv, (BF16, F32, F32), _tile(n_p, 1024))
            qs, ks, vs = _norm_matmul(xs, g_mix, w_qkv, (F32, F32, F32), _tile(n_s, 512))
            ap = _moba_prompt(qp, kp, vp, batch, seq, n_heads)
            a_s = _moba_sample(qs, ks, vs, cache_k, cache_v, j, page_table, n_t, n_heads)
            xp = _matmul_residual(ap, w_out, xp, _tile(n_p, 1024))
            xs = _matmul_residual(a_s, w_out, xs, _tile(n_s, 512))
            k_p.append(kp.reshape(batch, seq, n_heads, hd))
            v_p.append(vp.reshape(batch, seq, n_heads, hd))
            k_s.append(to_batch_major(ks).reshape(dec_b, n_t, n_heads, hd))
            v_s.append(to_batch_major(vs).reshape(dec_b, n_t, n_heads, hd))
        else:
            w_in, w_out = gmlp_w_in[j].astype(BF16), gmlp_w_out[j].astype(BF16)
            ln_g, ln_b = row(gmlp_ln_g[j]), row(gmlp_ln_b[j])
            gc = w_out.shape[0] // GMLP_GROUPS
            bias = jnp.broadcast_to(gmlp_b_s[j].T[:, :, None], (GMLP_CHUNK, GMLP_GROUPS, gc))
            xp, vn_p = _gmlp_prompt(xp, g_mix, w_in, ln_g, ln_b, gmlp_w_s[j],
                                    bias.reshape(GMLP_CHUNK, GMLP_GROUPS * gc), w_out, tp)
            xs, vn_s = _gmlp_sample(xs, g_mix, w_in, ln_g, ln_b,
                                    gmlp_w_s[j][:, :n_t, :n_t].reshape(-1),
                                    gmlp_b_s[j][:, :n_t].reshape(-1), w_out, n_t)
            gv_p.append(vn_p.reshape(batch, seq, -1))
            gv_s.append(to_batch_major(vn_s))

        g_ffn = row(norm_ffn[i])
        ne = moe_re_w.shape[2]
        fill = jnp.zeros((d, ROUTE_LANES - ne - MOE_GROUPS), F32)
        rw = jnp.concatenate([moe_re_w[i], moe_rg_w[i], fill], axis=1).astype(BF16)
        rb = jnp.concatenate([moe_re_b[i], moe_rg_b[i], fill[0]]).reshape(1, ROUTE_LANES)
        g_fin = row(norm_final) if i == depth - 1 else None
        xp, experts = _moe(xp, g_ffn, rw, rb, (moe_w_gate, moe_w_up, moe_w_down), i, g_fin, _tile(n_p, 1024))
        xs, _ = _moe(xs, g_ffn, rw, rb, experts, i, g_fin, _tile(n_s, 512))

    y_p = xp.reshape(batch, seq, d)
    y_s = to_batch_major(xs)
    return (y_p, y_s, jnp.stack(pool_p), jnp.stack(pool_s), jnp.stack(k_p), jnp.stack(v_p),
            jnp.stack(k_s), jnp.stack(v_s), jnp.stack(gv_p), jnp.stack(gv_s))
```
